```python
import jax
import jax.numpy as jnp
from jax import lax

D_MODEL = 1024
BATCH = 2
SEQ = 8192
DEPTH = 1

HEAD_DIM = 64
WIN_Q_HEADS = 8
WIN_KV_HEADS = 2
WIN_HALF = 128
DIL_SLOTS = 8
DIL_PAIRS = ((128, 1), (512, 4), (2048, 16))
N_DIL = len(DIL_PAIRS)
ROT_DIM = HEAD_DIM // 4
ROPE_THETA = 500000.0
MEM_LEN = 256
X_HEADS = 4
X_HEAD_DIM = D_MODEL // X_HEADS
D_FF = 2816
CONV_WIDTH = 3
WIN_WIDTH = WIN_Q_HEADS * HEAD_DIM
DIL_WIDTH = DIL_SLOTS * HEAD_DIM
MIX_WIDTH = WIN_WIDTH + DIL_WIDTH
A_Q = WIN_WIDTH
A_KV = WIN_KV_HEADS * HEAD_DIM
B_QKV = N_DIL * DIL_WIDTH
IN_WIDTH = A_Q + 2 * A_KV + 3 * B_QKV
SPLITS = (A_Q, A_Q + A_KV, A_Q + 2 * A_KV, A_Q + 2 * A_KV + B_QKV, A_Q + 2 * A_KV + 2 * B_QKV)
DEEPNORM_ALPHA = (2 * DEPTH) ** 0.25
DEEPNORM_BETA = (8 * DEPTH) ** -0.25
LN_EPS = 1e-5
NEG_INF = -1e30
POS_OFFSET_MAX = 4096

kernel_name = 'hymba_window_dilated_deepnorm_encoder'


def layer_norm(x, g, b):
    xf = x.astype(jnp.float32)
    mu = jnp.mean(xf, -1, keepdims=True)
    var = jnp.mean(jnp.square(xf - mu), -1, keepdims=True)
    return ((xf - mu) * lax.rsqrt(var + LN_EPS) * g + b).astype(x.dtype)


def rms_norm(x, g):
    xf = x.astype(jnp.float32)
    return (xf * lax.rsqrt(jnp.mean(jnp.square(xf), -1, keepdims=True) + LN_EPS) * g).astype(x.dtype)


def partial_rope(t, positions):
    half = ROT_DIM // 2
    inv_freq = ROPE_THETA ** (-jnp.arange(0, ROT_DIM, 2, dtype=jnp.float32) / ROT_DIM)
    ang = positions.astype(jnp.float32)[:, :, None] * inv_freq
    cos = jnp.cos(ang)[:, :, None, :]
    sin = jnp.sin(ang)[:, :, None, :]
    tr = t[..., :ROT_DIM].astype(jnp.float32)
    t1, t2 = tr[..., :half], tr[..., half:]
    rot = jnp.concatenate([t1 * cos - t2 * sin, t2 * cos + t1 * sin], -1).astype(t.dtype)
    return jnp.concatenate([rot, t[..., ROT_DIM:]], -1)


def banded_attention(q, k, v, n_side, sink=None):
    bt, seq_len, hkv, grp, dh = q.shape
    blk = n_side
    nb = -(-seq_len // blk)
    pad = nb * blk - seq_len
    qb = jnp.pad(q, ((0, 0), (0, pad), (0, 0), (0, 0), (0, 0))).reshape(bt, nb, blk, hkv, grp, dh)

    def neighbourhood(t):
        tp = jnp.pad(t, ((0, 0), (blk, blk + pad), (0, 0), (0, 0))).reshape(bt, nb + 2, blk, hkv, dh)
        return jnp.concatenate([tp[:, :-2], tp[:, 1:-1], tp[:, 2:]], axis=2)

    kw = neighbourhood(k)
    vw = neighbourhood(v)
    s = jnp.einsum('bnqhgd,bnkhd->bnhgqk', qb, kw).astype(jnp.float32) * (dh ** -0.5)
    qi = jnp.arange(blk)[:, None]
    kj = jnp.arange(3 * blk)[None, :]
    kabs = jnp.arange(nb)[:, None, None] * blk + kj[None] - blk
    mask = (jnp.abs(kj - blk - qi) <= n_side)[None] & (kabs >= 0) & (kabs < seq_len)
    mask = mask[None, :, None, None]
    s = jnp.where(mask, s, NEG_INF)
    m = jnp.max(s, -1)
    if sink is not None:
        sink_f = sink.astype(jnp.float32)[None, None, :, :, None]
        m = jnp.maximum(m, sink_f)
    p = jnp.where(mask, jnp.exp(s - m[..., None]), 0.0)
    denom = jnp.sum(p, -1)
    if sink is not None:
        denom = denom + jnp.exp(sink_f - m)
    o = jnp.einsum('bnhgqk,bnkhd->bnqhgd', p, vw.astype(jnp.float32))
    o = o / jnp.moveaxis(denom, -1, 2)[..., None]
    o = o.astype(q.dtype).reshape(bt, nb * blk, hkv, grp, dh)[:, :seq_len]
    lse = jnp.moveaxis(m + jnp.log(denom), -1, 2).reshape(bt, nb * blk, hkv, grp)[:, :seq_len]
    return o, lse


def to_residue(t, r):
    b, s = t.shape[:2]
    t = t.reshape((b, s // r, r) + t.shape[2:])
    t = jnp.moveaxis(t, 2, 1)
    return t.reshape((b * r, s // r) + t.shape[3:])


def from_residue(t, r, batch):
    t = t.reshape((batch, r) + t.shape[1:])
    t = jnp.moveaxis(t, 1, 2)
    return t.reshape((batch, t.shape[1] * r) + t.shape[3:])


def parallel_mixer(h, positions, w_in, attn_sink, g_win, g_dil, w_out):
    b, s, _ = h.shape
    z = h @ w_in
    qa, ka, va, qb, kb, vb = jnp.split(z, SPLITS, axis=-1)
    qa = partial_rope(qa.reshape(b, s, WIN_Q_HEADS, HEAD_DIM), positions)
    ka = partial_rope(ka.reshape(b, s, WIN_KV_HEADS, HEAD_DIM), positions)
    va = va.reshape(b, s, WIN_KV_HEADS, HEAD_DIM)
    qa = qa.reshape(b, s, WIN_KV_HEADS, WIN_Q_HEADS // WIN_KV_HEADS, HEAD_DIM)
    out_a, _ = banded_attention(qa, ka, va, WIN_HALF, attn_sink.reshape(WIN_KV_HEADS, -1))
    out_a = out_a.reshape(b, s, WIN_WIDTH)
    n_heads_b = N_DIL * DIL_SLOTS
    qb = partial_rope(qb.reshape(b, s, n_heads_b, HEAD_DIM), positions).reshape(b, s, N_DIL, DIL_SLOTS, HEAD_DIM)
    kb = partial_rope(kb.reshape(b, s, n_heads_b, HEAD_DIM), positions).reshape(b, s, N_DIL, DIL_SLOTS, HEAD_DIM)
    vb = vb.reshape(b, s, N_DIL, DIL_SLOTS, HEAD_DIM)
    outs = []
    lses = []
    for gi, (window, dil) in enumerate(DIL_PAIRS):
        n_side = window // (2 * dil)
        o, lse = banded_attention(to_residue(qb[:, :, gi], dil)[:, :, :, None],
                                  to_residue(kb[:, :, gi], dil),
                                  to_residue(vb[:, :, gi], dil), n_side)
        outs.append(from_residue(o[:, :, :, 0], dil, b))
        lses.append(from_residue(lse[..., 0], dil, b))
    wts = jax.nn.softmax(jnp.stack(lses), axis=0)[..., None]
    out_b = jnp.sum(wts * jnp.stack(outs).astype(jnp.float32), axis=0).astype(h.dtype).reshape(b, s, DIL_WIDTH)
    mixed = jnp.concatenate([rms_norm(out_a, g_win), rms_norm(out_b, g_dil)], -1)
    return mixed @ w_out


def memory_cross_attention(h, mem_n, w_q, w_k, w_v, w_o):
    b, s, _ = h.shape
    m_len = mem_n.shape[1]
    q = (h @ w_q).reshape(b, s, X_HEADS, X_HEAD_DIM)
    k = (mem_n @ w_k).reshape(b, m_len, X_HEADS, X_HEAD_DIM)
    v = (mem_n @ w_v).reshape(b, m_len, X_HEADS, X_HEAD_DIM)
    sc = jnp.einsum('bshd,bmhd->bhsm', q, k).astype(jnp.float32) * (X_HEAD_DIM ** -0.5)
    p = jax.nn.softmax(sc, axis=-1)
    o = jnp.einsum('bhsm,bmhd->bshd', p, v.astype(jnp.float32)).astype(h.dtype)
    return o.reshape(b, s, D_MODEL) @ w_o


def conv_glu(h, w_gate, w_up, conv_w, conv_b, w_down):
    s = h.shape[1]
    g = h @ w_gate
    half = CONV_WIDTH // 2
    gp = jnp.pad(g, ((0, 0), (half, half), (0, 0)))
    g = sum(gp[:, j:j + s] * conv_w[j] for j in range(CONV_WIDTH)) + conv_b
    return (jax.nn.gelu(g, approximate=False) * (h @ w_up)) @ w_down


def setup_inputs(seed: int = 0) -> dict:
    key = jax.random.key(seed)
    keys = list(jax.random.split(key, 40))
    f32 = jnp.float32
    d = D_MODEL
    nl = DEPTH
    beta = DEEPNORM_BETA

    def normal(shape, scale):
        return jax.random.normal(keys.pop(), shape, f32) * scale

    def gain(shape):
        return 1.0 + normal(shape, 0.02)

    def bias(shape):
        return normal(shape, 0.02)

    in_scale = jnp.concatenate([jnp.ones((A_Q + A_KV,), f32), jnp.full((A_KV,), beta, f32),
                                jnp.ones((2 * B_QKV,), f32), jnp.full((B_QKV,), beta, f32)])
    x = normal((BATCH, SEQ, d), 1.0)
    mem = normal((BATCH, MEM_LEN, d), 1.0)
    positions = jnp.arange(SEQ, dtype=jnp.int32)[None, :] + jax.random.randint(
        keys.pop(), (BATCH, 1), 0, POS_OFFSET_MAX, dtype=jnp.int32)
    return {
        'x': x,
        'mem': mem,
        'positions': positions,
        'ln_in_g': gain((d,)),
        'ln_in_b': bias((d,)),
        'w_in': normal((nl, d, IN_WIDTH), d ** -0.5) * in_scale,
        'attn_sink': normal((nl, WIN_Q_HEADS), 0.5),
        'g_win': gain((nl, WIN_WIDTH)),
        'g_dil': gain((nl, DIL_WIDTH)),
        'w_mix_out': normal((nl, MIX_WIDTH, d), MIX_WIDTH ** -0.5 * beta),
        'ln1_g': gain((nl, d)),
        'ln1_b': bias((nl, d)),
        'mem_ln_g': gain((nl, d)),
        'mem_ln_b': bias((nl, d)),
        'w_xq': normal((nl, d, d), d ** -0.5),
        'w_xk': normal((nl, d, d), d ** -0.5),
        'w_xv': normal((nl, d, d), d ** -0.5 * beta),
        'w_xo': normal((nl, d, d), d ** -0.5 * beta),
        'ln2_g': gain((nl, d)),
        'ln2_b': bias((nl, d)),
        'w_gate': normal((nl, d, D_FF), d ** -0.5),
        'w_up': normal((nl, d, D_FF), d ** -0.5 * beta),
        'conv_w': normal((nl, CONV_WIDTH, D_FF), CONV_WIDTH ** -0.5),
        'conv_b': bias((nl, D_FF)),
        'w_down': normal((nl, D_FF, d), D_FF ** -0.5 * beta),
        'ln3_g': gain((nl, d)),
        'ln3_b': bias((nl, d)),
    }


def reference(x, mem, positions, ln_in_g, ln_in_b, w_in, attn_sink, g_win, g_dil, w_mix_out,
              ln1_g, ln1_b, mem_ln_g, mem_ln_b, w_xq, w_xk, w_xv, w_xo, ln2_g, ln2_b,
              w_gate, w_up, conv_w, conv_b, w_down, ln3_g, ln3_b):
    h = layer_norm(x, ln_in_g, ln_in_b)
    for l in range(DEPTH):
        mix = parallel_mixer(h, positions, w_in[l], attn_sink[l], g_win[l], g_dil[l], w_mix_out[l])
        h = layer_norm(DEEPNORM_ALPHA * h + mix, ln1_g[l], ln1_b[l])
        mem_n = layer_norm(mem, mem_ln_g[l], mem_ln_b[l])
        xa = memory_cross_attention(h, mem_n, w_xq[l], w_xk[l], w_xv[l], w_xo[l])
        h = layer_norm(DEEPNORM_ALPHA * h + xa, ln2_g[l], ln2_b[l])
        ff = conv_glu(h, w_gate[l], w_up[l], conv_w[l], conv_b[l], w_down[l])
        h = layer_norm(DEEPNORM_ALPHA * h + ff, ln3_g[l], ln3_b[l])
    return h
```

```python
import functools

import numpy as np
import jax
import jax.numpy as jnp
from jax import lax
from jax.experimental import pallas as pl
from jax.experimental.pallas import tpu as pltpu

D_MODEL = 1024
HEAD_DIM = 64
WIN_Q_HEADS = 8
WIN_KV_HEADS = 2
WIN_HALF = 128
DIL_SLOTS = 8
DIL_PAIRS = ((128, 1), (512, 4), (2048, 16))
N_DIL = len(DIL_PAIRS)
ROT_DIM = HEAD_DIM // 4
ROPE_THETA = 500000.0
X_HEADS = 4
X_HEAD_DIM = D_MODEL // X_HEADS
D_FF = 2816
CONV_WIDTH = 3
WIN_WIDTH = WIN_Q_HEADS * HEAD_DIM
DIL_WIDTH = DIL_SLOTS * HEAD_DIM
A_Q = WIN_WIDTH
A_KV = WIN_KV_HEADS * HEAD_DIM
B_QKV = N_DIL * DIL_WIDTH
DEPTH = 1
DEEPNORM_ALPHA = (2 * DEPTH) ** 0.25
LN_EPS = 1e-5
NEG_INF = -1e30

LANES = 128
DIL_SIDE = 64
VMEM_LIMIT = 56 * 1024 * 1024

F32 = jnp.float32
BF16 = jnp.bfloat16


def _ln(x, g, b):
    mu = jnp.mean(x, -1, keepdims=True)
    xc = x - mu
    var = jnp.mean(xc * xc, -1, keepdims=True)
    return xc * lax.rsqrt(var + LN_EPS) * g + b


def _rms(x, g):
    return x * lax.rsqrt(jnp.mean(x * x, -1, keepdims=True) + LN_EPS) * g


def _dot(a, b):
    return jnp.dot(a, b, preferred_element_type=F32)


def _dot_nt(a, b):
    return lax.dot_general(a, b, (((1,), (1,)), ((), ())), preferred_element_type=F32)


def _const_spec(shape):
    nd = len(shape)
    return pl.BlockSpec(shape, lambda *_: (0,) * nd, pipeline_mode=pl.Buffered(1))


def _params(sem):
    return pltpu.CompilerParams(dimension_semantics=sem, vmem_limit_bytes=VMEM_LIMIT)


def _rope_tab_kernel(pos_ref, invf_ref, cos_ref, sin_ref):
    ang = pos_ref[...].astype(F32) * invf_ref[...]
    cos_ref[...] = jnp.cos(ang)
    sin_ref[...] = jnp.sin(ang)


def _rope_tables(positions):
    b, s = positions.shape
    half = ROT_DIM // 2
    inv_freq = ROPE_THETA ** (-jnp.arange(0, ROT_DIM, 2, dtype=F32) / ROT_DIM)
    pos_rep = jnp.repeat(positions.reshape(-1), half).reshape(-1, LANES)
    invf = jnp.tile(inv_freq, LANES // half).reshape(1, LANES)
    cos, sin = pl.pallas_call(
        _rope_tab_kernel,
        out_shape=(jax.ShapeDtypeStruct(pos_rep.shape, F32),) * 2,
        name="rope_tables",
    )(pos_rep, invf)
    cos = cos.reshape(b, s, half)
    sin = sin.reshape(b, s, half)
    ones = jnp.ones((b, s, HEAD_DIM - ROT_DIM), F32)
    zeros = jnp.zeros((b, s, HEAD_DIM - ROT_DIM), F32)
    zh = jnp.zeros((b, s, half), F32)
    c_tab = jnp.concatenate([cos, cos, ones] * 2, -1)
    sa_tab = jnp.concatenate([sin, zh, zeros] * 2, -1)
    sb_tab = jnp.concatenate([zh, sin, zeros] * 2, -1)
    return c_tab, sa_tab, sb_tab


def _qkv_kernel(segs, x_ref, g_ref, b_ref, w_ref, c_ref, sa_ref, sb_ref, h_ref, *out_refs):
    h = _ln(x_ref[...], g_ref[...], b_ref[...])
    h_ref[...] = h
    hb = h.astype(BF16)
    c_tab = c_ref[...]
    sa_tab = sa_ref[...]
    sb_tab = sb_ref[...]
    for (c0, width, rope), o_ref in zip(segs, out_refs):
        z = _dot(hb, w_ref[:, c0:c0 + width])
        if rope:
            for j in range(width // LANES):
                zc = z[:, j * LANES:(j + 1) * LANES]
                up = pltpu.roll(zc, LANES - ROT_DIM // 2, 1)
                dn = pltpu.roll(zc, ROT_DIM // 2, 1)
                o_ref[:, j * LANES:(j + 1) * LANES] = (zc * c_tab - up * sa_tab + dn * sb_tab).astype(BF16)
        else:
            o_ref[...] = z.astype(BF16)


def _qkv_proj(x2, ln_g, ln_b, w_all, segs, c_tab, sa_tab, sb_tab, tm):
    t = x2.shape[0]
    wtot = w_all.shape[1]
    row = lambda w: pl.BlockSpec((tm, w), lambda i: (i, 0))
    out_shape = [jax.ShapeDtypeStruct((t, D_MODEL), F32)]
    out_specs = [row(D_MODEL)]
    for _, width, _ in segs:
        out_shape.append(jax.ShapeDtypeStruct((t, width), BF16))
        out_specs.append(row(width))
    return pl.pallas_call(
        functools.partial(_qkv_kernel, segs),
        grid=(t // tm,),
        in_specs=[row(D_MODEL), _const_spec((1, D_MODEL)), _const_spec((1, D_MODEL)),
                  _const_spec((D_MODEL, wtot)), row(LANES), row(LANES), row(LANES)],
        out_specs=out_specs,
        out_shape=out_shape,
        compiler_params=_params(("parallel",)),
        name="qkv_proj",
    )(x2, ln_g, ln_b, w_all, c_tab, sa_tab, sb_tab)


def _lane_lo(shape):
    return (lax.broadcasted_iota(jnp.int32, shape, len(shape) - 1) % LANES) < HEAD_DIM


def _band_mask(nq, nk, side, key0, seq_len):
    qi = lax.broadcasted_iota(jnp.int32, (nq, nk), 0)
    kj = lax.broadcasted_iota(jnp.int32, (nq, nk), 1)
    rel = kj - side - qi
    kabs = key0 + kj
    return (rel >= -side) & (rel <= side) & (kabs >= 0) & (kabs < seq_len)


def _win_kernel(tq, seq_len, sink_ref, q_ref, kp_ref, km_ref, kn_ref, vp_ref, vm_ref, vn_ref, g_ref, o_ref,
                kcat, vcat):
    i = pl.program_id(1)
    blk = WIN_HALF
    kcat[0:blk] = kp_ref[0]
    kcat[blk:blk + tq] = km_ref[0]
    kcat[blk + tq:] = kn_ref[0]
    vcat[0:blk] = vp_ref[0]
    vcat[blk:blk + tq] = vm_ref[0]
    vcat[blk + tq:] = vn_ref[0]
    grp = WIN_Q_HEADS // WIN_KV_HEADS
    lo_q = _lane_lo((blk, LANES))
    lo_o = _lane_lo((blk, LANES))
    gain = g_ref[...]
    for j in range(tq // blk):
        r0 = j * blk
        mask = _band_mask(blk, 3 * blk, blk, i * tq + r0 - blk, seq_len)
        mask = jnp.concatenate([mask] * grp, 0)
        qblk = q_ref[0, r0:r0 + blk, :]
        pairs = []
        for hk in range(WIN_KV_HEADS):
            kk = kcat[r0:r0 + 3 * blk, hk * LANES:(hk + 1) * LANES]
            vv = vcat[r0:r0 + 3 * blk, hk * LANES:(hk + 1) * LANES]
            qs, sinks = [], []
            for t in range(grp):
                hq = hk * grp + t
                q2 = qblk[:, (hq // 2) * LANES:(hq // 2 + 1) * LANES]
                keep = lo_q if hq % 2 == 0 else jnp.logical_not(lo_q)
                qs.append(jnp.where(keep, q2, jnp.zeros_like(q2)))
                sinks.append(jnp.full((blk, 1), sink_ref[hq], F32))
            s = _dot_nt(jnp.concatenate(qs, 0), kk)
            sink = jnp.concatenate(sinks, 0)
            s = jnp.where(mask, s, NEG_INF)
            m = jnp.maximum(jnp.max(s, -1, keepdims=True), sink)
            p = jnp.exp(s - m)
            denom = jnp.sum(p, -1, keepdims=True) + jnp.exp(sink - m)
            o = _dot(p.astype(BF16), vv) / denom
            for t in range(0, grp, 2):
                pairs.append(jnp.where(lo_o, o[t * blk:(t + 1) * blk], o[(t + 1) * blk:(t + 2) * blk]))
        out = jnp.concatenate(pairs, -1)
        o_ref[0, r0:r0 + blk, :] = _rms(out, gain).astype(BF16)


def _win_attn(qa, kd, vd, sink, g_win, tq):
    b, s, _ = qa.shape
    blk = WIN_HALF
    kw = kd.shape[-1]
    nblk = s // blk
    per = tq // blk
    main = pl.BlockSpec((1, tq, kw), lambda bi, i: (bi, i, 0))
    prev = pl.BlockSpec((1, blk, kw), lambda bi, i: (bi, jnp.maximum(i * per - 1, 0), 0))
    nxt = pl.BlockSpec((1, blk, kw), lambda bi, i: (bi, jnp.minimum((i + 1) * per, nblk - 1), 0))
    return pl.pallas_call(
        functools.partial(_win_kernel, tq, s),
        grid=(b, s // tq),
        in_specs=[pl.BlockSpec(memory_space=pltpu.SMEM),
                  pl.BlockSpec((1, tq, WIN_WIDTH), lambda bi, i: (bi, i, 0)),
                  prev, main, nxt, prev, main, nxt,
                  _const_spec((1, WIN_WIDTH))],
        out_specs=pl.BlockSpec((1, tq, WIN_WIDTH), lambda bi, i: (bi, i, 0)),
        out_shape=jax.ShapeDtypeStruct((b, s, WIN_WIDTH), BF16),
        scratch_shapes=[pltpu.VMEM((tq + 2 * blk, kw), BF16), pltpu.VMEM((tq + 2 * blk, kw), BF16)],
        compiler_params=_params(("parallel", "parallel")),
        name="win_attn",
    )(sink, qa, kd, kd, kd, vd, vd, vd, g_win)


def _dil_kernel(tq, seq_len, q_ref, kp_ref, km_ref, kn_ref, vp_ref, vm_ref, vn_ref, o_ref, lse_ref, kcat, vcat):
    i = pl.program_id(2)
    side = DIL_SIDE
    blk = 2 * side
    kcat[0:side] = kp_ref[0]
    kcat[side:side + tq] = km_ref[0]
    kcat[side + tq:] = kn_ref[0]
    vcat[0:side] = vp_ref[0]
    vcat[side:side + tq] = vm_ref[0]
    vcat[side + tq:] = vn_ref[0]
    lo = _lane_lo((blk, LANES))
    lane8 = lax.broadcasted_iota(jnp.int32, (blk, DIL_SLOTS), 1)
    for j in range(tq // blk):
        r0 = j * blk
        mask = _band_mask(blk, 2 * blk, side, i * tq + r0 - side, seq_len)
        mask = jnp.concatenate([mask, mask], 0)
        qblk = q_ref[0, r0:r0 + blk, :]
        pairs = []
        lse8 = jnp.zeros((blk, DIL_SLOTS), F32)
        for pr in range(DIL_SLOTS // 2):
            cols = slice(pr * LANES, (pr + 1) * LANES)
            q2 = qblk[:, cols]
            zero = jnp.zeros_like(q2)
            qs = jnp.concatenate([jnp.where(lo, q2, zero), jnp.where(lo, zero, q2)], 0)
            s = _dot_nt(qs, kcat[r0:r0 + 2 * blk, cols])
            s = jnp.where(mask, s, NEG_INF)
            m = jnp.max(s, -1, keepdims=True)
            p = jnp.exp(s - m)
            denom = jnp.sum(p, -1, keepdims=True)
            o = _dot(p.astype(BF16), vcat[r0:r0 + 2 * blk, cols]) / denom
            pairs.append(jnp.where(lo, o[:blk], o[blk:]))
            lse = m + jnp.log(denom)
            lse8 = jnp.where(lane8 == 2 * pr, lse[:blk], lse8)
            lse8 = jnp.where(lane8 == 2 * pr + 1, lse[blk:], lse8)
        o_ref[0, r0:r0 + blk, :] = jnp.concatenate(pairs, -1).astype(BF16)
        lse_ref[0, 0, r0:r0 + blk, :] = lse8


def _dil_attn(q, k, v, dil, tq):
    b, s, w = q.shape
    ln = s // dil
    tq = min(tq, ln)
    side = DIL_SIDE
    per = tq // side
    nblk = ln // side
    view = lambda t: t.reshape(b, ln, dil * w)
    main = pl.BlockSpec((1, tq, w), lambda bi, c, i: (bi, i, c))
    prev = pl.BlockSpec((1, side, w), lambda bi, c, i: (bi, jnp.maximum(i * per - 1, 0), c))
    nxt = pl.BlockSpec((1, side, w), lambda bi, c, i: (bi, jnp.minimum((i + 1) * per, nblk - 1), c))
    qv, kv, vv = view(q), view(k), view(v)
    o, lse = pl.pallas_call(
        functools.partial(_dil_kernel, tq, ln),
        grid=(b, dil, ln // tq),
        in_specs=[main, prev, main, nxt, prev, main, nxt],
        out_specs=[main, pl.BlockSpec((1, 1, tq, DIL_SLOTS), lambda bi, c, i: (bi, c, i, 0))],
        out_shape=[jax.ShapeDtypeStruct((b, ln, dil * w), BF16),
                   jax.ShapeDtypeStruct((b, dil, ln, DIL_SLOTS), F32)],
        scratch_shapes=[pltpu.VMEM((tq + 2 * side, w), BF16), pltpu.VMEM((tq + 2 * side, w), BF16)],
        compiler_params=_params(("parallel", "parallel", "parallel")),
        name=f"dil_attn_{dil}",
    )(qv, kv, kv, kv, vv, vv, vv)
    o = o.reshape(b, s, w)
    lse = jnp.transpose(lse, (0, 2, 1, 3)).reshape(b, s, DIL_SLOTS)
    return o, lse


def _mem_kernel(mem_ref, g_ref, b_ref, wk_ref, wv_ref, k_ref, v_ref):
    mn = _ln(mem_ref[0], g_ref[...], b_ref[...]).astype(BF16)
    k_ref[0] = _dot(mn, wk_ref[...]).astype(BF16)
    v_ref[0] = _dot(mn, wv_ref[...]).astype(BF16)


def _mem_kv(mem, g, b, wk, wv):
    bsz, m, d = mem.shape
    blk = pl.BlockSpec((1, m, d), lambda bi: (bi, 0, 0))
    return pl.pallas_call(
        _mem_kernel,
        grid=(bsz,),
        in_specs=[blk, _const_spec((1, d)), _const_spec((1, d)), _const_spec((d, d)), _const_spec((d, d))],
        out_specs=[blk, blk],
        out_shape=[jax.ShapeDtypeStruct((bsz, m, d), BF16)] * 2,
        compiler_params=_params(("parallel",)),
        name="mem_kv",
    )(mem, g, b, wk, wv)


def _mix_kernel(oa_ref, o0_ref, o1_ref, o2_ref, l0_ref, l1_ref, l2_ref, h_ref, e_ref, gd_ref, wo_ref,
                g1_ref, b1_ref, wq_ref, kx_ref, vx_ref, wxo_ref, g2_ref, b2_ref, out_ref):
    ls = [l0_ref[0], l1_ref[0], l2_ref[0]]
    mx = jnp.maximum(jnp.maximum(ls[0], ls[1]), ls[2])
    es = [jnp.exp(l - mx) for l in ls]
    tot = es[0] + es[1] + es[2]
    expand = e_ref[...]
    ob = None
    for e, o_ref in zip(es, (o0_ref, o1_ref, o2_ref)):
        wexp = _dot((e / tot).astype(BF16), expand)
        term = wexp * o_ref[0].astype(F32)
        ob = term if ob is None else ob + term
    obn = _rms(ob, gd_ref[...]).astype(BF16)
    mix = _dot(oa_ref[0], wo_ref[0:WIN_WIDTH, :]) + _dot(obn, wo_ref[WIN_WIDTH:, :])
    h1 = _ln(DEEPNORM_ALPHA * h_ref[0] + mix, g1_ref[...], b1_ref[...])
    q = _dot(h1.astype(BF16), wq_ref[...]).astype(BF16)
    heads = []
    for hh in range(X_HEADS):
        cols = slice(hh * X_HEAD_DIM, (hh + 1) * X_HEAD_DIM)
        s = _dot_nt(q[:, cols], kx_ref[0, :, cols])
        m = jnp.max(s, -1, keepdims=True)
        p = jnp.exp(s - m)
        denom = jnp.sum(p, -1, keepdims=True)
        heads.append((_dot(p.astype(BF16), vx_ref[0, :, cols]) / denom).astype(BF16))
    xa = _dot(jnp.concatenate(heads, -1), wxo_ref[...])
    out_ref[0] = _ln(DEEPNORM_ALPHA * h1 + xa, g2_ref[...], b2_ref[...])


def _mix_xattn(oa, os_, ls, h, expand, g_dil, w_out, g1, b1, w_xq, kx, vx, w_xo, g2, b2, tm):
    b, s, d = h.shape
    m = kx.shape[1]
    tok = lambda w: pl.BlockSpec((1, tm, w), lambda bi, i: (bi, i, 0))
    memspec = pl.BlockSpec((1, m, d), lambda bi, i: (bi, 0, 0))
    return pl.pallas_call(
        _mix_kernel,
        grid=(b, s // tm),
        in_specs=[tok(WIN_WIDTH), tok(DIL_WIDTH), tok(DIL_WIDTH), tok(DIL_WIDTH),
                  tok(DIL_SLOTS), tok(DIL_SLOTS), tok(DIL_SLOTS), tok(d),
                  _const_spec(expand.shape), _const_spec((1, DIL_WIDTH)), _const_spec((d, d)),
                  _const_spec((1, d)), _const_spec((1, d)), _const_spec((d, d)), memspec, memspec,
                  _const_spec((d, d)), _const_spec((1, d)), _const_spec((1, d))],
        out_specs=tok(d),
        out_shape=jax.ShapeDtypeStruct((b, s, d), F32),
        compiler_params=_params(("parallel", "parallel")),
        name="mix_xattn",
    )(oa, *os_, *ls, h, expand, g_dil, w_out, g1, b1, w_xq, kx, vx, w_xo, g2, b2)


FF_CHUNKS = ((0, 1024), (1024, 1024), (2048, 768))
HALO = 8


def _ffn_kernel(tm, hp_ref, hm_ref, hn_ref, wg_ref, wu_ref, cw_ref, cb_ref, wd_ref, g3_ref, b3_ref, out_ref, gs):
    i = pl.program_id(1)
    n = pl.num_programs(1)
    hm = hm_ref[0]
    hb = hm.astype(BF16)
    hp = jnp.where(i > 0, hp_ref[0], 0.0).astype(BF16)
    hn = jnp.where(i < n - 1, hn_ref[0], 0.0).astype(BF16)
    hcat = jnp.concatenate([hp, hb, hn], 0)
    acc = None
    for c0, width in FF_CHUNKS:
        cols = slice(c0, c0 + width)
        gs[:, 0:width] = _dot(hcat, wg_ref[:, cols])
        g = cb_ref[:, cols]
        for j in range(CONV_WIDTH):
            g = g + gs[HALO - 1 + j:HALO - 1 + j + tm, 0:width] * cw_ref[j:j + 1, cols]
        u = _dot(hb, wu_ref[:, cols])
        act = 0.5 * g * (1.0 + lax.erf(g * np.float32(np.sqrt(0.5)))) * u
        part = _dot(act.astype(BF16), wd_ref[cols, :])
        acc = part if acc is None else acc + part
    out_ref[0] = _ln(DEEPNORM_ALPHA * hm + acc, g3_ref[...], b3_ref[...])


def _conv_glu(h2, wg, wu, cw, cb, wd, g3, b3, tm):
    b, s, d = h2.shape
    per = tm // HALO
    nblk = s // HALO
    main = pl.BlockSpec((1, tm, d), lambda bi, i: (bi, i, 0))
    prev = pl.BlockSpec((1, HALO, d), lambda bi, i: (bi, jnp.maximum(i * per - 1, 0), 0))
    nxt = pl.BlockSpec((1, HALO, d), lambda bi, i: (bi, jnp.minimum((i + 1) * per, nblk - 1), 0))
    wmax = max(w for _, w in FF_CHUNKS)
    return pl.pallas_call(
        functools.partial(_ffn_kernel, tm),
        grid=(b, s // tm),
        in_specs=[prev, main, nxt, _const_spec((d, D_FF)), _const_spec((d, D_FF)),
                  _const_spec((CONV_WIDTH, D_FF)), _const_spec((1, D_FF)), _const_spec((D_FF, d)),
                  _const_spec((1, d)), _const_spec((1, d))],
        out_specs=main,
        out_shape=jax.ShapeDtypeStruct((b, s, d), F32),
        scratch_shapes=[pltpu.VMEM((tm + 2 * HALO, wmax), F32)],
        compiler_params=_params(("parallel", "parallel")),
        name="conv_glu",
    )(h2, h2, h2, wg, wu, cw, cb, wd, g3, b3)


def _dup_heads(w):
    d, n = w.shape
    return jnp.repeat(w.reshape(d, n // HEAD_DIM, 1, HEAD_DIM), 2, axis=2).reshape(d, 2 * n)


def kernel(x, mem, positions, ln_in_g, ln_in_b, w_in, attn_sink, g_win, g_dil, w_mix_out, ln1_g, ln1_b, mem_ln_g, mem_ln_b, w_xq, w_xk, w_xv, w_xo, ln2_g, ln2_b, w_gate, w_up, conv_w, conv_b, w_down, ln3_g, ln3_b):
    b, s, d = x.shape
    row = lambda v: v.reshape(1, -1)
    c_tab, sa_tab, sb_tab = _rope_tables(positions)
    c_tab, sa_tab, sb_tab = (t.reshape(b * s, LANES) for t in (c_tab, sa_tab, sb_tab))

    assert DEPTH == 1
    for l in range(DEPTH):
        wi = w_in[l]
        qscale = HEAD_DIM ** -0.5
        o_qa, o_ka, o_va, o_qb = 0, A_Q, A_Q + A_KV, A_Q + 2 * A_KV
        o_kb, o_vb = o_qb + B_QKV, o_qb + 2 * B_QKV
        pieces = [(wi[:, o_qa:o_ka] * qscale, True),
                  (_dup_heads(wi[:, o_ka:o_va]), True),
                  (_dup_heads(wi[:, o_va:o_qb]), False)]
        for gi in range(N_DIL):
            c = gi * DIL_WIDTH
            pieces += [(wi[:, o_qb + c:o_qb + c + DIL_WIDTH] * qscale, True),
                       (wi[:, o_kb + c:o_kb + c + DIL_WIDTH], True),
                       (wi[:, o_vb + c:o_vb + c + DIL_WIDTH], False)]
        segs, c0 = [], 0
        for wpiece, rope in pieces:
            segs.append((c0, wpiece.shape[1], rope))
            c0 += wpiece.shape[1]
        w_all = jnp.concatenate([p for p, _ in pieces], 1).astype(BF16)

        outs = _qkv_proj(x.reshape(b * s, d), row(ln_in_g), row(ln_in_b), w_all, tuple(segs), c_tab, sa_tab, sb_tab, tm=512)
        hn = outs[0].reshape(b, s, d)
        tok = lambda t: t.reshape(b, s, t.shape[-1])
        qa, kd, vd = (tok(t) for t in outs[1:4])
        oa = _win_attn(qa, kd, vd, attn_sink[l], row(g_win[l]), tq=512)
        os_, ls = [], []
        for gi, (_, dil) in enumerate(DIL_PAIRS):
            qg, kg, vg = (tok(t) for t in outs[4 + 3 * gi:7 + 3 * gi])
            o, lse = _dil_attn(qg, kg, vg, dil, tq=512)
            os_.append(o)
            ls.append(lse)

        kx, vx = _mem_kv(mem, row(mem_ln_g[l]), row(mem_ln_b[l]), w_xk[l].astype(BF16), w_xv[l].astype(BF16))
        expand = jnp.repeat(jnp.eye(DIL_SLOTS, dtype=BF16), HEAD_DIM, axis=1)
        xscale = X_HEAD_DIM ** -0.5
        h2 = _mix_xattn(oa, os_, ls, hn, expand, row(g_dil[l]), w_mix_out[l].astype(BF16),
                        row(ln1_g[l]), row(ln1_b[l]), (w_xq[l] * xscale).astype(BF16), kx, vx,
                        w_xo[l].astype(BF16), row(ln2_g[l]), row(ln2_b[l]), tm=512)
        h = _conv_glu(h2, w_gate[l].astype(BF16), w_up[l].astype(BF16), conv_w[l], row(conv_b[l]),
                      w_down[l].astype(BF16), row(ln3_g[l]), row(ln3_b[l]), tm=512)
    return h
```

```python
import functools

import numpy as np
import jax
import jax.numpy as jnp
from jax import lax
from jax.experimental import pallas as pl
from jax.experimental.pallas import tpu as pltpu

D_MODEL = 1024
HEAD_DIM = 64
WIN_Q_HEADS = 8
WIN_KV_HEADS = 2
WIN_HALF = 128
DIL_SLOTS = 8
DIL_PAIRS = ((128, 1), (512, 4), (2048, 16))
N_DIL = len(DIL_PAIRS)
ROT_DIM = HEAD_DIM // 4
ROT_HALF = ROT_DIM // 2
ROPE_THETA = 500000.0
X_HEADS = 4
X_HEAD_DIM = D_MODEL // X_HEADS
D_FF = 2816
CONV_WIDTH = 3
WIN_WIDTH = WIN_Q_HEADS * HEAD_DIM
DIL_WIDTH = DIL_SLOTS * HEAD_DIM
A_Q = WIN_WIDTH
A_KV = WIN_KV_HEADS * HEAD_DIM
B_QKV = N_DIL * DIL_WIDTH
DEPTH = 1
DEEPNORM_ALPHA = (2 * DEPTH) ** 0.25
LN_EPS = 1e-5
NEG_INF = -1e30

LANES = 128
TOK_PER_ROW = LANES // ROT_HALF
Q_BLK = 128
DIL_SIDE = 64
VMEM_LIMIT = 56 * 1024 * 1024

F32 = jnp.float32
BF16 = jnp.bfloat16

WIN_HEAD_ORDER = tuple(h for p in range(WIN_Q_HEADS // 2) for h in (p, p + WIN_Q_HEADS // 2))


def _ln(x, g, b):
    mu = jnp.mean(x, -1, keepdims=True)
    xc = x - mu
    var = jnp.mean(xc * xc, -1, keepdims=True)
    return xc * lax.rsqrt(var + LN_EPS) * g + b


def _rms(x, g):
    return x * lax.rsqrt(jnp.mean(x * x, -1, keepdims=True) + LN_EPS) * g


def _dot(a, b):
    return jnp.dot(a, b, preferred_element_type=F32)


def _dot_nt(a, b):
    return lax.dot_general(a, b, (((1,), (1,)), ((), ())), preferred_element_type=F32)


def _const_spec(shape):
    nd = len(shape)
    return pl.BlockSpec(shape, lambda *_: (0,) * nd, pipeline_mode=pl.Buffered(1))


def _params(sem):
    return pltpu.CompilerParams(dimension_semantics=sem, vmem_limit_bytes=VMEM_LIMIT)


def _rope_tab_kernel(pos_ref, invf_ref, cos_ref, sin_ref):
    ang = pos_ref[...].astype(F32) * invf_ref[...]
    cos_ref[...] = jnp.cos(ang)
    sin_ref[...] = jnp.sin(ang)


def _rope_tables(positions):
    inv_freq = ROPE_THETA ** (-jnp.arange(0, ROT_DIM, 2, dtype=F32) / ROT_DIM)
    pos_rep = jnp.repeat(positions.reshape(-1), ROT_HALF).reshape(-1, LANES)
    invf = jnp.tile(inv_freq, TOK_PER_ROW).reshape(1, LANES)
    return pl.pallas_call(
        _rope_tab_kernel,
        out_shape=(jax.ShapeDtypeStruct(pos_rep.shape, F32),) * 2,
        name="rope_tables",
    )(pos_rep, invf)


def _expand_rope(tab, tm):
    rows = jnp.broadcast_to(tab[:, None, :], (tm // TOK_PER_ROW, TOK_PER_ROW, LANES)).reshape(tm, LANES)
    tok = lax.broadcasted_iota(jnp.int32, (tm, LANES), 0) % TOK_PER_ROW
    lane = lax.broadcasted_iota(jnp.int32, (tm, LANES), 1)
    idx = tok * ROT_HALF + lane % ROT_HALF
    return jnp.take_along_axis(rows, idx, axis=1, mode="promise_in_bounds")


def _qkv_kernel(segs, tm, x_ref, g_ref, b_ref, w_ref, cos_ref, sin_ref, h_ref, *rest):
    out_refs, stage = rest[:-1], rest[-1]
    h = _ln(x_ref[...], g_ref[...], b_ref[...])
    h_ref[...] = h
    hb = h.astype(BF16)
    slot = lax.broadcasted_iota(jnp.int32, (tm, LANES), 1) % HEAD_DIM
    cos = _expand_rope(cos_ref[...], tm)
    sin = _expand_rope(sin_ref[...], tm)
    c_tab = jnp.where(slot < ROT_DIM, cos, 1.0)
    sa_tab = jnp.where(slot < ROT_HALF, sin, 0.0)
    sb_tab = jnp.where((slot >= ROT_HALF) & (slot < ROT_DIM), sin, 0.0)
    for (c0, width, rope, dil), o_ref in zip(segs, out_refs):
        z = _dot(hb, w_ref[:, c0:c0 + width])
        for j in range(width // LANES):
            zc = z[:, j * LANES:(j + 1) * LANES]
            if rope:
                up = pltpu.roll(zc, LANES - ROT_HALF, 1)
                dn = pltpu.roll(zc, ROT_HALF, 1)
                zc = zc * c_tab - up * sa_tab + dn * sb_tab
            if dil == 1:
                o_ref[:, j * LANES:(j + 1) * LANES] = zc.astype(BF16)
            else:
                stage[...] = zc
                for c in range(dil):
                    col = c * width + j * LANES
                    o_ref[:, col:col + LANES] = stage[pl.ds(c, tm // dil, stride=dil), :].astype(BF16)


def _qkv_proj(x2, ln_g, ln_b, w_all, segs, cos, sin, tm):
    t = x2.shape[0]
    wtot = w_all.shape[1]
    row = lambda r, w: pl.BlockSpec((r, w), lambda i: (i, 0))
    out_shape = [jax.ShapeDtypeStruct((t, D_MODEL), F32)]
    out_specs = [row(tm, D_MODEL)]
    for _, width, _, dil in segs:
        out_shape.append(jax.ShapeDtypeStruct((t // dil, dil * width), BF16))
        out_specs.append(row(tm // dil, dil * width))
    return pl.pallas_call(
        functools.partial(_qkv_kernel, segs, tm),
        grid=(t // tm,),
        in_specs=[row(tm, D_MODEL), _const_spec((1, D_MODEL)), _const_spec((1, D_MODEL)),
                  _const_spec((D_MODEL, wtot)), row(tm // TOK_PER_ROW, LANES), row(tm // TOK_PER_ROW, LANES)],
        out_specs=out_specs,
        out_shape=out_shape,
        scratch_shapes=[pltpu.VMEM((tm, LANES), F32)],
        compiler_params=_params(("parallel",)),
        name="qkv_proj",
    )(x2, ln_g, ln_b, w_all, cos, sin)


def _band_mask(nq, nk, side, key0, seq_len):
    qi = lax.broadcasted_iota(jnp.int32, (nq, nk), 0)
    kj = lax.broadcasted_iota(jnp.int32, (nq, nk), 1)
    rel = kj - side - qi
    kabs = key0 + kj
    return (rel >= -side) & (rel <= side) & (kabs >= 0) & (kabs < seq_len)


def _band_kernel(side, dil, tq, seq_len, kv_shared, is_win, *refs):
    if is_win:
        sink_ref, q_ref, kp_ref, km_ref, kn_ref, vp_ref, vm_ref, vn_ref, g_ref, o_ref, kcat, vcat = refs
    else:
        q_ref, kp_ref, km_ref, kn_ref, vp_ref, vm_ref, vn_ref, o_ref, lse_ref, kcat, vcat = refs
    i = pl.program_id(1)
    blk = Q_BLK
    nk = blk + 2 * side
    kcat[0:side] = kp_ref[0]
    kcat[side:side + tq] = km_ref[0]
    kcat[side + tq:] = kn_ref[0]
    vcat[0:side] = vp_ref[0]
    vcat[side:side + tq] = vm_ref[0]
    vcat[side + tq:] = vn_ref[0]
    width = q_ref.shape[-1] // dil
    n_pairs = width // LANES
    lo = (lax.broadcasted_iota(jnp.int32, (blk, LANES), 1)) < HEAD_DIM
    lane8 = lax.broadcasted_iota(jnp.int32, (blk, DIL_SLOTS), 1)
    for j in range(tq // blk):
        r0 = j * blk
        mask = _band_mask(blk, nk, side, i * tq + r0 - side, seq_len)
        mask = jnp.concatenate([mask, mask], 0)
        for c in range(dil):
            pairs = []
            lse8 = jnp.zeros((blk, DIL_SLOTS), F32)
            for pr in range(n_pairs):
                qcols = slice(c * width + pr * LANES, c * width + (pr + 1) * LANES)
                kcols = slice(0, LANES) if kv_shared else qcols
                q2 = q_ref[0, r0:r0 + blk, qcols]
                zero = jnp.zeros_like(q2)
                qs = jnp.concatenate([jnp.where(lo, q2, zero), jnp.where(lo, zero, q2)], 0)
                s = _dot_nt(qs, kcat[r0:r0 + nk, kcols])
                s = jnp.where(mask, s, NEG_INF)
                m = jnp.max(s, -1, keepdims=True)
                if is_win:
                    sink = jnp.concatenate([jnp.full((blk, 1), sink_ref[WIN_HEAD_ORDER[2 * pr]], F32),
                                            jnp.full((blk, 1), sink_ref[WIN_HEAD_ORDER[2 * pr + 1]], F32)], 0)
                    m = jnp.maximum(m, sink)
                p = jnp.exp(s - m)
                denom = jnp.sum(p, -1, keepdims=True)
                if is_win:
                    denom = denom + jnp.exp(sink - m)
                o = _dot(p.astype(BF16), vcat[r0:r0 + nk, kcols]) / denom
                pairs.append(jnp.where(lo, o[:blk], o[blk:]))
                if not is_win:
                    lse = m + jnp.log(denom)
                    lse8 = jnp.where(lane8 == 2 * pr, lse[:blk], lse8)
                    lse8 = jnp.where(lane8 == 2 * pr + 1, lse[blk:], lse8)
            out = jnp.concatenate(pairs, -1)
            if is_win:
                o_ref[0, r0:r0 + blk, :] = _rms(out, g_ref[...]).astype(BF16)
            else:
                o_ref[0, r0:r0 + blk, c * width:(c + 1) * width] = out.astype(BF16)
                lse_ref[0, c, r0:r0 + blk, :] = lse8


def _band_specs(tq, side, n_rows, width):
    per = tq // side
    nblk = n_rows // side
    main = pl.BlockSpec((1, tq, width), lambda bi, i: (bi, i, 0))
    prev = pl.BlockSpec((1, side, width), lambda bi, i: (bi, jnp.maximum(i * per - 1, 0), 0))
    nxt = pl.BlockSpec((1, side, width), lambda bi, i: (bi, jnp.minimum((i + 1) * per, nblk - 1), 0))
    return prev, main, nxt


def _win_attn(qa, ka, va, sink, g_win, tq):
    b, s, _ = qa.shape
    side = WIN_HALF
    kw = ka.shape[-1]
    prev, main, nxt = _band_specs(tq, side, s, kw)
    return pl.pallas_call(
        functools.partial(_band_kernel, side, 1, tq, s, True, True),
        grid=(b, s // tq),
        in_specs=[pl.BlockSpec(memory_space=pltpu.SMEM),
                  pl.BlockSpec((1, tq, WIN_WIDTH), lambda bi, i: (bi, i, 0)),
                  prev, main, nxt, prev, main, nxt,
                  _const_spec((1, WIN_WIDTH))],
        out_specs=pl.BlockSpec((1, tq, WIN_WIDTH), lambda bi, i: (bi, i, 0)),
        out_shape=jax.ShapeDtypeStruct((b, s, WIN_WIDTH), BF16),
        scratch_shapes=[pltpu.VMEM((tq + 2 * side, kw), BF16), pltpu.VMEM((tq + 2 * side, kw), BF16)],
        compiler_params=_params(("parallel", "parallel")),
        name="win_attn",
    )(sink, qa, ka, ka, ka, va, va, va, g_win)


def _dil_attn(qv, kv, vv, dil, tq):
    b, ln, wr = qv.shape
    side = DIL_SIDE
    prev, main, nxt = _band_specs(tq, side, ln, wr)
    return pl.pallas_call(
        functools.partial(_band_kernel, side, dil, tq, ln, False, False),
        grid=(b, ln // tq),
        in_specs=[main, prev, main, nxt, prev, main, nxt],
        out_specs=[main, pl.BlockSpec((1, dil, tq, DIL_SLOTS), lambda bi, i: (bi, 0, i, 0))],
        out_shape=[jax.ShapeDtypeStruct((b, ln, wr), BF16),
                   jax.ShapeDtypeStruct((b, dil, ln, DIL_SLOTS), F32)],
        scratch_shapes=[pltpu.VMEM((tq + 2 * side, wr), BF16), pltpu.VMEM((tq + 2 * side, wr), BF16)],
        compiler_params=_params(("parallel", "parallel")),
        name=f"dil_attn_{dil}",
    )(qv, kv, kv, kv, vv, vv, vv)


def _mem_kernel(mem_ref, g_ref, b_ref, wk_ref, wv_ref, k_ref, v_ref):
    mn = _ln(mem_ref[0], g_ref[...], b_ref[...]).astype(BF16)
    k_ref[0] = _dot(mn, wk_ref[...]).astype(BF16)
    v_ref[0] = _dot(mn, wv_ref[...]).astype(BF16)


def _mem_kv(mem, g, b, wk, wv):
    bsz, m, d = mem.shape
    blk = pl.BlockSpec((1, m, d), lambda bi: (bi, 0, 0))
    return pl.pallas_call(
        _mem_kernel,
        grid=(bsz,),
        in_specs=[blk, _const_spec((1, d)), _const_spec((1, d)), _const_spec((d, d)), _const_spec((d, d))],
        out_specs=[blk, blk],
        out_shape=[jax.ShapeDtypeStruct((bsz, m, d), BF16)] * 2,
        compiler_params=_params(("parallel",)),
        name="mem_kv",
    )(mem, g, b, wk, wv)


def _mix_kernel(tm, oa_ref, o0_ref, o1_ref, o2_ref, l0_ref, l1_ref, l2_ref, h_ref, e_ref, gd_ref, wo_ref,
                g1_ref, b1_ref, wq_ref, kx_ref, vx_ref, wxo_ref, g2_ref, b2_ref, out_ref, stage):
    ls = [l0_ref[0], l1_ref[0], l2_ref[0]]
    mx = jnp.maximum(jnp.maximum(ls[0], ls[1]), ls[2])
    es = [jnp.exp(l - mx) for l in ls]
    tot = es[0] + es[1] + es[2]
    expand = e_ref[...]
    n_chunks = DIL_WIDTH // LANES
    ob = None
    for e, o_ref, (_, dil) in zip(es, (o0_ref, o1_ref, o2_ref), DIL_PAIRS):
        wexp = _dot((e / tot).astype(BF16), expand)
        if dil == 1:
            og = o_ref[0].astype(F32)
        else:
            for c in range(dil):
                for j in range(n_chunks):
                    col = c * DIL_WIDTH + j * LANES
                    stage[j, pl.ds(c, tm // dil, stride=dil), :] = o_ref[0, :, col:col + LANES].astype(F32)
            og = jnp.concatenate([stage[j] for j in range(n_chunks)], -1)
        term = wexp * og
        ob = term if ob is None else ob + term
    obn = _rms(ob, gd_ref[...]).astype(BF16)
    mix = _dot(oa_ref[0], wo_ref[0:WIN_WIDTH, :]) + _dot(obn, wo_ref[WIN_WIDTH:, :])
    h1 = _ln(DEEPNORM_ALPHA * h_ref[0] + mix, g1_ref[...], b1_ref[...])
    q = _dot(h1.astype(BF16), wq_ref[...]).astype(BF16)
    heads = []
    for hh in range(X_HEADS):
        cols = slice(hh * X_HEAD_DIM, (hh + 1) * X_HEAD_DIM)
        s = _dot_nt(q[:, cols], kx_ref[0, :, cols])
        m = jnp.max(s, -1, keepdims=True)
        p = jnp.exp(s - m)
        denom = jnp.sum(p, -1, keepdims=True)
        heads.append((_dot(p.astype(BF16), vx_ref[0, :, cols]) / denom).astype(BF16))
    xa = _dot(jnp.concatenate(heads, -1), wxo_ref[...])
    out_ref[0] = _ln(DEEPNORM_ALPHA * h1 + xa, g2_ref[...], b2_ref[...])


def _mix_xattn(oa, os_, ls, h, expand, g_dil, w_out, g1, b1, w_xq, kx, vx, w_xo, g2, b2, tm):
    b, s, d = h.shape
    m = kx.shape[1]
    tok = lambda w: pl.BlockSpec((1, tm, w), lambda bi, i: (bi, i, 0))
    res = lambda dil: pl.BlockSpec((1, tm // dil, dil * DIL_WIDTH), lambda bi, i: (bi, i, 0))
    memspec = pl.BlockSpec((1, m, d), lambda bi, i: (bi, 0, 0))
    return pl.pallas_call(
        functools.partial(_mix_kernel, tm),
        grid=(b, s // tm),
        in_specs=[tok(WIN_WIDTH)] + [res(dil) for _, dil in DIL_PAIRS] +
                 [tok(DIL_SLOTS), tok(DIL_SLOTS), tok(DIL_SLOTS), tok(d),
                  _const_spec(expand.shape), _const_spec((1, DIL_WIDTH)), _const_spec((d, d)),
                  _const_spec((1, d)), _const_spec((1, d)), _const_spec((d, d)), memspec, memspec,
                  _const_spec((d, d)), _const_spec((1, d)), _const_spec((1, d))],
        out_specs=tok(d),
        out_shape=jax.ShapeDtypeStruct((b, s, d), F32),
        scratch_shapes=[pltpu.VMEM((DIL_WIDTH // LANES, tm, LANES), F32)],
        compiler_params=_params(("parallel", "parallel")),
        name="mix_xattn",
    )(oa, *os_, *ls, h, expand, g_dil, w_out, g1, b1, w_xq, kx, vx, w_xo, g2, b2)


FF_CHUNKS = ((0, 1024), (1024, 1024), (2048, 768))
HALO = 8


def _ffn_kernel(tm, hp_ref, hm_ref, hn_ref, wg_ref, wu_ref, cw_ref, cb_ref, wd_ref, g3_ref, b3_ref, out_ref, gs):
    i = pl.program_id(1)
    n = pl.num_programs(1)
    hm = hm_ref[0]
    hb = hm.astype(BF16)
    hp = jnp.where(i > 0, hp_ref[0], 0.0).astype(BF16)
    hn = jnp.where(i < n - 1, hn_ref[0], 0.0).astype(BF16)
    hcat = jnp.concatenate([hp, hb, hn], 0)
    acc = None
    for c0, width in FF_CHUNKS:
        cols = slice(c0, c0 + width)
        gs[:, 0:width] = _dot(hcat, wg_ref[:, cols])
        g = cb_ref[:, cols]
        for j in range(CONV_WIDTH):
            g = g + gs[HALO - 1 + j:HALO - 1 + j + tm, 0:width] * cw_ref[j:j + 1, cols]
        u = _dot(hb, wu_ref[:, cols])
        act = 0.5 * g * (1.0 + lax.erf(g * np.float32(np.sqrt(0.5)))) * u
        part = _dot(act.astype(BF16), wd_ref[cols, :])
        acc = part if acc is None else acc + part
    out_ref[0] = _ln(DEEPNORM_ALPHA * hm + acc, g3_ref[...], b3_ref[...])


def _conv_glu(h2, wg, wu, cw, cb, wd, g3, b3, tm):
    b, s, d = h2.shape
    per = tm // HALO
    nblk = s // HALO
    main = pl.BlockSpec((1, tm, d), lambda bi, i: (bi, i, 0))
    prev = pl.BlockSpec((1, HALO, d), lambda bi, i: (bi, jnp.maximum(i * per - 1, 0), 0))
    nxt = pl.BlockSpec((1, HALO, d), lambda bi, i: (bi, jnp.minimum((i + 1) * per, nblk - 1), 0))
    wmax = max(w for _, w in FF_CHUNKS)
    return pl.pallas_call(
        functools.partial(_ffn_kernel, tm),
        grid=(b, s // tm),
        in_specs=[prev, main, nxt, _const_spec((d, D_FF)), _const_spec((d, D_FF)),
                  _const_spec((CONV_WIDTH, D_FF)), _const_spec((1, D_FF)), _const_spec((D_FF, d)),
                  _const_spec((1, d)), _const_spec((1, d))],
        out_specs=main,
        out_shape=jax.ShapeDtypeStruct((b, s, d), F32),
        scratch_shapes=[pltpu.VMEM((tm + 2 * HALO, wmax), F32)],
        compiler_params=_params(("parallel", "parallel")),
        name="conv_glu",
    )(h2, h2, h2, wg, wu, cw, cb, wd, g3, b3)


def _head_cols(order):
    return np.concatenate([np.arange(h * HEAD_DIM, (h + 1) * HEAD_DIM) for h in order])


def kernel(x, mem, positions, ln_in_g, ln_in_b, w_in, attn_sink, g_win, g_dil, w_mix_out, ln1_g, ln1_b, mem_ln_g, mem_ln_b, w_xq, w_xk, w_xv, w_xo, ln2_g, ln2_b, w_gate, w_up, conv_w, conv_b, w_down, ln3_g, ln3_b):
    b, s, d = x.shape
    assert DEPTH == 1
    l = 0
    row = lambda v: v.reshape(1, -1)
    cos, sin = _rope_tables(positions)

    wi = w_in[l]
    qscale = HEAD_DIM ** -0.5
    win_cols = _head_cols(WIN_HEAD_ORDER)
    o_ka, o_va, o_qb = A_Q, A_Q + A_KV, A_Q + 2 * A_KV
    o_kb, o_vb = o_qb + B_QKV, o_qb + 2 * B_QKV
    pieces = [(wi[:, win_cols] * qscale, True, 1),
              (wi[:, o_ka:o_va], True, 1),
              (wi[:, o_va:o_qb], False, 1)]
    for gi, (_, dil) in enumerate(DIL_PAIRS):
        c = gi * DIL_WIDTH
        pieces += [(wi[:, o_qb + c:o_qb + c + DIL_WIDTH] * qscale, True, dil),
                   (wi[:, o_kb + c:o_kb + c + DIL_WIDTH], True, dil),
                   (wi[:, o_vb + c:o_vb + c + DIL_WIDTH], False, dil)]
    segs, c0 = [], 0
    for wpiece, rope, dil in pieces:
        segs.append((c0, wpiece.shape[1], rope, dil))
        c0 += wpiece.shape[1]
    w_all = jnp.concatenate([p for p, _, _ in pieces], 1).astype(BF16)

    outs = _qkv_proj(x.reshape(b * s, d), row(ln_in_g), row(ln_in_b), w_all, tuple(segs), cos, sin, tm=512)
    hn = outs[0].reshape(b, s, d)
    batched = lambda t: t.reshape(b, t.shape[0] // b, t.shape[1])
    qa, ka, va = (batched(t) for t in outs[1:4])
    oa = _win_attn(qa, ka, va, attn_sink[l], row(g_win[l][win_cols]), tq=512)
    os_, ls = [], []
    for gi, (_, dil) in enumerate(DIL_PAIRS):
        qg, kg, vg = (batched(t) for t in outs[4 + 3 * gi:7 + 3 * gi])
        o, lse = _dil_attn(qg, kg, vg, dil, tq=max(2048 // dil, Q_BLK) if dil > 1 else 512)
        os_.append(o)
        ls.append(jnp.transpose(lse, (0, 2, 1, 3)).reshape(b, s, DIL_SLOTS))

    kx, vx = _mem_kv(mem, row(mem_ln_g[l]), row(mem_ln_b[l]), w_xk[l].astype(BF16), w_xv[l].astype(BF16))
    expand = jnp.repeat(jnp.eye(DIL_SLOTS, dtype=BF16), HEAD_DIM, axis=1)
    xscale = X_HEAD_DIM ** -0.5
    wo = w_mix_out[l]
    wo = jnp.concatenate([wo[win_cols], wo[WIN_WIDTH:]], 0).astype(BF16)
    h2 = _mix_xattn(oa, os_, ls, hn, expand, row(g_dil[l]), wo,
                    row(ln1_g[l]), row(ln1_b[l]), (w_xq[l] * xscale).astype(BF16), kx, vx,
                    w_xo[l].astype(BF16), row(ln2_g[l]), row(ln2_b[l]), tm=512)
    return _conv_glu(h2, w_gate[l].astype(BF16), w_up[l].astype(BF16), conv_w[l], row(conv_b[l]),
                     w_down[l].astype(BF16), row(ln3_g[l]), row(ln3_b[l]), tm=512)
```

```python
import functools

import numpy as np
import jax
import jax.numpy as jnp
from jax import lax
from jax.experimental import pallas as pl
from jax.experimental.pallas import tpu as pltpu

D_MODEL = 1024
HEAD_DIM = 64
WIN_Q_HEADS = 8
WIN_KV_HEADS = 2
WIN_HALF = 128
DIL_SLOTS = 8
DIL_PAIRS = ((128, 1), (512, 4), (2048, 16))
N_DIL = len(DIL_PAIRS)
ROT_DIM = HEAD_DIM // 4
ROT_HALF = ROT_DIM // 2
ROPE_THETA = 500000.0
X_HEADS = 4
X_HEAD_DIM = D_MODEL // X_HEADS
D_FF = 2816
CONV_WIDTH = 3
WIN_WIDTH = WIN_Q_HEADS * HEAD_DIM
DIL_WIDTH = DIL_SLOTS * HEAD_DIM
A_Q = WIN_WIDTH
A_KV = WIN_KV_HEADS * HEAD_DIM
B_QKV = N_DIL * DIL_WIDTH
DEPTH = 1
DEEPNORM_ALPHA = (2 * DEPTH) ** 0.25
LN_EPS = 1e-5
NEG_INF = -1e30
LOG2E = float(np.log2(np.e))

LANES = 128
TOK_PER_ROW = LANES // ROT_HALF
Q_BLK = 128
DIL_SIDE = 64
VMEM_LIMIT = 56 * 1024 * 1024

F32 = jnp.float32
BF16 = jnp.bfloat16


def _ln(x, g, b):
    mu = jnp.mean(x, -1, keepdims=True)
    xc = x - mu
    var = jnp.mean(xc * xc, -1, keepdims=True)
    return xc * lax.rsqrt(var + LN_EPS) * g + b


def _rms(x, g):
    return x * lax.rsqrt(jnp.mean(x * x, -1, keepdims=True) + LN_EPS) * g


def _dot(a, b):
    return jnp.dot(a, b, preferred_element_type=F32)


def _dot_nt(a, b):
    return lax.dot_general(a, b, (((1,), (1,)), ((), ())), preferred_element_type=F32)


def _const_spec(shape):
    nd = len(shape)
    return pl.BlockSpec(shape, lambda *_: (0,) * nd, pipeline_mode=pl.Buffered(1))


def _params(sem):
    return pltpu.CompilerParams(dimension_semantics=sem, vmem_limit_bytes=VMEM_LIMIT)


def _rope_tab_kernel(pos_ref, invf_ref, cos_ref, sin_ref):
    ang = pos_ref[...].astype(F32) * invf_ref[...]
    cos_ref[...] = jnp.cos(ang)
    sin_ref[...] = jnp.sin(ang)


def _rope_tables(positions):
    inv_freq = ROPE_THETA ** (-jnp.arange(0, ROT_DIM, 2, dtype=F32) / ROT_DIM)
    pos_rep = jnp.repeat(positions.reshape(-1), ROT_HALF).reshape(-1, LANES)
    invf = jnp.tile(inv_freq, TOK_PER_ROW).reshape(1, LANES)
    return pl.pallas_call(
        _rope_tab_kernel,
        out_shape=(jax.ShapeDtypeStruct(pos_rep.shape, F32),) * 2,
        name="rope_tables",
    )(pos_rep, invf)


def _expand_rope(tab, tm):
    rows = jnp.broadcast_to(tab[:, None, :], (tm // TOK_PER_ROW, TOK_PER_ROW, LANES)).reshape(tm, LANES)
    tok = lax.broadcasted_iota(jnp.int32, (tm, LANES), 0) % TOK_PER_ROW
    lane = lax.broadcasted_iota(jnp.int32, (tm, LANES), 1)
    idx = tok * ROT_HALF + lane % ROT_HALF
    return jnp.take_along_axis(rows, idx, axis=1, mode="promise_in_bounds")


def _qkv_kernel(segs, tm, x_ref, g_ref, b_ref, w_ref, cos_ref, sin_ref, h_ref, *rest):
    out_refs, stage = rest[:-1], rest[-1]
    h = _ln(x_ref[...], g_ref[...], b_ref[...])
    h_ref[...] = h
    hb = h.astype(BF16)
    slot = lax.broadcasted_iota(jnp.int32, (tm, LANES), 1) % HEAD_DIM
    cos = _expand_rope(cos_ref[...], tm)
    sin = _expand_rope(sin_ref[...], tm)
    c_tab = jnp.where(slot < ROT_DIM, cos, 1.0)
    sa_tab = jnp.where(slot < ROT_HALF, sin, 0.0)
    sb_tab = jnp.where((slot >= ROT_HALF) & (slot < ROT_DIM), sin, 0.0)
    lo = slot == lax.broadcasted_iota(jnp.int32, (tm, LANES), 1)
    for (c0, width, rope, dil, scale, dup), o_ref in zip(segs, out_refs):
        z = _dot(hb, w_ref[:, c0:c0 + width])
        for j in range(width // LANES):
            zc = z[:, j * LANES:(j + 1) * LANES]
            if rope:
                up = pltpu.roll(zc, LANES - ROT_HALF, 1)
                dn = pltpu.roll(zc, ROT_HALF, 1)
                zc = zc * c_tab - up * sa_tab + dn * sb_tab
            if scale != 1.0:
                zc = zc * scale
            if dup:
                sw = pltpu.roll(zc, HEAD_DIM, 1)
                o_ref[:, 2 * j * LANES:(2 * j + 1) * LANES] = jnp.where(lo, zc, sw).astype(BF16)
                o_ref[:, (2 * j + 1) * LANES:(2 * j + 2) * LANES] = jnp.where(lo, sw, zc).astype(BF16)
            elif dil == 1:
                o_ref[:, j * LANES:(j + 1) * LANES] = zc.astype(BF16)
            else:
                stage[...] = zc
                for c in range(dil):
                    col = c * width + j * LANES
                    o_ref[:, col:col + LANES] = stage[pl.ds(c, tm // dil, stride=dil), :].astype(BF16)


def _qkv_proj(x2, ln_g, ln_b, w_all, segs, cos, sin, tm):
    t = x2.shape[0]
    wtot = w_all.shape[1]
    row = lambda r, w: pl.BlockSpec((r, w), lambda i: (i, 0))
    out_shape = [jax.ShapeDtypeStruct((t, D_MODEL), F32)]
    out_specs = [row(tm, D_MODEL)]
    for _, width, _, dil, _, dup in segs:
        wout = 2 * width if dup else dil * width
        out_shape.append(jax.ShapeDtypeStruct((t // dil, wout), BF16))
        out_specs.append(row(tm // dil, wout))
    return pl.pallas_call(
        functools.partial(_qkv_kernel, segs, tm),
        grid=(t // tm,),
        in_specs=[row(tm, D_MODEL), _const_spec((1, D_MODEL)), _const_spec((1, D_MODEL)),
                  _const_spec((D_MODEL, wtot)), row(tm // TOK_PER_ROW, LANES), row(tm // TOK_PER_ROW, LANES)],
        out_specs=out_specs,
        out_shape=out_shape,
        scratch_shapes=[pltpu.VMEM((tm, LANES), F32)],
        compiler_params=_params(("parallel",)),
        name="qkv_proj",
    )(x2, ln_g, ln_b, w_all, cos, sin)


def _band_bias(nq, nk, side):
    qi = lax.broadcasted_iota(jnp.int32, (nq, nk), 0)
    kj = lax.broadcasted_iota(jnp.int32, (nq, nk), 1)
    rel = kj - side - qi
    return jnp.where((rel >= -side) & (rel <= side), 0.0, NEG_INF).astype(F32)


def _band_kernel(side, dil, tq, kv_shared, is_win, *refs):
    if is_win:
        sink_ref, q_ref, kp_ref, km_ref, kn_ref, vp_ref, vm_ref, vn_ref, g_ref, o_ref, kcat, vcat = refs
    else:
        q_ref, kp_ref, km_ref, kn_ref, vp_ref, vm_ref, vn_ref, o_ref, lse_ref, kcat, vcat = refs
    i = pl.program_id(1)
    n_tiles = pl.num_programs(1)
    blk = Q_BLK
    nk = blk + 2 * side
    n_blk = tq // blk
    kcat[0:side] = kp_ref[0]
    kcat[side:side + tq] = km_ref[0]
    kcat[side + tq:] = kn_ref[0]
    vcat[0:side] = vp_ref[0]
    vcat[side:side + tq] = vm_ref[0]
    vcat[side + tq:] = vn_ref[0]
    width = q_ref.shape[-1] // dil
    n_pairs = width // LANES
    lo = lax.broadcasted_iota(jnp.int32, (blk, LANES), 1) < HEAD_DIM
    lane8 = lax.broadcasted_iota(jnp.int32, (blk, DIL_SLOTS), 1)
    band = _band_bias(blk, nk, side)
    kj = lax.broadcasted_iota(jnp.int32, (blk, nk), 1)
    band_first = jnp.where((i == 0) & (kj < side), NEG_INF, band)
    band_last = jnp.where((i == n_tiles - 1) & (kj >= side + blk), NEG_INF, band)
    for j in range(n_blk):
        r0 = j * blk
        bias = band
        if j == 0:
            bias = band_first
        if j == n_blk - 1:
            bias = band_last if j > 0 else jnp.minimum(band_first, band_last)
        bias = jnp.concatenate([bias, bias], 0)
        for c in range(dil):
            pairs = []
            lse8 = jnp.zeros((blk, DIL_SLOTS), F32)
            for pr in range(n_pairs):
                qcols = slice(c * width + pr * LANES, c * width + (pr + 1) * LANES)
                kcols = slice((pr // 2) * LANES, (pr // 2 + 1) * LANES) if kv_shared else qcols
                q2 = q_ref[0, r0:r0 + blk, qcols]
                zero = jnp.zeros_like(q2)
                qs = jnp.concatenate([jnp.where(lo, q2, zero), jnp.where(lo, zero, q2)], 0)
                s = _dot_nt(qs, kcat[r0:r0 + nk, kcols]) + bias
                m = jnp.max(s, -1, keepdims=True)
                if is_win:
                    sink = jnp.concatenate(
                        [jnp.full((blk, 1), sink_ref[2 * pr] * LOG2E, F32),
                         jnp.full((blk, 1), sink_ref[2 * pr + 1] * LOG2E, F32)], 0)
                    m = jnp.maximum(m, sink)
                p = jnp.exp2(s - m)
                denom = jnp.sum(p, -1, keepdims=True)
                if is_win:
                    denom = denom + jnp.exp2(sink - m)
                o = _dot(p.astype(BF16), vcat[r0:r0 + nk, kcols]) / denom
                pairs.append(jnp.where(lo, o[:blk], o[blk:]))
                if not is_win:
                    lse = m + jnp.log2(denom)
                    lse8 = jnp.where(lane8 == 2 * pr, lse[:blk], lse8)
                    lse8 = jnp.where(lane8 == 2 * pr + 1, lse[blk:], lse8)
            out = jnp.concatenate(pairs, -1)
            if is_win:
                o_ref[0, r0:r0 + blk, :] = _rms(out, g_ref[...]).astype(BF16)
            else:
                o_ref[0, r0:r0 + blk, c * width:(c + 1) * width] = out.astype(BF16)
                lse_ref[0, c, r0:r0 + blk, :] = lse8


def _band_specs(tq, side, n_rows, width):
    per = tq // side
    nblk = n_rows // side
    main = pl.BlockSpec((1, tq, width), lambda bi, i: (bi, i, 0))
    prev = pl.BlockSpec((1, side, width), lambda bi, i: (bi, jnp.maximum(i * per - 1, 0), 0))
    nxt = pl.BlockSpec((1, side, width), lambda bi, i: (bi, jnp.minimum((i + 1) * per, nblk - 1), 0))
    return prev, main, nxt


def _win_attn(qa, ka, va, sink, g_win, tq):
    b, s, _ = qa.shape
    side = WIN_HALF
    kw = ka.shape[-1]
    prev, main, nxt = _band_specs(tq, side, s, kw)
    return pl.pallas_call(
        functools.partial(_band_kernel, side, 1, tq, True, True),
        grid=(b, s // tq),
        in_specs=[pl.BlockSpec(memory_space=pltpu.SMEM),
                  pl.BlockSpec((1, tq, WIN_WIDTH), lambda bi, i: (bi, i, 0)),
                  prev, main, nxt, prev, main, nxt,
                  _const_spec((1, WIN_WIDTH))],
        out_specs=pl.BlockSpec((1, tq, WIN_WIDTH), lambda bi, i: (bi, i, 0)),
        out_shape=jax.ShapeDtypeStruct((b, s, WIN_WIDTH), BF16),
        scratch_shapes=[pltpu.VMEM((tq + 2 * side, kw), BF16), pltpu.VMEM((tq + 2 * side, kw), BF16)],
        compiler_params=_params(("parallel", "parallel")),
        name="win_attn",
    )(sink, qa, ka, ka, ka, va, va, va, g_win)


def _dil_attn(qv, kv, vv, dil, tq):
    b, ln, wr = qv.shape
    side = DIL_SIDE
    prev, main, nxt = _band_specs(tq, side, ln, wr)
    return pl.pallas_call(
        functools.partial(_band_kernel, side, dil, tq, False, False),
        grid=(b, ln // tq),
        in_specs=[main, prev, main, nxt, prev, main, nxt],
        out_specs=[main, pl.BlockSpec((1, dil, tq, DIL_SLOTS), lambda bi, i: (bi, 0, i, 0))],
        out_shape=[jax.ShapeDtypeStruct((b, ln, wr), BF16),
                   jax.ShapeDtypeStruct((b, dil, ln, DIL_SLOTS), F32)],
        scratch_shapes=[pltpu.VMEM((tq + 2 * side, wr), BF16), pltpu.VMEM((tq + 2 * side, wr), BF16)],
        compiler_params=_params(("parallel", "parallel")),
        name=f"dil_attn_{dil}",
    )(qv, kv, kv, kv, vv, vv, vv)


def _mem_kernel(mem_ref, g_ref, b_ref, wk_ref, wv_ref, k_ref, v_ref):
    mn = _ln(mem_ref[0], g_ref[...], b_ref[...]).astype(BF16)
    k_ref[0] = _dot(mn, wk_ref[...]).astype(BF16)
    v_ref[0] = _dot(mn, wv_ref[...]).astype(BF16)


def _mem_kv(mem, g, b, wk, wv):
    bsz, m, d = mem.shape
    blk = pl.BlockSpec((1, m, d), lambda bi: (bi, 0, 0))
    return pl.pallas_call(
        _mem_kernel,
        grid=(bsz,),
        in_specs=[blk, _const_spec((1, d)), _const_spec((1, d)), _const_spec((d, d)), _const_spec((d, d))],
        out_specs=[blk, blk],
        out_shape=[jax.ShapeDtypeStruct((bsz, m, d), BF16)] * 2,
        compiler_params=_params(("parallel",)),
        name="mem_kv",
    )(mem, g, b, wk, wv)


def _mix_kernel(tm, oa_ref, o0_ref, o1_ref, o2_ref, l0_ref, l1_ref, l2_ref, h_ref, e_ref, gd_ref, wo_ref,
                g1_ref, b1_ref, wq_ref, kx_ref, vx_ref, wxo_ref, g2_ref, b2_ref, out_ref, stage):
    ls = [l0_ref[0], l1_ref[0], l2_ref[0]]
    mx = jnp.maximum(jnp.maximum(ls[0], ls[1]), ls[2])
    es = [jnp.exp2(l - mx) for l in ls]
    tot = es[0] + es[1] + es[2]
    expand = e_ref[...]
    n_chunks = DIL_WIDTH // LANES
    ob = None
    for e, o_ref, (_, dil) in zip(es, (o0_ref, o1_ref, o2_ref), DIL_PAIRS):
        wexp = _dot((e / tot).astype(BF16), expand)
        if dil == 1:
            og = o_ref[0].astype(F32)
        else:
            for c in range(dil):
                for j in range(n_chunks):
                    col = c * DIL_WIDTH + j * LANES
                    stage[j, pl.ds(c, tm // dil, stride=dil), :] = o_ref[0, :, col:col + LANES].astype(F32)
            og = jnp.concatenate([stage[j] for j in range(n_chunks)], -1)
        term = wexp * og
        ob = term if ob is None else ob + term
    obn = _rms(ob, gd_ref[...]).astype(BF16)
    mix = _dot(oa_ref[0], wo_ref[0:WIN_WIDTH, :]) + _dot(obn, wo_ref[WIN_WIDTH:, :])
    h1 = _ln(DEEPNORM_ALPHA * h_ref[0] + mix, g1_ref[...], b1_ref[...])
    q = _dot(h1.astype(BF16), wq_ref[...]).astype(BF16)
    heads = []
    for hh in range(X_HEADS):
        cols = slice(hh * X_HEAD_DIM, (hh + 1) * X_HEAD_DIM)
        s = _dot_nt(q[:, cols], kx_ref[0, :, cols])
        m = jnp.max(s, -1, keepdims=True)
        p = jnp.exp(s - m)
        denom = jnp.sum(p, -1, keepdims=True)
        heads.append((_dot(p.astype(BF16), vx_ref[0, :, cols]) / denom).astype(BF16))
    xa = _dot(jnp.concatenate(heads, -1), wxo_ref[...])
    out_ref[0] = _ln(DEEPNORM_ALPHA * h1 + xa, g2_ref[...], b2_ref[...])


def _mix_xattn(oa, os_, ls, h, expand, g_dil, w_out, g1, b1, w_xq, kx, vx, w_xo, g2, b2, tm):
    b, s, d = h.shape
    m = kx.shape[1]
    tok = lambda w: pl.BlockSpec((1, tm, w), lambda bi, i: (bi, i, 0))
    res = lambda dil: pl.BlockSpec((1, tm // dil, dil * DIL_WIDTH), lambda bi, i: (bi, i, 0))
    memspec = pl.BlockSpec((1, m, d), lambda bi, i: (bi, 0, 0))
    return pl.pallas_call(
        functools.partial(_mix_kernel, tm),
        grid=(b, s // tm),
        in_specs=[tok(WIN_WIDTH)] + [res(dil) for _, dil in DIL_PAIRS] +
                 [tok(DIL_SLOTS), tok(DIL_SLOTS), tok(DIL_SLOTS), tok(d),
                  _const_spec(expand.shape), _const_spec((1, DIL_WIDTH)), _const_spec((d, d)),
                  _const_spec((1, d)), _const_spec((1, d)), _const_spec((d, d)), memspec, memspec,
                  _const_spec((d, d)), _const_spec((1, d)), _const_spec((1, d))],
        out_specs=tok(d),
        out_shape=jax.ShapeDtypeStruct((b, s, d), F32),
        scratch_shapes=[pltpu.VMEM((DIL_WIDTH // LANES, tm, LANES), F32)],
        compiler_params=_params(("parallel", "parallel")),
        name="mix_xattn",
    )(oa, *os_, *ls, h, expand, g_dil, w_out, g1, b1, w_xq, kx, vx, w_xo, g2, b2)


FF_CHUNKS = ((0, 1024), (1024, 1024), (2048, 768))
HALO = 8


def _ffn_kernel(tm, hp_ref, hm_ref, hn_ref, wg_ref, wu_ref, cw_ref, cb_ref, wd_ref, g3_ref, b3_ref, out_ref, gs):
    i = pl.program_id(1)
    n = pl.num_programs(1)
    hm = hm_ref[0]
    hb = hm.astype(BF16)
    hp = jnp.where(i > 0, hp_ref[0], 0.0).astype(BF16)
    hn = jnp.where(i < n - 1, hn_ref[0], 0.0).astype(BF16)
    hcat = jnp.concatenate([hp, hb, hn], 0)
    acc = None
    for c0, width in FF_CHUNKS:
        cols = slice(c0, c0 + width)
        gs[:, 0:width] = _dot(hcat, wg_ref[:, cols])
        g = cb_ref[:, cols]
        for j in range(CONV_WIDTH):
            g = g + gs[HALO - 1 + j:HALO - 1 + j + tm, 0:width] * cw_ref[j:j + 1, cols]
        u = _dot(hb, wu_ref[:, cols])
        act = 0.5 * g * (1.0 + lax.erf(g * np.float32(np.sqrt(0.5)))) * u
        part = _dot(act.astype(BF16), wd_ref[cols, :])
        acc = part if acc is None else acc + part
    out_ref[0] = _ln(DEEPNORM_ALPHA * hm + acc, g3_ref[...], b3_ref[...])


def _conv_glu(h2, wg, wu, cw, cb, wd, g3, b3, tm):
    b, s, d = h2.shape
    per = tm // HALO
    nblk = s // HALO
    main = pl.BlockSpec((1, tm, d), lambda bi, i: (bi, i, 0))
    prev = pl.BlockSpec((1, HALO, d), lambda bi, i: (bi, jnp.maximum(i * per - 1, 0), 0))
    nxt = pl.BlockSpec((1, HALO, d), lambda bi, i: (bi, jnp.minimum((i + 1) * per, nblk - 1), 0))
    wmax = max(w for _, w in FF_CHUNKS)
    return pl.pallas_call(
        functools.partial(_ffn_kernel, tm),
        grid=(b, s // tm),
        in_specs=[prev, main, nxt, _const_spec((d, D_FF)), _const_spec((d, D_FF)),
                  _const_spec((CONV_WIDTH, D_FF)), _const_spec((1, D_FF)), _const_spec((D_FF, d)),
                  _const_spec((1, d)), _const_spec((1, d))],
        out_specs=main,
        out_shape=jax.ShapeDtypeStruct((b, s, d), F32),
        scratch_shapes=[pltpu.VMEM((tm + 2 * HALO, wmax), F32)],
        compiler_params=_params(("parallel", "parallel")),
        name="conv_glu",
    )(h2, h2, h2, wg, wu, cw, cb, wd, g3, b3)


def kernel(x, mem, positions, ln_in_g, ln_in_b, w_in, attn_sink, g_win, g_dil, w_mix_out, ln1_g, ln1_b, mem_ln_g, mem_ln_b, w_xq, w_xk, w_xv, w_xo, ln2_g, ln2_b, w_gate, w_up, conv_w, conv_b, w_down, ln3_g, ln3_b):
    b, s, d = x.shape
    assert DEPTH == 1
    l = 0
    row = lambda v: v.reshape(1, -1)
    cos, sin = _rope_tables(positions)

    qscale = HEAD_DIM ** -0.5 * LOG2E
    o_ka, o_va, o_qb = A_Q, A_Q + A_KV, A_Q + 2 * A_KV
    o_kb, o_vb = o_qb + B_QKV, o_qb + 2 * B_QKV
    segs = [(0, A_Q, True, 1, qscale, False),
            (o_ka, A_KV, True, 1, 1.0, True),
            (o_va, A_KV, False, 1, 1.0, True)]
    for gi, (_, dil) in enumerate(DIL_PAIRS):
        c = gi * DIL_WIDTH
        segs += [(o_qb + c, DIL_WIDTH, True, dil, qscale, False),
                 (o_kb + c, DIL_WIDTH, True, dil, 1.0, False),
                 (o_vb + c, DIL_WIDTH, False, dil, 1.0, False)]

    outs = _qkv_proj(x.reshape(b * s, d), row(ln_in_g), row(ln_in_b), w_in[l].astype(BF16), tuple(segs),
                     cos, sin, tm=512)
    hn = outs[0].reshape(b, s, d)
    batched = lambda t: t.reshape(b, t.shape[0] // b, t.shape[1])
    qa, ka, va = (batched(t) for t in outs[1:4])
    oa = _win_attn(qa, ka, va, attn_sink[l], row(g_win[l]), tq=512)
    os_, ls = [], []
    for gi, (_, dil) in enumerate(DIL_PAIRS):
        qg, kg, vg = (batched(t) for t in outs[4 + 3 * gi:7 + 3 * gi])
        o, lse = _dil_attn(qg, kg, vg, dil, tq=max(2048 // dil, Q_BLK) if dil > 1 else 512)
        os_.append(o)
        ls.append(jnp.transpose(lse, (0, 2, 1, 3)).reshape(b, s, DIL_SLOTS))

    kx, vx = _mem_kv(mem, row(mem_ln_g[l]), row(mem_ln_b[l]), w_xk[l].astype(BF16), w_xv[l].astype(BF16))
    expand = jnp.repeat(jnp.eye(DIL_SLOTS, dtype=BF16), HEAD_DIM, axis=1)
    xscale = X_HEAD_DIM ** -0.5
    h2 = _mix_xattn(oa, os_, ls, hn, expand, row(g_dil[l]), w_mix_out[l].astype(BF16),
                    row(ln1_g[l]), row(ln1_b[l]), (w_xq[l] * xscale).astype(BF16), kx, vx,
                    w_xo[l].astype(BF16), row(ln2_g[l]), row(ln2_b[l]), tm=512)
    return _conv_glu(h2, w_gate[l].astype(BF16), w_up[l].astype(BF16), conv_w[l], row(conv_b[l]),
                     w_down[l].astype(BF16), row(ln3_g[l]), row(ln3_b[l]), tm=512)
```

```python
import functools

import numpy as np
import jax
import jax.numpy as jnp
from jax import lax
from jax.experimental import pallas as pl
from jax.experimental.pallas import tpu as pltpu

D_MODEL = 1024
HEAD_DIM = 64
WIN_Q_HEADS = 8
WIN_KV_HEADS = 2
WIN_HALF = 128
DIL_SLOTS = 8
DIL_PAIRS = ((128, 1), (512, 4), (2048, 16))
N_DIL = len(DIL_PAIRS)
ROT_DIM = HEAD_DIM // 4
ROT_HALF = ROT_DIM // 2
ROPE_THETA = 500000.0
X_HEADS = 4
X_HEAD_DIM = D_MODEL // X_HEADS
D_FF = 2816
CONV_WIDTH = 3
WIN_WIDTH = WIN_Q_HEADS * HEAD_DIM
DIL_WIDTH = DIL_SLOTS * HEAD_DIM
A_Q = WIN_WIDTH
A_KV = WIN_KV_HEADS * HEAD_DIM
B_QKV = N_DIL * DIL_WIDTH
DEPTH = 1
DEEPNORM_ALPHA = (2 * DEPTH) ** 0.25
LN_EPS = 1e-5
NEG_INF = -1e30
LOG2E = float(np.log2(np.e))

LANES = 128
TOK_PER_ROW = LANES // ROT_HALF
Q_BLK = 128
DIL_SIDE = 64
VMEM_LIMIT = 56 * 1024 * 1024

F32 = jnp.float32
BF16 = jnp.bfloat16


def _ln(x, g, b):
    mu = jnp.mean(x, -1, keepdims=True)
    xc = x - mu
    var = jnp.mean(xc * xc, -1, keepdims=True)
    return xc * lax.rsqrt(var + LN_EPS) * g + b


def _rms(x, g):
    return x * lax.rsqrt(jnp.mean(x * x, -1, keepdims=True) + LN_EPS) * g


def _dot(a, b):
    return jnp.dot(a, b, preferred_element_type=F32)


def _dot_nt(a, b):
    return lax.dot_general(a, b, (((1,), (1,)), ((), ())), preferred_element_type=F32)


def _const_spec(shape):
    nd = len(shape)
    return pl.BlockSpec(shape, lambda *_: (0,) * nd, pipeline_mode=pl.Buffered(1))


def _params(sem):
    return pltpu.CompilerParams(dimension_semantics=sem, vmem_limit_bytes=VMEM_LIMIT)


def _rope_tab_kernel(pos_ref, invf_ref, cos_ref, sin_ref):
    ang = pos_ref[...].astype(F32) * invf_ref[...]
    cos_ref[...] = jnp.cos(ang)
    sin_ref[...] = jnp.sin(ang)


def _rope_tables(positions):
    inv_freq = ROPE_THETA ** (-jnp.arange(0, ROT_DIM, 2, dtype=F32) / ROT_DIM)
    pos_rep = jnp.repeat(positions.reshape(-1), ROT_HALF).reshape(-1, LANES)
    invf = jnp.tile(inv_freq, TOK_PER_ROW).reshape(1, LANES)
    return pl.pallas_call(
        _rope_tab_kernel,
        out_shape=(jax.ShapeDtypeStruct(pos_rep.shape, F32),) * 2,
        name="rope_tables",
    )(pos_rep, invf)


def _expand_rope(tab, tm):
    rows = jnp.broadcast_to(tab[:, None, :], (tm // TOK_PER_ROW, TOK_PER_ROW, LANES)).reshape(tm, LANES)
    tok = lax.broadcasted_iota(jnp.int32, (tm, LANES), 0) % TOK_PER_ROW
    lane = lax.broadcasted_iota(jnp.int32, (tm, LANES), 1)
    idx = tok * ROT_HALF + lane % ROT_HALF
    return jnp.take_along_axis(rows, idx, axis=1, mode="promise_in_bounds")


def _qkv_kernel(segs, tm, x_ref, g_ref, b_ref, w_ref, cos_ref, sin_ref, h_ref, *rest):
    out_refs, stage = rest[:-1], rest[-1]
    h = _ln(x_ref[...], g_ref[...], b_ref[...])
    h_ref[...] = h
    hb = h.astype(BF16)
    slot = lax.broadcasted_iota(jnp.int32, (tm, LANES), 1) % HEAD_DIM
    cos = _expand_rope(cos_ref[...], tm)
    sin = _expand_rope(sin_ref[...], tm)
    c_tab = jnp.where(slot < ROT_DIM, cos, 1.0)
    sa_tab = jnp.where(slot < ROT_HALF, sin, 0.0)
    sb_tab = jnp.where((slot >= ROT_HALF) & (slot < ROT_DIM), sin, 0.0)
    lo = slot == lax.broadcasted_iota(jnp.int32, (tm, LANES), 1)
    for (c0, width, rope, dil, scale, dup), o_ref in zip(segs, out_refs):
        z = _dot(hb, w_ref[:, c0:c0 + width])
        for j in range(width // LANES):
            zc = z[:, j * LANES:(j + 1) * LANES]
            if rope:
                up = pltpu.roll(zc, LANES - ROT_HALF, 1)
                dn = pltpu.roll(zc, ROT_HALF, 1)
                zc = zc * c_tab - up * sa_tab + dn * sb_tab
            if scale != 1.0:
                zc = zc * scale
            if dup:
                sw = pltpu.roll(zc, HEAD_DIM, 1)
                o_ref[:, 2 * j * LANES:(2 * j + 1) * LANES] = jnp.where(lo, zc, sw).astype(BF16)
                o_ref[:, (2 * j + 1) * LANES:(2 * j + 2) * LANES] = jnp.where(lo, sw, zc).astype(BF16)
            elif dil == 1:
                o_ref[:, j * LANES:(j + 1) * LANES] = zc.astype(BF16)
            else:
                stage[...] = zc
                for c in range(dil):
                    col = c * width + j * LANES
                    o_ref[:, col:col + LANES] = stage[pl.ds(c, tm // dil, stride=dil), :].astype(BF16)


def _qkv_proj(x2, ln_g, ln_b, w_all, segs, cos, sin, tm):
    t = x2.shape[0]
    wtot = w_all.shape[1]
    row = lambda r, w: pl.BlockSpec((r, w), lambda i: (i, 0))
    out_shape = [jax.ShapeDtypeStruct((t, D_MODEL), F32)]
    out_specs = [row(tm, D_MODEL)]
    for _, width, _, dil, _, dup in segs:
        wout = 2 * width if dup else dil * width
        out_shape.append(jax.ShapeDtypeStruct((t // dil, wout), BF16))
        out_specs.append(row(tm // dil, wout))
    return pl.pallas_call(
        functools.partial(_qkv_kernel, segs, tm),
        grid=(t // tm,),
        in_specs=[row(tm, D_MODEL), _const_spec((1, D_MODEL)), _const_spec((1, D_MODEL)),
                  _const_spec((D_MODEL, wtot)), row(tm // TOK_PER_ROW, LANES), row(tm // TOK_PER_ROW, LANES)],
        out_specs=out_specs,
        out_shape=out_shape,
        scratch_shapes=[pltpu.VMEM((tm, LANES), F32)],
        compiler_params=_params(("parallel",)),
        name="qkv_proj",
    )(x2, ln_g, ln_b, w_all, cos, sin)


def _band_bias(nq, nk, side):
    qi = lax.broadcasted_iota(jnp.int32, (nq, nk), 0)
    kj = lax.broadcasted_iota(jnp.int32, (nq, nk), 1)
    rel = kj - side - qi
    return jnp.where((rel >= -side) & (rel <= side), 0.0, NEG_INF).astype(F32)


def _band_kernel(side, dil, tq, kv_shared, is_win, *refs):
    if is_win:
        sink_ref, q_ref, kp_ref, km_ref, kn_ref, vp_ref, vm_ref, vn_ref, g_ref, o_ref, kcat, vcat = refs
    else:
        q_ref, kp_ref, km_ref, kn_ref, vp_ref, vm_ref, vn_ref, o_ref, lse_ref, kcat, vcat = refs
    i = pl.program_id(1)
    n_tiles = pl.num_programs(1)
    blk = Q_BLK
    nk = blk + 2 * side
    n_blk = tq // blk
    kcat[0:side] = kp_ref[0]
    kcat[side:side + tq] = km_ref[0]
    kcat[side + tq:] = kn_ref[0]
    vcat[0:side] = vp_ref[0]
    vcat[side:side + tq] = vm_ref[0]
    vcat[side + tq:] = vn_ref[0]
    width = q_ref.shape[-1] // dil
    n_pairs = width // LANES
    lo = lax.broadcasted_iota(jnp.int32, (blk, LANES), 1) < HEAD_DIM
    lane8 = lax.broadcasted_iota(jnp.int32, (blk, DIL_SLOTS), 1)
    band = _band_bias(blk, nk, side)
    kj = lax.broadcasted_iota(jnp.int32, (blk, nk), 1)
    band_first = jnp.where((i == 0) & (kj < side), NEG_INF, band)
    band_last = jnp.where((i == n_tiles - 1) & (kj >= side + blk), NEG_INF, band)
    biases = []
    for j in range(n_blk):
        bias = band
        if j == 0:
            bias = band_first
        if j == n_blk - 1:
            bias = band_last if j > 0 else jnp.minimum(band_first, band_last)
        biases.append(jnp.concatenate([bias, bias], 0))

    units = [(j, c, pr) for j in range(n_blk) for c in range(dil) for pr in range(n_pairs)]

    def kv_cols(c, pr):
        if kv_shared:
            return slice((pr // 2) * LANES, (pr // 2 + 1) * LANES)
        return slice(c * width + pr * LANES, c * width + (pr + 1) * LANES)

    def scores(j, c, pr):
        r0 = j * blk
        q2 = q_ref[0, r0:r0 + blk, c * width + pr * LANES:c * width + (pr + 1) * LANES]
        zero = jnp.zeros_like(q2)
        qs = jnp.concatenate([jnp.where(lo, q2, zero), jnp.where(lo, zero, q2)], 0)
        return _dot_nt(qs, kcat[r0:r0 + nk, kv_cols(c, pr)]) + biases[j]

    def softmax(s, pr):
        ps, rdens, lses = [], [], []
        for hf in range(2):
            sh = s[hf * blk:(hf + 1) * blk]
            m = jnp.max(sh, -1, keepdims=True)
            if is_win:
                sink = sink_ref[2 * pr + hf] * LOG2E
                m = jnp.maximum(m, sink)
            ph = jnp.exp2(sh - m)
            denom = jnp.sum(ph, -1, keepdims=True)
            if is_win:
                denom = denom + jnp.exp2(sink - m)
            else:
                lses.append(m + jnp.log2(denom))
            ps.append(ph.astype(BF16))
            rdens.append(1.0 / denom)
        return jnp.concatenate(ps, 0), rdens, lses

    pairs, lse8 = [], None

    def finish(unit, o, rdens, lses):
        nonlocal pairs, lse8
        j, c, pr = unit
        r0 = j * blk
        pairs.append(jnp.where(lo, o[:blk] * rdens[0], o[blk:] * rdens[1]))
        if not is_win:
            if lse8 is None:
                lse8 = jnp.zeros((blk, DIL_SLOTS), F32)
            lse8 = jnp.where(lane8 == 2 * pr, lses[0], lse8)
            lse8 = jnp.where(lane8 == 2 * pr + 1, lses[1], lse8)
        if pr == n_pairs - 1:
            out = jnp.concatenate(pairs, -1)
            if is_win:
                o_ref[0, r0:r0 + blk, :] = _rms(out, g_ref[...]).astype(BF16)
            else:
                o_ref[0, r0:r0 + blk, c * width:(c + 1) * width] = out.astype(BF16)
                lse_ref[0, c, r0:r0 + blk, :] = lse8
            pairs, lse8 = [], None

    s_next = scores(*units[0])
    pending = None
    for n, (j, c, pr) in enumerate(units):
        s = s_next
        if n + 1 < len(units):
            s_next = scores(*units[n + 1])
        p, rdens, lses = softmax(s, pr)
        o = _dot(p, vcat[j * blk:j * blk + nk, kv_cols(c, pr)])
        if pending is not None:
            finish(*pending)
        pending = ((j, c, pr), o, rdens, lses)
    finish(*pending)


def _band_specs(tq, side, n_rows, width):
    per = tq // side
    nblk = n_rows // side
    main = pl.BlockSpec((1, tq, width), lambda bi, i: (bi, i, 0))
    prev = pl.BlockSpec((1, side, width), lambda bi, i: (bi, jnp.maximum(i * per - 1, 0), 0))
    nxt = pl.BlockSpec((1, side, width), lambda bi, i: (bi, jnp.minimum((i + 1) * per, nblk - 1), 0))
    return prev, main, nxt


def _win_attn(qa, ka, va, sink, g_win, tq):
    b, s, _ = qa.shape
    side = WIN_HALF
    kw = ka.shape[-1]
    prev, main, nxt = _band_specs(tq, side, s, kw)
    return pl.pallas_call(
        functools.partial(_band_kernel, side, 1, tq, True, True),
        grid=(b, s // tq),
        in_specs=[pl.BlockSpec(memory_space=pltpu.SMEM),
                  pl.BlockSpec((1, tq, WIN_WIDTH), lambda bi, i: (bi, i, 0)),
                  prev, main, nxt, prev, main, nxt,
                  _const_spec((1, WIN_WIDTH))],
        out_specs=pl.BlockSpec((1, tq, WIN_WIDTH), lambda bi, i: (bi, i, 0)),
        out_shape=jax.ShapeDtypeStruct((b, s, WIN_WIDTH), BF16),
        scratch_shapes=[pltpu.VMEM((tq + 2 * side, kw), BF16), pltpu.VMEM((tq + 2 * side, kw), BF16)],
        compiler_params=_params(("parallel", "parallel")),
        name="win_attn",
    )(sink, qa, ka, ka, ka, va, va, va, g_win)


def _dil_attn(qv, kv, vv, dil, tq):
    b, ln, wr = qv.shape
    side = DIL_SIDE
    prev, main, nxt = _band_specs(tq, side, ln, wr)
    return pl.pallas_call(
        functools.partial(_band_kernel, side, dil, tq, False, False),
        grid=(b, ln // tq),
        in_specs=[main, prev, main, nxt, prev, main, nxt],
        out_specs=[main, pl.BlockSpec((1, dil, tq, DIL_SLOTS), lambda bi, i: (bi, 0, i, 0))],
        out_shape=[jax.ShapeDtypeStruct((b, ln, wr), BF16),
                   jax.ShapeDtypeStruct((b, dil, ln, DIL_SLOTS), F32)],
        scratch_shapes=[pltpu.VMEM((tq + 2 * side, wr), BF16), pltpu.VMEM((tq + 2 * side, wr), BF16)],
        compiler_params=_params(("parallel", "parallel")),
        name=f"dil_attn_{dil}",
    )(qv, kv, kv, kv, vv, vv, vv)


def _mem_kernel(mem_ref, g_ref, b_ref, wk_ref, wv_ref, k_ref, v_ref):
    mn = _ln(mem_ref[0], g_ref[...], b_ref[...]).astype(BF16)
    k_ref[0] = _dot(mn, wk_ref[...]).astype(BF16)
    v_ref[0] = _dot(mn, wv_ref[...]).astype(BF16)


def _mem_kv(mem, g, b, wk, wv):
    bsz, m, d = mem.shape
    blk = pl.BlockSpec((1, m, d), lambda bi: (bi, 0, 0))
    return pl.pallas_call(
        _mem_kernel,
        grid=(bsz,),
        in_specs=[blk, _const_spec((1, d)), _const_spec((1, d)), _const_spec((d, d)), _const_spec((d, d))],
        out_specs=[blk, blk],
        out_shape=[jax.ShapeDtypeStruct((bsz, m, d), BF16)] * 2,
        compiler_params=_params(("parallel",)),
        name="mem_kv",
    )(mem, g, b, wk, wv)


def _mix_kernel(tm, oa_ref, o0_ref, o1_ref, o2_ref, l0_ref, l1_ref, l2_ref, h_ref, e_ref, gd_ref, wo_ref,
                g1_ref, b1_ref, wq_ref, kx_ref, vx_ref, wxo_ref, g2_ref, b2_ref, out_ref, stage):
    ls = [l0_ref[0], l1_ref[0], l2_ref[0]]
    mx = jnp.maximum(jnp.maximum(ls[0], ls[1]), ls[2])
    es = [jnp.exp2(l - mx) for l in ls]
    tot = es[0] + es[1] + es[2]
    expand = e_ref[...]
    n_chunks = DIL_WIDTH // LANES
    ob = None
    for e, o_ref, (_, dil) in zip(es, (o0_ref, o1_ref, o2_ref), DIL_PAIRS):
        wexp = _dot((e / tot).astype(BF16), expand)
        if dil == 1:
            og = o_ref[0].astype(F32)
        else:
            for c in range(dil):
                for j in range(n_chunks):
                    col = c * DIL_WIDTH + j * LANES
                    stage[j, pl.ds(c, tm // dil, stride=dil), :] = o_ref[0, :, col:col + LANES].astype(F32)
            og = jnp.concatenate([stage[j] for j in range(n_chunks)], -1)
        term = wexp * og
        ob = term if ob is None else ob + term
    obn = _rms(ob, gd_ref[...]).astype(BF16)
    mix = _dot(oa_ref[0], wo_ref[0:WIN_WIDTH, :]) + _dot(obn, wo_ref[WIN_WIDTH:, :])
    h1 = _ln(DEEPNORM_ALPHA * h_ref[0] + mix, g1_ref[...], b1_ref[...])
    q = _dot(h1.astype(BF16), wq_ref[...]).astype(BF16)
    heads = []
    for hh in range(X_HEADS):
        cols = slice(hh * X_HEAD_DIM, (hh + 1) * X_HEAD_DIM)
        s = _dot_nt(q[:, cols], kx_ref[0, :, cols])
        m = jnp.max(s, -1, keepdims=True)
        p = jnp.exp(s - m)
        denom = jnp.sum(p, -1, keepdims=True)
        heads.append((_dot(p.astype(BF16), vx_ref[0, :, cols]) / denom).astype(BF16))
    xa = _dot(jnp.concatenate(heads, -1), wxo_ref[...])
    out_ref[0] = _ln(DEEPNORM_ALPHA * h1 + xa, g2_ref[...], b2_ref[...])


def _mix_xattn(oa, os_, ls, h, expand, g_dil, w_out, g1, b1, w_xq, kx, vx, w_xo, g2, b2, tm):
    b, s, d = h.shape
    m = kx.shape[1]
    tok = lambda w: pl.BlockSpec((1, tm, w), lambda bi, i: (bi, i, 0))
    res = lambda dil: pl.BlockSpec((1, tm // dil, dil * DIL_WIDTH), lambda bi, i: (bi, i, 0))
    memspec = pl.BlockSpec((1, m, d), lambda bi, i: (bi, 0, 0))
    return pl.pallas_call(
        functools.partial(_mix_kernel, tm),
        grid=(b, s // tm),
        in_specs=[tok(WIN_WIDTH)] + [res(dil) for _, dil in DIL_PAIRS] +
                 [tok(DIL_SLOTS), tok(DIL_SLOTS), tok(DIL_SLOTS), tok(d),
                  _const_spec(expand.shape), _const_spec((1, DIL_WIDTH)), _const_spec((d, d)),
                  _const_spec((1, d)), _const_spec((1, d)), _const_spec((d, d)), memspec, memspec,
                  _const_spec((d, d)), _const_spec((1, d)), _const_spec((1, d))],
        out_specs=tok(d),
        out_shape=jax.ShapeDtypeStruct((b, s, d), F32),
        scratch_shapes=[pltpu.VMEM((DIL_WIDTH // LANES, tm, LANES), F32)],
        compiler_params=_params(("parallel", "parallel")),
        name="mix_xattn",
    )(oa, *os_, *ls, h, expand, g_dil, w_out, g1, b1, w_xq, kx, vx, w_xo, g2, b2)


FF_CHUNKS = ((0, 1024), (1024, 1024), (2048, 768))
HALO = 8


def _ffn_kernel(tm, hp_ref, hm_ref, hn_ref, wg_ref, wu_ref, cw_ref, cb_ref, wd_ref, g3_ref, b3_ref, out_ref, gs):
    i = pl.program_id(1)
    n = pl.num_programs(1)
    hm = hm_ref[0]
    hb = hm.astype(BF16)
    hp = jnp.where(i > 0, hp_ref[0], 0.0).astype(BF16)
    hn = jnp.where(i < n - 1, hn_ref[0], 0.0).astype(BF16)
    hcat = jnp.concatenate([hp, hb, hn], 0)
    acc = None
    for c0, width in FF_CHUNKS:
        cols = slice(c0, c0 + width)
        gs[:, 0:width] = _dot(hcat, wg_ref[:, cols])
        g = cb_ref[:, cols]
        for j in range(CONV_WIDTH):
            g = g + gs[HALO - 1 + j:HALO - 1 + j + tm, 0:width] * cw_ref[j:j + 1, cols]
        u = _dot(hb, wu_ref[:, cols])
        act = 0.5 * g * (1.0 + lax.erf(g * np.float32(np.sqrt(0.5)))) * u
        part = _dot(act.astype(BF16), wd_ref[cols, :])
        acc = part if acc is None else acc + part
    out_ref[0] = _ln(DEEPNORM_ALPHA * hm + acc, g3_ref[...], b3_ref[...])


def _conv_glu(h2, wg, wu, cw, cb, wd, g3, b3, tm):
    b, s, d = h2.shape
    per = tm // HALO
    nblk = s // HALO
    main = pl.BlockSpec((1, tm, d), lambda bi, i: (bi, i, 0))
    prev = pl.BlockSpec((1, HALO, d), lambda bi, i: (bi, jnp.maximum(i * per - 1, 0), 0))
    nxt = pl.BlockSpec((1, HALO, d), lambda bi, i: (bi, jnp.minimum((i + 1) * per, nblk - 1), 0))
    wmax = max(w for _, w in FF_CHUNKS)
    return pl.pallas_call(
        functools.partial(_ffn_kernel, tm),
        grid=(b, s // tm),
        in_specs=[prev, main, nxt, _const_spec((d, D_FF)), _const_spec((d, D_FF)),
                  _const_spec((CONV_WIDTH, D_FF)), _const_spec((1, D_FF)), _const_spec((D_FF, d)),
                  _const_spec((1, d)), _const_spec((1, d))],
        out_specs=main,
        out_shape=jax.ShapeDtypeStruct((b, s, d), F32),
        scratch_shapes=[pltpu.VMEM((tm + 2 * HALO, wmax), F32)],
        compiler_params=_params(("parallel", "parallel")),
        name="conv_glu",
    )(h2, h2, h2, wg, wu, cw, cb, wd, g3, b3)


def kernel(x, mem, positions, ln_in_g, ln_in_b, w_in, attn_sink, g_win, g_dil, w_mix_out, ln1_g, ln1_b, mem_ln_g, mem_ln_b, w_xq, w_xk, w_xv, w_xo, ln2_g, ln2_b, w_gate, w_up, conv_w, conv_b, w_down, ln3_g, ln3_b):
    b, s, d = x.shape
    assert DEPTH == 1
    l = 0
    row = lambda v: v.reshape(1, -1)
    cos, sin = _rope_tables(positions)

    qscale = HEAD_DIM ** -0.5 * LOG2E
    o_ka, o_va, o_qb = A_Q, A_Q + A_KV, A_Q + 2 * A_KV
    o_kb, o_vb = o_qb + B_QKV, o_qb + 2 * B_QKV
    segs = [(0, A_Q, True, 1, qscale, False),
            (o_ka, A_KV, True, 1, 1.0, True),
            (o_va, A_KV, False, 1, 1.0, True)]
    for gi, (_, dil) in enumerate(DIL_PAIRS):
        c = gi * DIL_WIDTH
        segs += [(o_qb + c, DIL_WIDTH, True, dil, qscale, False),
                 (o_kb + c, DIL_WIDTH, True, dil, 1.0, False),
                 (o_vb + c, DIL_WIDTH, False, dil, 1.0, False)]

    outs = _qkv_proj(x.reshape(b * s, d), row(ln_in_g), row(ln_in_b), w_in[l].astype(BF16), tuple(segs),
                     cos, sin, tm=512)
    hn = outs[0].reshape(b, s, d)
    batched = lambda t: t.reshape(b, t.shape[0] // b, t.shape[1])
    qa, ka, va = (batched(t) for t in outs[1:4])
    oa = _win_attn(qa, ka, va, attn_sink[l], row(g_win[l]), tq=512)
    os_, ls = [], []
    for gi, (_, dil) in enumerate(DIL_PAIRS):
        qg, kg, vg = (batched(t) for t in outs[4 + 3 * gi:7 + 3 * gi])
        o, lse = _dil_attn(qg, kg, vg, dil, tq=max(2048 // dil, Q_BLK) if dil > 1 else 512)
        os_.append(o)
        ls.append(jnp.transpose(lse, (0, 2, 1, 3)).reshape(b, s, DIL_SLOTS))

    kx, vx = _mem_kv(mem, row(mem_ln_g[l]), row(mem_ln_b[l]), w_xk[l].astype(BF16), w_xv[l].astype(BF16))
    expand = jnp.repeat(jnp.eye(DIL_SLOTS, dtype=BF16), HEAD_DIM, axis=1)
    xscale = X_HEAD_DIM ** -0.5
    h2 = _mix_xattn(oa, os_, ls, hn, expand, row(g_dil[l]), w_mix_out[l].astype(BF16),
                    row(ln1_g[l]), row(ln1_b[l]), (w_xq[l] * xscale).astype(BF16), kx, vx,
                    w_xo[l].astype(BF16), row(ln2_g[l]), row(ln2_b[l]), tm=512)
    return _conv_glu(h2, w_gate[l].astype(BF16), w_up[l].astype(BF16), conv_w[l], row(conv_b[l]),
                     w_down[l].astype(BF16), row(ln3_g[l]), row(ln3_b[l]), tm=512)
```

```python
import functools

import numpy as np
import jax
import jax.numpy as jnp
from jax import lax
from jax.experimental import pallas as pl
from jax.experimental.pallas import tpu as pltpu

D_MODEL = 1024
HEAD_DIM = 64
WIN_Q_HEADS = 8
WIN_KV_HEADS = 2
WIN_HALF = 128
DIL_SLOTS = 8
DIL_PAIRS = ((128, 1), (512, 4), (2048, 16))
N_DIL = len(DIL_PAIRS)
ROT_DIM = HEAD_DIM // 4
ROT_HALF = ROT_DIM // 2
ROPE_THETA = 500000.0
X_HEADS = 4
X_HEAD_DIM = D_MODEL // X_HEADS
D_FF = 2816
CONV_WIDTH = 3
WIN_WIDTH = WIN_Q_HEADS * HEAD_DIM
DIL_WIDTH = DIL_SLOTS * HEAD_DIM
A_Q = WIN_WIDTH
A_KV = WIN_KV_HEADS * HEAD_DIM
B_QKV = N_DIL * DIL_WIDTH
DEPTH = 1
DEEPNORM_ALPHA = (2 * DEPTH) ** 0.25
LN_EPS = 1e-5
NEG_INF = -1e30
LOG2E = float(np.log2(np.e))

LANES = 128
TOK_PER_ROW = LANES // ROT_HALF
Q_BLK = 128
N_SUB = 2
DIL_SIDE = 64
VMEM_LIMIT = 56 * 1024 * 1024

F32 = jnp.float32
BF16 = jnp.bfloat16


def _ln(x, g, b):
    mu = jnp.mean(x, -1, keepdims=True)
    xc = x - mu
    var = jnp.mean(xc * xc, -1, keepdims=True)
    return xc * lax.rsqrt(var + LN_EPS) * g + b


def _rms(x, g):
    return x * lax.rsqrt(jnp.mean(x * x, -1, keepdims=True) + LN_EPS) * g


def _dot(a, b):
    return jnp.dot(a, b, preferred_element_type=F32)


def _dot_nt(a, b):
    return lax.dot_general(a, b, (((1,), (1,)), ((), ())), preferred_element_type=F32)


def _const_spec(shape):
    nd = len(shape)
    return pl.BlockSpec(shape, lambda *_: (0,) * nd, pipeline_mode=pl.Buffered(1))


def _params(sem):
    return pltpu.CompilerParams(dimension_semantics=sem, vmem_limit_bytes=VMEM_LIMIT)


def _rope_tab_kernel(pos_ref, invf_ref, cos_ref, sin_ref):
    ang = pos_ref[...].astype(F32) * invf_ref[...]
    cos_ref[...] = jnp.cos(ang)
    sin_ref[...] = jnp.sin(ang)


def _rope_tables(positions):
    inv_freq = ROPE_THETA ** (-jnp.arange(0, ROT_DIM, 2, dtype=F32) / ROT_DIM)
    pos_rep = jnp.repeat(positions.reshape(-1), ROT_HALF).reshape(-1, LANES)
    invf = jnp.tile(inv_freq, TOK_PER_ROW).reshape(1, LANES)
    return pl.pallas_call(
        _rope_tab_kernel,
        out_shape=(jax.ShapeDtypeStruct(pos_rep.shape, F32),) * 2,
        name="rope_tables",
    )(pos_rep, invf)


def _expand_rope(tab, tm):
    rows = jnp.broadcast_to(tab[:, None, :], (tm // TOK_PER_ROW, TOK_PER_ROW, LANES)).reshape(tm, LANES)
    tok = lax.broadcasted_iota(jnp.int32, (tm, LANES), 0) % TOK_PER_ROW
    lane = lax.broadcasted_iota(jnp.int32, (tm, LANES), 1)
    idx = tok * ROT_HALF + lane % ROT_HALF
    return jnp.take_along_axis(rows, idx, axis=1, mode="promise_in_bounds")


def _qkv_kernel(segs, tm, x_ref, g_ref, b_ref, w_ref, cos_ref, sin_ref, h_ref, *rest):
    out_refs, stage = rest[:-1], rest[-1]
    ts = tm // N_SUB
    slot = lax.broadcasted_iota(jnp.int32, (ts, LANES), 1) % HEAD_DIM
    lo = slot == lax.broadcasted_iota(jnp.int32, (ts, LANES), 1)

    def sub_tile(k):
        rows = slice(k * ts, (k + 1) * ts)
        h = _ln(x_ref[rows, :], g_ref[...], b_ref[...])
        h_ref[rows, :] = h
        hb = h.astype(BF16)
        trows = slice(k * ts // TOK_PER_ROW, (k + 1) * ts // TOK_PER_ROW)
        cos = _expand_rope(cos_ref[trows, :], ts)
        sin = _expand_rope(sin_ref[trows, :], ts)
        c_tab = jnp.where(slot < ROT_DIM, cos, 1.0)
        sa_tab = jnp.where(slot < ROT_HALF, sin, 0.0)
        sb_tab = jnp.where((slot >= ROT_HALF) & (slot < ROT_DIM), sin, 0.0)
        yield
        for (c0, width, rope, dil, scale, dup), o_ref in zip(segs, out_refs):
            z = _dot(hb, w_ref[:, c0:c0 + width])
            yield
            orows = slice(k * ts // dil, (k + 1) * ts // dil)
            for j in range(width // LANES):
                zc = z[:, j * LANES:(j + 1) * LANES]
                if rope:
                    up = pltpu.roll(zc, LANES - ROT_HALF, 1)
                    dn = pltpu.roll(zc, ROT_HALF, 1)
                    zc = zc * c_tab - up * sa_tab + dn * sb_tab
                if scale != 1.0:
                    zc = zc * scale
                if dup:
                    sw = pltpu.roll(zc, HEAD_DIM, 1)
                    o_ref[orows, 2 * j * LANES:(2 * j + 1) * LANES] = jnp.where(lo, zc, sw).astype(BF16)
                    o_ref[orows, (2 * j + 1) * LANES:(2 * j + 2) * LANES] = jnp.where(lo, sw, zc).astype(BF16)
                elif dil == 1:
                    o_ref[orows, j * LANES:(j + 1) * LANES] = zc.astype(BF16)
                else:
                    stage[k, j] = zc
                    for c in range(dil):
                        col = c * width + j * LANES
                        o_ref[orows, col:col + LANES] = (
                            stage[k, j, pl.ds(c, ts // dil, stride=dil), :].astype(BF16))
            yield

    _round_robin([sub_tile(k) for k in range(N_SUB)])


def _qkv_proj(x2, ln_g, ln_b, w_all, segs, cos, sin, tm):
    t = x2.shape[0]
    wtot = w_all.shape[1]
    row = lambda r, w: pl.BlockSpec((r, w), lambda i: (i, 0))
    out_shape = [jax.ShapeDtypeStruct((t, D_MODEL), F32)]
    out_specs = [row(tm, D_MODEL)]
    for _, width, _, dil, _, dup in segs:
        wout = 2 * width if dup else dil * width
        out_shape.append(jax.ShapeDtypeStruct((t // dil, wout), BF16))
        out_specs.append(row(tm // dil, wout))
    return pl.pallas_call(
        functools.partial(_qkv_kernel, segs, tm),
        grid=(t // tm,),
        in_specs=[row(tm, D_MODEL), _const_spec((1, D_MODEL)), _const_spec((1, D_MODEL)),
                  _const_spec((D_MODEL, wtot)), row(tm // TOK_PER_ROW, LANES), row(tm // TOK_PER_ROW, LANES)],
        out_specs=out_specs,
        out_shape=out_shape,
        scratch_shapes=[pltpu.VMEM((N_SUB, max(s[1] for s in segs) // LANES, tm // N_SUB, LANES), F32)],
        compiler_params=_params(("parallel",)),
        name="qkv_proj",
    )(x2, ln_g, ln_b, w_all, cos, sin)


def _band_bias(nq, nk, side):
    qi = lax.broadcasted_iota(jnp.int32, (nq, nk), 0)
    kj = lax.broadcasted_iota(jnp.int32, (nq, nk), 1)
    rel = kj - side - qi
    return jnp.where((rel >= -side) & (rel <= side), 0.0, NEG_INF).astype(F32)


def _band_kernel(side, dil, tq, kv_shared, is_win, *refs):
    if is_win:
        sink_ref, q_ref, kp_ref, km_ref, kn_ref, vp_ref, vm_ref, vn_ref, g_ref, o_ref, kcat, vcat = refs
    else:
        q_ref, kp_ref, km_ref, kn_ref, vp_ref, vm_ref, vn_ref, o_ref, lse_ref, kcat, vcat = refs
    i = pl.program_id(1)
    n_tiles = pl.num_programs(1)
    blk = Q_BLK
    nk = blk + 2 * side
    n_blk = tq // blk
    kcat[0:side] = kp_ref[0]
    kcat[side:side + tq] = km_ref[0]
    kcat[side + tq:] = kn_ref[0]
    vcat[0:side] = vp_ref[0]
    vcat[side:side + tq] = vm_ref[0]
    vcat[side + tq:] = vn_ref[0]
    width = q_ref.shape[-1] // dil
    n_pairs = width // LANES
    lo = lax.broadcasted_iota(jnp.int32, (blk, LANES), 1) < HEAD_DIM
    lane8 = lax.broadcasted_iota(jnp.int32, (blk, DIL_SLOTS), 1)
    band = _band_bias(blk, nk, side)
    kj = lax.broadcasted_iota(jnp.int32, (blk, nk), 1)
    band_first = jnp.where((i == 0) & (kj < side), NEG_INF, band)
    band_last = jnp.where((i == n_tiles - 1) & (kj >= side + blk), NEG_INF, band)
    biases = []
    for j in range(n_blk):
        bias = band
        if j == 0:
            bias = band_first
        if j == n_blk - 1:
            bias = band_last if j > 0 else jnp.minimum(band_first, band_last)
        biases.append(jnp.concatenate([bias, bias], 0))

    units = [(j, c, pr) for j in range(n_blk) for c in range(dil) for pr in range(n_pairs)]

    def kv_cols(c, pr):
        if kv_shared:
            return slice((pr // 2) * LANES, (pr // 2 + 1) * LANES)
        return slice(c * width + pr * LANES, c * width + (pr + 1) * LANES)

    def scores(j, c, pr):
        r0 = j * blk
        q2 = q_ref[0, r0:r0 + blk, c * width + pr * LANES:c * width + (pr + 1) * LANES]
        zero = jnp.zeros_like(q2)
        qs = jnp.concatenate([jnp.where(lo, q2, zero), jnp.where(lo, zero, q2)], 0)
        return _dot_nt(qs, kcat[r0:r0 + nk, kv_cols(c, pr)]) + biases[j]

    def softmax(s, pr):
        ps, rdens, lses = [], [], []
        for hf in range(2):
            sh = s[hf * blk:(hf + 1) * blk]
            m = jnp.max(sh, -1, keepdims=True)
            if is_win:
                sink = sink_ref[2 * pr + hf] * LOG2E
                m = jnp.maximum(m, sink)
            ph = jnp.exp2(sh - m)
            denom = jnp.sum(ph, -1, keepdims=True)
            if is_win:
                denom = denom + jnp.exp2(sink - m)
            else:
                lses.append(m + jnp.log2(denom))
            ps.append(ph.astype(BF16))
            rdens.append(1.0 / denom)
        return jnp.concatenate(ps, 0), rdens, lses

    pairs, lse8 = [], None

    def finish(unit, o, rdens, lses):
        nonlocal pairs, lse8
        j, c, pr = unit
        r0 = j * blk
        pairs.append(jnp.where(lo, o[:blk] * rdens[0], o[blk:] * rdens[1]))
        if not is_win:
            if lse8 is None:
                lse8 = jnp.zeros((blk, DIL_SLOTS), F32)
            lse8 = jnp.where(lane8 == 2 * pr, lses[0], lse8)
            lse8 = jnp.where(lane8 == 2 * pr + 1, lses[1], lse8)
        if pr == n_pairs - 1:
            out = jnp.concatenate(pairs, -1)
            if is_win:
                o_ref[0, r0:r0 + blk, :] = _rms(out, g_ref[...]).astype(BF16)
            else:
                o_ref[0, r0:r0 + blk, c * width:(c + 1) * width] = out.astype(BF16)
                lse_ref[0, c, r0:r0 + blk, :] = lse8
            pairs, lse8 = [], None

    s_next = scores(*units[0])
    pending = None
    for n, (j, c, pr) in enumerate(units):
        s = s_next
        if n + 1 < len(units):
            s_next = scores(*units[n + 1])
        p, rdens, lses = softmax(s, pr)
        o = _dot(p, vcat[j * blk:j * blk + nk, kv_cols(c, pr)])
        if pending is not None:
            finish(*pending)
        pending = ((j, c, pr), o, rdens, lses)
    finish(*pending)


def _band_specs(tq, side, n_rows, width):
    per = tq // side
    nblk = n_rows // side
    main = pl.BlockSpec((1, tq, width), lambda bi, i: (bi, i, 0))
    prev = pl.BlockSpec((1, side, width), lambda bi, i: (bi, jnp.maximum(i * per - 1, 0), 0))
    nxt = pl.BlockSpec((1, side, width), lambda bi, i: (bi, jnp.minimum((i + 1) * per, nblk - 1), 0))
    return prev, main, nxt


def _win_attn(qa, ka, va, sink, g_win, tq):
    b, s, _ = qa.shape
    side = WIN_HALF
    kw = ka.shape[-1]
    prev, main, nxt = _band_specs(tq, side, s, kw)
    return pl.pallas_call(
        functools.partial(_band_kernel, side, 1, tq, True, True),
        grid=(b, s // tq),
        in_specs=[pl.BlockSpec(memory_space=pltpu.SMEM),
                  pl.BlockSpec((1, tq, WIN_WIDTH), lambda bi, i: (bi, i, 0)),
                  prev, main, nxt, prev, main, nxt,
                  _const_spec((1, WIN_WIDTH))],
        out_specs=pl.BlockSpec((1, tq, WIN_WIDTH), lambda bi, i: (bi, i, 0)),
        out_shape=jax.ShapeDtypeStruct((b, s, WIN_WIDTH), BF16),
        scratch_shapes=[pltpu.VMEM((tq + 2 * side, kw), BF16), pltpu.VMEM((tq + 2 * side, kw), BF16)],
        compiler_params=_params(("parallel", "parallel")),
        name="win_attn",
    )(sink, qa, ka, ka, ka, va, va, va, g_win)


def _dil_attn(qv, kv, vv, dil, tq):
    b, ln, wr = qv.shape
    side = DIL_SIDE
    prev, main, nxt = _band_specs(tq, side, ln, wr)
    return pl.pallas_call(
        functools.partial(_band_kernel, side, dil, tq, False, False),
        grid=(b, ln // tq),
        in_specs=[main, prev, main, nxt, prev, main, nxt],
        out_specs=[main, pl.BlockSpec((1, dil, tq, DIL_SLOTS), lambda bi, i: (bi, 0, i, 0))],
        out_shape=[jax.ShapeDtypeStruct((b, ln, wr), BF16),
                   jax.ShapeDtypeStruct((b, dil, ln, DIL_SLOTS), F32)],
        scratch_shapes=[pltpu.VMEM((tq + 2 * side, wr), BF16), pltpu.VMEM((tq + 2 * side, wr), BF16)],
        compiler_params=_params(("parallel", "parallel")),
        name=f"dil_attn_{dil}",
    )(qv, kv, kv, kv, vv, vv, vv)


def _mem_kernel(mem_ref, g_ref, b_ref, wk_ref, wv_ref, k_ref, v_ref):
    mn = _ln(mem_ref[0], g_ref[...], b_ref[...]).astype(BF16)
    k_ref[0] = _dot(mn, wk_ref[...]).astype(BF16)
    v_ref[0] = _dot(mn, wv_ref[...]).astype(BF16)


def _mem_kv(mem, g, b, wk, wv):
    bsz, m, d = mem.shape
    blk = pl.BlockSpec((1, m, d), lambda bi: (bi, 0, 0))
    return pl.pallas_call(
        _mem_kernel,
        grid=(bsz,),
        in_specs=[blk, _const_spec((1, d)), _const_spec((1, d)), _const_spec((d, d)), _const_spec((d, d))],
        out_specs=[blk, blk],
        out_shape=[jax.ShapeDtypeStruct((bsz, m, d), BF16)] * 2,
        compiler_params=_params(("parallel",)),
        name="mem_kv",
    )(mem, g, b, wk, wv)


def _round_robin(gens):
    live = list(gens)
    while live:
        for g in list(live):
            try:
                next(g)
            except StopIteration:
                live.remove(g)


def _mix_kernel(tm, oa_ref, o0_ref, o1_ref, o2_ref, l0_ref, l1_ref, l2_ref, h_ref, e_ref, gd_ref, wo_ref,
                g1_ref, b1_ref, wq_ref, kx_ref, vx_ref, wxo_ref, g2_ref, b2_ref, out_ref, stage):
    ts = tm // N_SUB
    n_chunks = DIL_WIDTH // LANES

    def sub_tile(k):
        rows = slice(k * ts, (k + 1) * ts)
        ls = [l_ref[0, rows, :] for l_ref in (l0_ref, l1_ref, l2_ref)]
        mx = jnp.maximum(jnp.maximum(ls[0], ls[1]), ls[2])
        es = [jnp.exp2(l - mx) for l in ls]
        tot = es[0] + es[1] + es[2]
        ob = None
        for gi, (e, o_ref, (_, dil)) in enumerate(zip(es, (o0_ref, o1_ref, o2_ref), DIL_PAIRS)):
            wexp = _dot((e / tot).astype(BF16), e_ref[...])
            if dil == 1:
                og = o_ref[0, rows, :].astype(F32)
            else:
                rrows = slice(k * ts // dil, (k + 1) * ts // dil)
                for c in range(dil):
                    for j in range(n_chunks):
                        col = c * DIL_WIDTH + j * LANES
                        stage[gi - 1, j, pl.ds(k * ts + c, ts // dil, stride=dil), :] = (
                            o_ref[0, rrows, col:col + LANES].astype(F32))
                og = jnp.concatenate([stage[gi - 1, j, rows, :] for j in range(n_chunks)], -1)
            term = wexp * og
            ob = term if ob is None else ob + term
        obn = _rms(ob, gd_ref[...]).astype(BF16)
        yield
        mix = _dot(oa_ref[0, rows, :], wo_ref[0:WIN_WIDTH, :]) + _dot(obn, wo_ref[WIN_WIDTH:, :])
        yield
        h1 = _ln(DEEPNORM_ALPHA * h_ref[0, rows, :] + mix, g1_ref[...], b1_ref[...])
        yield
        q = _dot(h1.astype(BF16), wq_ref[...]).astype(BF16)
        yield
        heads = []
        for hh in range(X_HEADS):
            cols = slice(hh * X_HEAD_DIM, (hh + 1) * X_HEAD_DIM)
            s = _dot_nt(q[:, cols], kx_ref[0, :, cols])
            yield
            m = jnp.max(s, -1, keepdims=True)
            p = jnp.exp(s - m)
            denom = jnp.sum(p, -1, keepdims=True)
            yield
            heads.append((_dot(p.astype(BF16), vx_ref[0, :, cols]) / denom).astype(BF16))
        xa = _dot(jnp.concatenate(heads, -1), wxo_ref[...])
        yield
        out_ref[0, rows, :] = _ln(DEEPNORM_ALPHA * h1 + xa, g2_ref[...], b2_ref[...])

    _round_robin([sub_tile(k) for k in range(N_SUB)])


def _mix_xattn(oa, os_, ls, h, expand, g_dil, w_out, g1, b1, w_xq, kx, vx, w_xo, g2, b2, tm):
    b, s, d = h.shape
    m = kx.shape[1]
    tok = lambda w: pl.BlockSpec((1, tm, w), lambda bi, i: (bi, i, 0))
    res = lambda dil: pl.BlockSpec((1, tm // dil, dil * DIL_WIDTH), lambda bi, i: (bi, i, 0))
    memspec = pl.BlockSpec((1, m, d), lambda bi, i: (bi, 0, 0))
    return pl.pallas_call(
        functools.partial(_mix_kernel, tm),
        grid=(b, s // tm),
        in_specs=[tok(WIN_WIDTH)] + [res(dil) for _, dil in DIL_PAIRS] +
                 [tok(DIL_SLOTS), tok(DIL_SLOTS), tok(DIL_SLOTS), tok(d),
                  _const_spec(expand.shape), _const_spec((1, DIL_WIDTH)), _const_spec((d, d)),
                  _const_spec((1, d)), _const_spec((1, d)), _const_spec((d, d)), memspec, memspec,
                  _const_spec((d, d)), _const_spec((1, d)), _const_spec((1, d))],
        out_specs=tok(d),
        out_shape=jax.ShapeDtypeStruct((b, s, d), F32),
        scratch_shapes=[pltpu.VMEM((N_DIL - 1, DIL_WIDTH // LANES, tm, LANES), F32)],
        compiler_params=_params(("parallel", "parallel")),
        name="mix_xattn",
    )(oa, *os_, *ls, h, expand, g_dil, w_out, g1, b1, w_xq, kx, vx, w_xo, g2, b2)


FF_CHUNKS = ((0, 1024), (1024, 1024), (2048, 768))
HALO = 8


def _ffn_kernel(tm, hp_ref, hm_ref, hn_ref, wg_ref, wu_ref, cw_ref, cb_ref, wd_ref, g3_ref, b3_ref, out_ref, gs):
    i = pl.program_id(1)
    n = pl.num_programs(1)
    hm = hm_ref[0]
    hb = hm.astype(BF16)
    hp = jnp.where(i > 0, hp_ref[0], 0.0).astype(BF16)
    hn = jnp.where(i < n - 1, hn_ref[0], 0.0).astype(BF16)
    hcat = jnp.concatenate([hp, hb, hn], 0)
    acc = None
    for c0, width in FF_CHUNKS:
        cols = slice(c0, c0 + width)
        gs[:, 0:width] = _dot(hcat, wg_ref[:, cols])
        g = cb_ref[:, cols]
        for j in range(CONV_WIDTH):
            g = g + gs[HALO - 1 + j:HALO - 1 + j + tm, 0:width] * cw_ref[j:j + 1, cols]
        u = _dot(hb, wu_ref[:, cols])
        act = 0.5 * g * (1.0 + lax.erf(g * np.float32(np.sqrt(0.5)))) * u
        part = _dot(act.astype(BF16), wd_ref[cols, :])
        acc = part if acc is None else acc + part
    out_ref[0] = _ln(DEEPNORM_ALPHA * hm + acc, g3_ref[...], b3_ref[...])


def _conv_glu(h2, wg, wu, cw, cb, wd, g3, b3, tm):
    b, s, d = h2.shape
    per = tm // HALO
    nblk = s // HALO
    main = pl.BlockSpec((1, tm, d), lambda bi, i: (bi, i, 0))
    prev = pl.BlockSpec((1, HALO, d), lambda bi, i: (bi, jnp.maximum(i * per - 1, 0), 0))
    nxt = pl.BlockSpec((1, HALO, d), lambda bi, i: (bi, jnp.minimum((i + 1) * per, nblk - 1), 0))
    wmax = max(w for _, w in FF_CHUNKS)
    return pl.pallas_call(
        functools.partial(_ffn_kernel, tm),
        grid=(b, s // tm),
        in_specs=[prev, main, nxt, _const_spec((d, D_FF)), _const_spec((d, D_FF)),
                  _const_spec((CONV_WIDTH, D_FF)), _const_spec((1, D_FF)), _const_spec((D_FF, d)),
                  _const_spec((1, d)), _const_spec((1, d))],
        out_specs=main,
        out_shape=jax.ShapeDtypeStruct((b, s, d), F32),
        scratch_shapes=[pltpu.VMEM((tm + 2 * HALO, wmax), F32)],
        compiler_params=_params(("parallel", "parallel")),
        name="conv_glu",
    )(h2, h2, h2, wg, wu, cw, cb, wd, g3, b3)


def kernel(x, mem, positions, ln_in_g, ln_in_b, w_in, attn_sink, g_win, g_dil, w_mix_out, ln1_g, ln1_b, mem_ln_g, mem_ln_b, w_xq, w_xk, w_xv, w_xo, ln2_g, ln2_b, w_gate, w_up, conv_w, conv_b, w_down, ln3_g, ln3_b):
    b, s, d = x.shape
    assert DEPTH == 1
    l = 0
    row = lambda v: v.reshape(1, -1)
    cos, sin = _rope_tables(positions)

    qscale = HEAD_DIM ** -0.5 * LOG2E
    o_ka, o_va, o_qb = A_Q, A_Q + A_KV, A_Q + 2 * A_KV
    o_kb, o_vb = o_qb + B_QKV, o_qb + 2 * B_QKV
    segs = []
    for gi, (_, dil) in enumerate(DIL_PAIRS):
        c = gi * DIL_WIDTH
        segs += [(o_qb + c, DIL_WIDTH, True, dil, qscale, False),
                 (o_kb + c, DIL_WIDTH, True, dil, 1.0, False),
                 (o_vb + c, DIL_WIDTH, False, dil, 1.0, False)]
    segs += [(0, A_Q, True, 1, qscale, False),
             (o_ka, A_KV, True, 1, 1.0, True),
             (o_va, A_KV, False, 1, 1.0, True)]

    outs = _qkv_proj(x.reshape(b * s, d), row(ln_in_g), row(ln_in_b), w_in[l].astype(BF16), tuple(segs),
                     cos, sin, tm=512)
    hn = outs[0].reshape(b, s, d)
    batched = lambda t: t.reshape(b, t.shape[0] // b, t.shape[1])
    qa, ka, va = (batched(t) for t in outs[1 + 3 * N_DIL:])
    oa = _win_attn(qa, ka, va, attn_sink[l], row(g_win[l]), tq=512)
    os_, ls = [], []
    for gi, (_, dil) in enumerate(DIL_PAIRS):
        qg, kg, vg = (batched(t) for t in outs[1 + 3 * gi:4 + 3 * gi])
        o, lse = _dil_attn(qg, kg, vg, dil, tq=max(2048 // dil, Q_BLK) if dil > 1 else 512)
        os_.append(o)
        ls.append(jnp.transpose(lse, (0, 2, 1, 3)).reshape(b, s, DIL_SLOTS))

    kx, vx = _mem_kv(mem, row(mem_ln_g[l]), row(mem_ln_b[l]), w_xk[l].astype(BF16), w_xv[l].astype(BF16))
    expand = jnp.repeat(jnp.eye(DIL_SLOTS, dtype=BF16), HEAD_DIM, axis=1)
    xscale = X_HEAD_DIM ** -0.5
    h2 = _mix_xattn(oa, os_, ls, hn, expand, row(g_dil[l]), w_mix_out[l].astype(BF16),
                    row(ln1_g[l]), row(ln1_b[l]), (w_xq[l] * xscale).astype(BF16), kx, vx,
                    w_xo[l].astype(BF16), row(ln2_g[l]), row(ln2_b[l]), tm=512)
    return _conv_glu(h2, w_gate[l].astype(BF16), w_up[l].astype(BF16), conv_w[l], row(conv_b[l]),
                     w_down[l].astype(BF16), row(ln3_g[l]), row(ln3_b[l]), tm=512)
```

```python
import functools

import numpy as np
import jax
import jax.numpy as jnp
from jax import lax
from jax.experimental import pallas as pl
from jax.experimental.pallas import tpu as pltpu

D_MODEL = 1024
HEAD_DIM = 64
WIN_Q_HEADS = 8
WIN_KV_HEADS = 2
WIN_HALF = 128
DIL_SLOTS = 8
DIL_PAIRS = ((128, 1), (512, 4), (2048, 16))
N_DIL = len(DIL_PAIRS)
ROT_DIM = HEAD_DIM // 4
ROT_HALF = ROT_DIM // 2
ROPE_THETA = 500000.0
X_HEADS = 4
X_HEAD_DIM = D_MODEL // X_HEADS
D_FF = 2816
CONV_WIDTH = 3
WIN_WIDTH = WIN_Q_HEADS * HEAD_DIM
DIL_WIDTH = DIL_SLOTS * HEAD_DIM
A_Q = WIN_WIDTH
A_KV = WIN_KV_HEADS * HEAD_DIM
B_QKV = N_DIL * DIL_WIDTH
DEPTH = 1
DEEPNORM_ALPHA = (2 * DEPTH) ** 0.25
LN_EPS = 1e-5
NEG_INF = -1e30
LOG2E = float(np.log2(np.e))

LANES = 128
TOK_PER_ROW = LANES // ROT_HALF
Q_BLK = 128
N_SUB = 2
DIL_SIDE = 64
VMEM_LIMIT = 56 * 1024 * 1024

F32 = jnp.float32
BF16 = jnp.bfloat16


def _ln(x, g, b):
    mu = jnp.mean(x, -1, keepdims=True)
    xc = x - mu
    var = jnp.mean(xc * xc, -1, keepdims=True)
    return xc * lax.rsqrt(var + LN_EPS) * g + b


def _rms(x, g):
    return x * lax.rsqrt(jnp.mean(x * x, -1, keepdims=True) + LN_EPS) * g


def _dot(a, b):
    return jnp.dot(a, b, preferred_element_type=F32)


def _dot_nt(a, b):
    return lax.dot_general(a, b, (((1,), (1,)), ((), ())), preferred_element_type=F32)


def _const_spec(shape):
    nd = len(shape)
    return pl.BlockSpec(shape, lambda *_: (0,) * nd, pipeline_mode=pl.Buffered(1))


def _params(sem):
    return pltpu.CompilerParams(dimension_semantics=sem, vmem_limit_bytes=VMEM_LIMIT)


def _rope_tab_kernel(pos_ref, invf_ref, cos_ref, sin_ref):
    ang = pos_ref[...].astype(F32) * invf_ref[...]
    cos_ref[...] = jnp.cos(ang)
    sin_ref[...] = jnp.sin(ang)


def _rope_tables(positions):
    inv_freq = ROPE_THETA ** (-jnp.arange(0, ROT_DIM, 2, dtype=F32) / ROT_DIM)
    pos_rep = jnp.repeat(positions.reshape(-1), ROT_HALF).reshape(-1, LANES)
    invf = jnp.tile(inv_freq, TOK_PER_ROW).reshape(1, LANES)
    return pl.pallas_call(
        _rope_tab_kernel,
        out_shape=(jax.ShapeDtypeStruct(pos_rep.shape, F32),) * 2,
        name="rope_tables",
    )(pos_rep, invf)


def _expand_rope(tab, tm):
    rows = jnp.broadcast_to(tab[:, None, :], (tm // TOK_PER_ROW, TOK_PER_ROW, LANES)).reshape(tm, LANES)
    tok = lax.broadcasted_iota(jnp.int32, (tm, LANES), 0) % TOK_PER_ROW
    lane = lax.broadcasted_iota(jnp.int32, (tm, LANES), 1)
    idx = tok * ROT_HALF + lane % ROT_HALF
    return jnp.take_along_axis(rows, idx, axis=1, mode="promise_in_bounds")


def _qkv_kernel(segs, tm, x_ref, g_ref, b_ref, w_ref, cos_ref, sin_ref, h_ref, *rest):
    out_refs, stage = rest[:-1], rest[-1]
    h = _ln(x_ref[...], g_ref[...], b_ref[...])
    h_ref[...] = h
    hb = h.astype(BF16)
    slot = lax.broadcasted_iota(jnp.int32, (tm, LANES), 1) % HEAD_DIM
    cos = _expand_rope(cos_ref[...], tm)
    sin = _expand_rope(sin_ref[...], tm)
    c_tab = jnp.where(slot < ROT_DIM, cos, 1.0)
    sa_tab = jnp.where(slot < ROT_HALF, sin, 0.0)
    sb_tab = jnp.where((slot >= ROT_HALF) & (slot < ROT_DIM), sin, 0.0)
    lo = slot == lax.broadcasted_iota(jnp.int32, (tm, LANES), 1)
    for (c0, width, rope, dil, scale, dup), o_ref in zip(segs, out_refs):
        z = _dot(hb, w_ref[:, c0:c0 + width])
        for j in range(width // LANES):
            zc = z[:, j * LANES:(j + 1) * LANES]
            if rope:
                up = pltpu.roll(zc, LANES - ROT_HALF, 1)
                dn = pltpu.roll(zc, ROT_HALF, 1)
                zc = zc * c_tab - up * sa_tab + dn * sb_tab
            if scale != 1.0:
                zc = zc * scale
            if dup:
                sw = pltpu.roll(zc, HEAD_DIM, 1)
                o_ref[:, 2 * j * LANES:(2 * j + 1) * LANES] = jnp.where(lo, zc, sw).astype(BF16)
                o_ref[:, (2 * j + 1) * LANES:(2 * j + 2) * LANES] = jnp.where(lo, sw, zc).astype(BF16)
            elif dil == 1:
                o_ref[:, j * LANES:(j + 1) * LANES] = zc.astype(BF16)
            else:
                stage[...] = zc
                for c in range(dil):
                    col = c * width + j * LANES
                    o_ref[:, col:col + LANES] = stage[pl.ds(c, tm // dil, stride=dil), :].astype(BF16)


def _qkv_proj(x2, ln_g, ln_b, w_all, segs, cos, sin, tm):
    t = x2.shape[0]
    wtot = w_all.shape[1]
    row = lambda r, w: pl.BlockSpec((r, w), lambda i: (i, 0))
    out_shape = [jax.ShapeDtypeStruct((t, D_MODEL), F32)]
    out_specs = [row(tm, D_MODEL)]
    for _, width, _, dil, _, dup in segs:
        wout = 2 * width if dup else dil * width
        out_shape.append(jax.ShapeDtypeStruct((t // dil, wout), BF16))
        out_specs.append(row(tm // dil, wout))
    return pl.pallas_call(
        functools.partial(_qkv_kernel, segs, tm),
        grid=(t // tm,),
        in_specs=[row(tm, D_MODEL), _const_spec((1, D_MODEL)), _const_spec((1, D_MODEL)),
                  _const_spec((D_MODEL, wtot)), row(tm // TOK_PER_ROW, LANES), row(tm // TOK_PER_ROW, LANES)],
        out_specs=out_specs,
        out_shape=out_shape,
        scratch_shapes=[pltpu.VMEM((tm, LANES), F32)],
        compiler_params=_params(("parallel",)),
        name="qkv_proj",
    )(x2, ln_g, ln_b, w_all, cos, sin)


def _band_bias(nq, nk, side):
    qi = lax.broadcasted_iota(jnp.int32, (nq, nk), 0)
    kj = lax.broadcasted_iota(jnp.int32, (nq, nk), 1)
    rel = kj - side - qi
    return jnp.where((rel >= -side) & (rel <= side), 0.0, NEG_INF).astype(F32)


def _band_kernel(side, dil, tq, kv_shared, is_win, *refs):
    if is_win:
        sink_ref, q_ref, kp_ref, km_ref, kn_ref, vp_ref, vm_ref, vn_ref, g_ref, o_ref, kcat, vcat = refs
    else:
        q_ref, kp_ref, km_ref, kn_ref, vp_ref, vm_ref, vn_ref, o_ref, lse_ref, kcat, vcat = refs
    i = pl.program_id(1)
    n_tiles = pl.num_programs(1)
    blk = Q_BLK
    nk = blk + 2 * side
    n_blk = tq // blk
    kcat[0:side] = kp_ref[0]
    kcat[side:side + tq] = km_ref[0]
    kcat[side + tq:] = kn_ref[0]
    vcat[0:side] = vp_ref[0]
    vcat[side:side + tq] = vm_ref[0]
    vcat[side + tq:] = vn_ref[0]
    width = q_ref.shape[-1] // dil
    n_pairs = width // LANES
    lo = lax.broadcasted_iota(jnp.int32, (blk, LANES), 1) < HEAD_DIM
    lse_w = 0 if is_win else lse_ref.shape[-1]
    lane8 = lax.broadcasted_iota(jnp.int32, (blk, max(lse_w, 1)), 1)
    band = _band_bias(blk, nk, side)
    kj = lax.broadcasted_iota(jnp.int32, (blk, nk), 1)
    band_first = jnp.where((i == 0) & (kj < side), NEG_INF, band)
    band_last = jnp.where((i == n_tiles - 1) & (kj >= side + blk), NEG_INF, band)
    biases = []
    for j in range(n_blk):
        bias = band
        if j == 0:
            bias = band_first
        if j == n_blk - 1:
            bias = band_last if j > 0 else jnp.minimum(band_first, band_last)
        biases.append(jnp.concatenate([bias, bias], 0))

    units = [(j, c, pr) for j in range(n_blk) for c in range(dil) for pr in range(n_pairs)]

    def kv_cols(c, pr):
        if kv_shared:
            return slice((pr // 2) * LANES, (pr // 2 + 1) * LANES)
        return slice(c * width + pr * LANES, c * width + (pr + 1) * LANES)

    def scores(j, c, pr):
        r0 = j * blk
        q2 = q_ref[0, r0:r0 + blk, c * width + pr * LANES:c * width + (pr + 1) * LANES]
        zero = jnp.zeros_like(q2)
        qs = jnp.concatenate([jnp.where(lo, q2, zero), jnp.where(lo, zero, q2)], 0)
        return _dot_nt(qs, kcat[r0:r0 + nk, kv_cols(c, pr)]) + biases[j]

    def softmax(s, pr):
        ps, rdens, lses = [], [], []
        for hf in range(2):
            sh = s[hf * blk:(hf + 1) * blk]
            m = jnp.max(sh, -1, keepdims=True)
            if is_win:
                sink = sink_ref[2 * pr + hf] * LOG2E
                m = jnp.maximum(m, sink)
            ph = jnp.exp2(sh - m)
            denom = jnp.sum(ph, -1, keepdims=True)
            if is_win:
                denom = denom + jnp.exp2(sink - m)
            else:
                lses.append(m + jnp.log2(denom))
            ps.append(ph.astype(BF16))
            rdens.append(1.0 / denom)
        return jnp.concatenate(ps, 0), rdens, lses

    pairs, lse8 = [], None

    def finish(unit, o, rdens, lses):
        nonlocal pairs, lse8
        j, c, pr = unit
        r0 = j * blk
        pairs.append(jnp.where(lo, o[:blk] * rdens[0], o[blk:] * rdens[1]))
        if not is_win:
            if lse8 is None:
                lse8 = jnp.zeros((blk, lse_w), F32)
            lse8 = jnp.where(lane8 == 2 * pr, lses[0], lse8)
            lse8 = jnp.where(lane8 == 2 * pr + 1, lses[1], lse8)
        if pr == n_pairs - 1:
            out = jnp.concatenate(pairs, -1)
            if is_win:
                o_ref[0, r0:r0 + blk, :] = _rms(out, g_ref[...]).astype(BF16)
            else:
                o_ref[0, r0:r0 + blk, c * width:(c + 1) * width] = out.astype(BF16)
                lse_ref[0, c, r0:r0 + blk, :] = lse8
            pairs, lse8 = [], None

    s_next = scores(*units[0])
    pending = None
    for n, (j, c, pr) in enumerate(units):
        s = s_next
        if n + 1 < len(units):
            s_next = scores(*units[n + 1])
        p, rdens, lses = softmax(s, pr)
        o = _dot(p, vcat[j * blk:j * blk + nk, kv_cols(c, pr)])
        if pending is not None:
            finish(*pending)
        pending = ((j, c, pr), o, rdens, lses)
    finish(*pending)


def _band_specs(tq, side, n_rows, width):
    per = tq // side
    nblk = n_rows // side
    main = pl.BlockSpec((1, tq, width), lambda bi, i: (bi, i, 0))
    prev = pl.BlockSpec((1, side, width), lambda bi, i: (bi, jnp.maximum(i * per - 1, 0), 0))
    nxt = pl.BlockSpec((1, side, width), lambda bi, i: (bi, jnp.minimum((i + 1) * per, nblk - 1), 0))
    return prev, main, nxt


def _win_attn(qa, ka, va, sink, g_win, tq):
    b, s, _ = qa.shape
    side = WIN_HALF
    kw = ka.shape[-1]
    prev, main, nxt = _band_specs(tq, side, s, kw)
    return pl.pallas_call(
        functools.partial(_band_kernel, side, 1, tq, True, True),
        grid=(b, s // tq),
        in_specs=[pl.BlockSpec(memory_space=pltpu.SMEM),
                  pl.BlockSpec((1, tq, WIN_WIDTH), lambda bi, i: (bi, i, 0)),
                  prev, main, nxt, prev, main, nxt,
                  _const_spec((1, WIN_WIDTH))],
        out_specs=pl.BlockSpec((1, tq, WIN_WIDTH), lambda bi, i: (bi, i, 0)),
        out_shape=jax.ShapeDtypeStruct((b, s, WIN_WIDTH), BF16),
        scratch_shapes=[pltpu.VMEM((tq + 2 * side, kw), BF16), pltpu.VMEM((tq + 2 * side, kw), BF16)],
        compiler_params=_params(("parallel", "parallel")),
        name="win_attn",
    )(sink, qa, ka, ka, ka, va, va, va, g_win)


def _dil_attn(qv, kv, vv, dil, tq):
    b, ln, wr = qv.shape
    side = DIL_SIDE
    lse_w = DIL_SLOTS if dil == 1 else LANES
    prev, main, nxt = _band_specs(tq, side, ln, wr)
    return pl.pallas_call(
        functools.partial(_band_kernel, side, dil, tq, False, False),
        grid=(b, ln // tq),
        in_specs=[main, prev, main, nxt, prev, main, nxt],
        out_specs=[main, pl.BlockSpec((1, dil, tq, lse_w), lambda bi, i: (bi, 0, i, 0))],
        out_shape=[jax.ShapeDtypeStruct((b, ln, wr), BF16),
                   jax.ShapeDtypeStruct((b, dil, ln, lse_w), F32)],
        scratch_shapes=[pltpu.VMEM((tq + 2 * side, wr), BF16), pltpu.VMEM((tq + 2 * side, wr), BF16)],
        compiler_params=_params(("parallel", "parallel")),
        name=f"dil_attn_{dil}",
    )(qv, kv, kv, kv, vv, vv, vv)


def _mem_kernel(mem_ref, g_ref, b_ref, wk_ref, wv_ref, k_ref, v_ref):
    mn = _ln(mem_ref[0], g_ref[...], b_ref[...]).astype(BF16)
    k_ref[0] = _dot(mn, wk_ref[...]).astype(BF16)
    v_ref[0] = _dot(mn, wv_ref[...]).astype(BF16)


def _mem_kv(mem, g, b, wk, wv):
    bsz, m, d = mem.shape
    blk = pl.BlockSpec((1, m, d), lambda bi: (bi, 0, 0))
    return pl.pallas_call(
        _mem_kernel,
        grid=(bsz,),
        in_specs=[blk, _const_spec((1, d)), _const_spec((1, d)), _const_spec((d, d)), _const_spec((d, d))],
        out_specs=[blk, blk],
        out_shape=[jax.ShapeDtypeStruct((bsz, m, d), BF16)] * 2,
        compiler_params=_params(("parallel",)),
        name="mem_kv",
    )(mem, g, b, wk, wv)


def _round_robin(gens):
    live = list(gens)
    while live:
        for g in list(live):
            try:
                next(g)
            except StopIteration:
                live.remove(g)


def _mix_kernel(tm, oa_ref, o0_ref, o1_ref, o2_ref, l0_ref, l1_ref, l2_ref, h_ref, e_ref, gd_ref, wo_ref,
                g1_ref, b1_ref, wq_ref, kx_ref, vx_ref, wxo_ref, g2_ref, b2_ref, out_ref, stage, lstage):
    ts = tm // N_SUB
    n_chunks = DIL_WIDTH // LANES

    def sub_tile(k):
        rows = slice(k * ts, (k + 1) * ts)
        ls = []
        for gi, (l_ref, (_, dil)) in enumerate(zip((l0_ref, l1_ref, l2_ref), DIL_PAIRS)):
            if dil == 1:
                ls.append(l_ref[0, rows, :])
            else:
                rrows = slice(k * ts // dil, (k + 1) * ts // dil)
                for c in range(dil):
                    lstage[gi - 1, pl.ds(k * ts + c, ts // dil, stride=dil), :] = l_ref[0, c, rrows, :]
                ls.append(lstage[gi - 1, rows, 0:DIL_SLOTS])
        mx = jnp.maximum(jnp.maximum(ls[0], ls[1]), ls[2])
        es = [jnp.exp2(l - mx) for l in ls]
        tot = es[0] + es[1] + es[2]
        ob = None
        for gi, (e, o_ref, (_, dil)) in enumerate(zip(es, (o0_ref, o1_ref, o2_ref), DIL_PAIRS)):
            wexp = _dot((e / tot).astype(BF16), e_ref[...])
            if dil == 1:
                og = o_ref[0, rows, :].astype(F32)
            else:
                rrows = slice(k * ts // dil, (k + 1) * ts // dil)
                for c in range(dil):
                    for j in range(n_chunks):
                        col = c * DIL_WIDTH + j * LANES
                        stage[gi - 1, j, pl.ds(k * ts + c, ts // dil, stride=dil), :] = (
                            o_ref[0, rrows, col:col + LANES].astype(F32))
                og = jnp.concatenate([stage[gi - 1, j, rows, :] for j in range(n_chunks)], -1)
            term = wexp * og
            ob = term if ob is None else ob + term
        obn = _rms(ob, gd_ref[...]).astype(BF16)
        yield
        mix = _dot(oa_ref[0, rows, :], wo_ref[0:WIN_WIDTH, :]) + _dot(obn, wo_ref[WIN_WIDTH:, :])
        yield
        h1 = _ln(DEEPNORM_ALPHA * h_ref[0, rows, :] + mix, g1_ref[...], b1_ref[...])
        yield
        q = _dot(h1.astype(BF16), wq_ref[...]).astype(BF16)
        yield
        heads = []
        for hh in range(X_HEADS):
            cols = slice(hh * X_HEAD_DIM, (hh + 1) * X_HEAD_DIM)
            s = _dot_nt(q[:, cols], kx_ref[0, :, cols])
            yield
            m = jnp.max(s, -1, keepdims=True)
            p = jnp.exp(s - m)
            denom = jnp.sum(p, -1, keepdims=True)
            yield
            heads.append((_dot(p.astype(BF16), vx_ref[0, :, cols]) / denom).astype(BF16))
        xa = _dot(jnp.concatenate(heads, -1), wxo_ref[...])
        yield
        out_ref[0, rows, :] = _ln(DEEPNORM_ALPHA * h1 + xa, g2_ref[...], b2_ref[...])

    _round_robin([sub_tile(k) for k in range(N_SUB)])


def _mix_xattn(oa, os_, ls, h, expand, g_dil, w_out, g1, b1, w_xq, kx, vx, w_xo, g2, b2, tm):
    b, s, d = h.shape
    m = kx.shape[1]
    tok = lambda w: pl.BlockSpec((1, tm, w), lambda bi, i: (bi, i, 0))
    res = lambda dil: pl.BlockSpec((1, tm // dil, dil * DIL_WIDTH), lambda bi, i: (bi, i, 0))
    memspec = pl.BlockSpec((1, m, d), lambda bi, i: (bi, 0, 0))
    return pl.pallas_call(
        functools.partial(_mix_kernel, tm),
        grid=(b, s // tm),
        in_specs=[tok(WIN_WIDTH)] + [res(dil) for _, dil in DIL_PAIRS] +
                 [tok(DIL_SLOTS) if dil == 1 else
                  pl.BlockSpec((1, dil, tm // dil, LANES), lambda bi, i: (bi, 0, i, 0)) for _, dil in DIL_PAIRS] +
                 [tok(d),
                  _const_spec(expand.shape), _const_spec((1, DIL_WIDTH)), _const_spec((d, d)),
                  _const_spec((1, d)), _const_spec((1, d)), _const_spec((d, d)), memspec, memspec,
                  _const_spec((d, d)), _const_spec((1, d)), _const_spec((1, d))],
        out_specs=tok(d),
        out_shape=jax.ShapeDtypeStruct((b, s, d), F32),
        scratch_shapes=[pltpu.VMEM((N_DIL - 1, DIL_WIDTH // LANES, tm, LANES), F32),
                        pltpu.VMEM((N_DIL - 1, tm, LANES), F32)],
        compiler_params=_params(("parallel", "parallel")),
        name="mix_xattn",
    )(oa, *os_, *ls, h, expand, g_dil, w_out, g1, b1, w_xq, kx, vx, w_xo, g2, b2)


FF_CHUNKS = ((0, 1024), (1024, 1024), (2048, 768))
HALO = 8


def _ffn_kernel(tm, hp_ref, hm_ref, hn_ref, wg_ref, wu_ref, cw_ref, cb_ref, wd_ref, g3_ref, b3_ref, out_ref, gs):
    i = pl.program_id(1)
    n = pl.num_programs(1)
    hm = hm_ref[0]
    hb = hm.astype(BF16)
    hp = jnp.where(i > 0, hp_ref[0], 0.0).astype(BF16)
    hn = jnp.where(i < n - 1, hn_ref[0], 0.0).astype(BF16)
    hcat = jnp.concatenate([hp, hb, hn], 0)
    acc = None
    for c0, width in FF_CHUNKS:
        cols = slice(c0, c0 + width)
        gs[:, 0:width] = _dot(hcat, wg_ref[:, cols])
        g = cb_ref[:, cols]
        for j in range(CONV_WIDTH):
            g = g + gs[HALO - 1 + j:HALO - 1 + j + tm, 0:width] * cw_ref[j:j + 1, cols]
        u = _dot(hb, wu_ref[:, cols])
        act = 0.5 * g * (1.0 + lax.erf(g * np.float32(np.sqrt(0.5)))) * u
        part = _dot(act.astype(BF16), wd_ref[cols, :])
        acc = part if acc is None else acc + part
    out_ref[0] = _ln(DEEPNORM_ALPHA * hm + acc, g3_ref[...], b3_ref[...])


def _conv_glu(h2, wg, wu, cw, cb, wd, g3, b3, tm):
    b, s, d = h2.shape
    per = tm // HALO
    nblk = s // HALO
    main = pl.BlockSpec((1, tm, d), lambda bi, i: (bi, i, 0))
    prev = pl.BlockSpec((1, HALO, d), lambda bi, i: (bi, jnp.maximum(i * per - 1, 0), 0))
    nxt = pl.BlockSpec((1, HALO, d), lambda bi, i: (bi, jnp.minimum((i + 1) * per, nblk - 1), 0))
    wmax = max(w for _, w in FF_CHUNKS)
    return pl.pallas_call(
        functools.partial(_ffn_kernel, tm),
        grid=(b, s // tm),
        in_specs=[prev, main, nxt, _const_spec((d, D_FF)), _const_spec((d, D_FF)),
                  _const_spec((CONV_WIDTH, D_FF)), _const_spec((1, D_FF)), _const_spec((D_FF, d)),
                  _const_spec((1, d)), _const_spec((1, d))],
        out_specs=main,
        out_shape=jax.ShapeDtypeStruct((b, s, d), F32),
        scratch_shapes=[pltpu.VMEM((tm + 2 * HALO, wmax), F32)],
        compiler_params=_params(("parallel", "parallel")),
        name="conv_glu",
    )(h2, h2, h2, wg, wu, cw, cb, wd, g3, b3)


def kernel(x, mem, positions, ln_in_g, ln_in_b, w_in, attn_sink, g_win, g_dil, w_mix_out, ln1_g, ln1_b, mem_ln_g, mem_ln_b, w_xq, w_xk, w_xv, w_xo, ln2_g, ln2_b, w_gate, w_up, conv_w, conv_b, w_down, ln3_g, ln3_b):
    b, s, d = x.shape
    assert DEPTH == 1
    l = 0
    row = lambda v: v.reshape(1, -1)
    cos, sin = _rope_tables(positions)

    qscale = HEAD_DIM ** -0.5 * LOG2E
    o_ka, o_va, o_qb = A_Q, A_Q + A_KV, A_Q + 2 * A_KV
    o_kb, o_vb = o_qb + B_QKV, o_qb + 2 * B_QKV
    segs = [(0, A_Q, True, 1, qscale, False),
            (o_ka, A_KV, True, 1, 1.0, True),
            (o_va, A_KV, False, 1, 1.0, True)]
    for gi, (_, dil) in enumerate(DIL_PAIRS):
        c = gi * DIL_WIDTH
        segs += [(o_qb + c, DIL_WIDTH, True, dil, qscale, False),
                 (o_kb + c, DIL_WIDTH, True, dil, 1.0, False),
                 (o_vb + c, DIL_WIDTH, False, dil, 1.0, False)]

    outs = _qkv_proj(x.reshape(b * s, d), row(ln_in_g), row(ln_in_b), w_in[l].astype(BF16), tuple(segs),
                     cos, sin, tm=512)
    hn = outs[0].reshape(b, s, d)
    batched = lambda t: t.reshape(b, t.shape[0] // b, t.shape[1])
    qa, ka, va = (batched(t) for t in outs[1:4])
    oa = _win_attn(qa, ka, va, attn_sink[l], row(g_win[l]), tq=512)
    os_, ls = [], []
    for gi, (_, dil) in enumerate(DIL_PAIRS):
        qg, kg, vg = (batched(t) for t in outs[4 + 3 * gi:7 + 3 * gi])
        o, lse = _dil_attn(qg, kg, vg, dil, tq=max(2048 // dil, Q_BLK) if dil > 1 else 512)
        os_.append(o)
        ls.append(lse.reshape(b, s, DIL_SLOTS) if dil == 1 else lse)

    kx, vx = _mem_kv(mem, row(mem_ln_g[l]), row(mem_ln_b[l]), w_xk[l].astype(BF16), w_xv[l].astype(BF16))
    expand = jnp.repeat(jnp.eye(DIL_SLOTS, dtype=BF16), HEAD_DIM, axis=1)
    xscale = X_HEAD_DIM ** -0.5
    h2 = _mix_xattn(oa, os_, ls, hn, expand, row(g_dil[l]), w_mix_out[l].astype(BF16),
                    row(ln1_g[l]), row(ln1_b[l]), (w_xq[l] * xscale).astype(BF16), kx, vx,
                    w_xo[l].astype(BF16), row(ln2_g[l]), row(ln2_b[l]), tm=512)
    return _conv_glu(h2, w_gate[l].astype(BF16), w_up[l].astype(BF16), conv_w[l], row(conv_b[l]),
                     w_down[l].astype(BF16), row(ln3_g[l]), row(ln3_b[l]), tm=1024)
```

```python
import functools

import numpy as np
import jax
import jax.numpy as jnp
from jax import lax
from jax.experimental import pallas as pl
from jax.experimental.pallas import tpu as pltpu

D_MODEL = 1024
HEAD_DIM = 64
WIN_Q_HEADS = 8
WIN_KV_HEADS = 2
WIN_HALF = 128
DIL_SLOTS = 8
DIL_PAIRS = ((128, 1), (512, 4), (2048, 16))
N_DIL = len(DIL_PAIRS)
ROT_DIM = HEAD_DIM // 4
ROT_HALF = ROT_DIM // 2
ROPE_THETA = 500000.0
X_HEADS = 4
X_HEAD_DIM = D_MODEL // X_HEADS
D_FF = 2816
CONV_WIDTH = 3
WIN_WIDTH = WIN_Q_HEADS * HEAD_DIM
DIL_WIDTH = DIL_SLOTS * HEAD_DIM
A_Q = WIN_WIDTH
A_KV = WIN_KV_HEADS * HEAD_DIM
B_QKV = N_DIL * DIL_WIDTH
DEPTH = 1
DEEPNORM_ALPHA = (2 * DEPTH) ** 0.25
LN_EPS = 1e-5
NEG_INF = -1e30
LOG2E = float(np.log2(np.e))

LANES = 128
TOK_PER_ROW = LANES // ROT_HALF
Q_BLK = 128
N_SUB = 2
DIL_SIDE = 64
VMEM_LIMIT = 56 * 1024 * 1024

F32 = jnp.float32
BF16 = jnp.bfloat16


def _ln(x, g, b):
    mu = jnp.mean(x, -1, keepdims=True)
    xc = x - mu
    var = jnp.mean(xc * xc, -1, keepdims=True)
    return xc * lax.rsqrt(var + LN_EPS) * g + b


def _rms(x, g):
    return x * lax.rsqrt(jnp.mean(x * x, -1, keepdims=True) + LN_EPS) * g


def _dot(a, b):
    return jnp.dot(a, b, preferred_element_type=F32)


def _dot_nt(a, b):
    return lax.dot_general(a, b, (((1,), (1,)), ((), ())), preferred_element_type=F32)


def _const_spec(shape):
    nd = len(shape)
    return pl.BlockSpec(shape, lambda *_: (0,) * nd, pipeline_mode=pl.Buffered(1))


def _params(sem):
    return pltpu.CompilerParams(dimension_semantics=sem, vmem_limit_bytes=VMEM_LIMIT)


def _round_robin(gens):
    live = list(gens)
    while live:
        for g in list(live):
            try:
                next(g)
            except StopIteration:
                live.remove(g)


def _rope_tab_kernel(pos_ref, invf_ref, cos_ref, sin_ref):
    ang = pos_ref[...].astype(F32) * invf_ref[...]
    cos_ref[...] = jnp.cos(ang)
    sin_ref[...] = jnp.sin(ang)


def _rope_tables(positions):
    inv_freq = ROPE_THETA ** (-jnp.arange(0, ROT_DIM, 2, dtype=F32) / ROT_DIM)
    pos_rep = jnp.repeat(positions.reshape(-1), ROT_HALF).reshape(-1, LANES)
    invf = jnp.tile(inv_freq, TOK_PER_ROW).reshape(1, LANES)
    return pl.pallas_call(
        _rope_tab_kernel,
        out_shape=(jax.ShapeDtypeStruct(pos_rep.shape, F32),) * 2,
        name="rope_tables",
    )(pos_rep, invf)


def _expand_rope(tab, tm):
    rows = jnp.broadcast_to(tab[:, None, :], (tm // TOK_PER_ROW, TOK_PER_ROW, LANES)).reshape(tm, LANES)
    tok = lax.broadcasted_iota(jnp.int32, (tm, LANES), 0) % TOK_PER_ROW
    lane = lax.broadcasted_iota(jnp.int32, (tm, LANES), 1)
    idx = tok * ROT_HALF + lane % ROT_HALF
    return jnp.take_along_axis(rows, idx, axis=1, mode="promise_in_bounds")


def _qkv_kernel(segs, tm, scales, x_ref, g_ref, b_ref, w_ref, cos_ref, sin_ref, *rest):
    n_cast = len(scales)
    cast_in, rest = rest[:n_cast], rest[n_cast:]
    h_ref, out_refs, cast_out, stage = rest[0], rest[1:1 + len(segs)], rest[1 + len(segs):-1], rest[-1]
    for scale, wi_ref, wo_ref in zip(scales, cast_in, cast_out):
        wo_ref[...] = (wi_ref[...] * scale if scale != 1.0 else wi_ref[...]).astype(BF16)
    h = _ln(x_ref[...], g_ref[...], b_ref[...])
    h_ref[...] = h
    hb = h.astype(BF16)
    slot = lax.broadcasted_iota(jnp.int32, (tm, LANES), 1) % HEAD_DIM
    cos = _expand_rope(cos_ref[...], tm)
    sin = _expand_rope(sin_ref[...], tm)
    c_tab = jnp.where(slot < ROT_DIM, cos, 1.0)
    sa_tab = jnp.where(slot < ROT_HALF, sin, 0.0)
    sb_tab = jnp.where((slot >= ROT_HALF) & (slot < ROT_DIM), sin, 0.0)
    lo = slot == lax.broadcasted_iota(jnp.int32, (tm, LANES), 1)
    for (c0, width, rope, dil, scale, dup), o_ref in zip(segs, out_refs):
        z = _dot(hb, w_ref[:, c0:c0 + width])
        for j in range(width // LANES):
            zc = z[:, j * LANES:(j + 1) * LANES]
            if rope:
                up = pltpu.roll(zc, LANES - ROT_HALF, 1)
                dn = pltpu.roll(zc, ROT_HALF, 1)
                zc = zc * c_tab - up * sa_tab + dn * sb_tab
            if scale != 1.0:
                zc = zc * scale
            if dup:
                sw = pltpu.roll(zc, HEAD_DIM, 1)
                o_ref[:, 2 * j * LANES:(2 * j + 1) * LANES] = jnp.where(lo, zc, sw).astype(BF16)
                o_ref[:, (2 * j + 1) * LANES:(2 * j + 2) * LANES] = jnp.where(lo, sw, zc).astype(BF16)
            elif dil == 1:
                o_ref[:, j * LANES:(j + 1) * LANES] = zc.astype(BF16)
            else:
                stage[...] = zc
                for c in range(dil):
                    col = c * width + j * LANES
                    o_ref[:, col:col + LANES] = stage[pl.ds(c, tm // dil, stride=dil), :].astype(BF16)


BF16_ROWS = 16


def _qkv_proj(x2, ln_g, ln_b, w_all, segs, cos, sin, to_cast, tm):
    t = x2.shape[0]
    steps = t // tm
    wtot = w_all.shape[1]
    row = lambda r, w: pl.BlockSpec((r, w), lambda i: (i, 0))
    out_shape = [jax.ShapeDtypeStruct((t, D_MODEL), F32)]
    out_specs = [row(tm, D_MODEL)]
    for _, width, _, dil, _, dup in segs:
        wout = 2 * width if dup else dil * width
        out_shape.append(jax.ShapeDtypeStruct((t // dil, wout), BF16))
        out_specs.append(row(tm // dil, wout))
    cast_specs = []
    for w, _ in to_cast:
        rows, cols = w.shape
        nblk = max(n for n in range(1, steps + 1) if steps % n == 0 and rows % (n * BF16_ROWS) == 0)
        spec = pl.BlockSpec((rows // nblk, cols), lambda i, every=steps // nblk: (i // every, 0))
        cast_specs.append(spec)
        out_shape.append(jax.ShapeDtypeStruct((rows, cols), BF16))
        out_specs.append(spec)
    return pl.pallas_call(
        functools.partial(_qkv_kernel, segs, tm, tuple(s for _, s in to_cast)),
        grid=(steps,),
        in_specs=[row(tm, D_MODEL), _const_spec((1, D_MODEL)), _const_spec((1, D_MODEL)),
                  _const_spec((D_MODEL, wtot)), row(tm // TOK_PER_ROW, LANES), row(tm // TOK_PER_ROW, LANES)]
                 + cast_specs,
        out_specs=out_specs,
        out_shape=out_shape,
        scratch_shapes=[pltpu.VMEM((tm, LANES), F32)],
        compiler_params=_params(("arbitrary",)),
        name="qkv_proj",
    )(x2, ln_g, ln_b, w_all, cos, sin, *[w for w, _ in to_cast])


def _band_bias(nq, nk, side):
    qi = lax.broadcasted_iota(jnp.int32, (nq, nk), 0)
    kj = lax.broadcasted_iota(jnp.int32, (nq, nk), 1)
    rel = kj - side - qi
    return jnp.where((rel >= -side) & (rel <= side), 0.0, NEG_INF).astype(F32)


def _fill_halo_buffer(cat, prev, main, nxt, side, tq):
    cat[0:side] = prev
    cat[side:side + tq] = main
    cat[side + tq:] = nxt


def _band_units(side, dil, tq, tile, n_tiles, sink_ref, q_ref, kcat, vcat, emit, lse_w=0):
    is_win = sink_ref is not None
    blk = Q_BLK
    nk = blk + 2 * side
    n_blk = tq // blk
    width = q_ref.shape[-1] // dil
    n_pairs = width // LANES
    lo = lax.broadcasted_iota(jnp.int32, (blk, LANES), 1) < HEAD_DIM
    lane8 = lax.broadcasted_iota(jnp.int32, (blk, max(lse_w, 1)), 1)
    band = _band_bias(blk, nk, side)
    kj = lax.broadcasted_iota(jnp.int32, (blk, nk), 1)
    band_first = jnp.where((tile == 0) & (kj < side), NEG_INF, band)
    band_last = jnp.where((tile == n_tiles - 1) & (kj >= side + blk), NEG_INF, band)
    biases = []
    for j in range(n_blk):
        bias = band
        if j == 0:
            bias = band_first
        if j == n_blk - 1:
            bias = band_last if j > 0 else jnp.minimum(band_first, band_last)
        biases.append(jnp.concatenate([bias, bias], 0))

    units = [(j, c, pr) for j in range(n_blk) for c in range(dil) for pr in range(n_pairs)]

    def kv_cols(c, pr):
        if is_win:
            return slice((pr // 2) * LANES, (pr // 2 + 1) * LANES)
        return slice(c * width + pr * LANES, c * width + (pr + 1) * LANES)

    def scores(j, c, pr):
        r0 = j * blk
        q2 = q_ref[r0:r0 + blk, c * width + pr * LANES:c * width + (pr + 1) * LANES]
        zero = jnp.zeros_like(q2)
        qs = jnp.concatenate([jnp.where(lo, q2, zero), jnp.where(lo, zero, q2)], 0)
        return _dot_nt(qs, kcat[r0:r0 + nk, kv_cols(c, pr)]) + biases[j]

    def softmax(s, pr):
        ps, rdens, lses = [], [], []
        for hf in range(2):
            sh = s[hf * blk:(hf + 1) * blk]
            m = jnp.max(sh, -1, keepdims=True)
            if is_win:
                sink = sink_ref[2 * pr + hf] * LOG2E
                m = jnp.maximum(m, sink)
            ph = jnp.exp2(sh - m)
            denom = jnp.sum(ph, -1, keepdims=True)
            if is_win:
                denom = denom + jnp.exp2(sink - m)
            else:
                lses.append(m + jnp.log2(denom))
            ps.append(ph.astype(BF16))
            rdens.append(1.0 / denom)
        return jnp.concatenate(ps, 0), rdens, lses

    pairs, lse8 = [], None

    def finish(unit, o, rdens, lses):
        nonlocal pairs, lse8
        j, c, pr = unit
        pairs.append(jnp.where(lo, o[:blk] * rdens[0], o[blk:] * rdens[1]))
        if lse_w:
            if lse8 is None:
                lse8 = jnp.zeros((blk, lse_w), F32)
            lse8 = jnp.where(lane8 == 2 * pr, lses[0], lse8)
            lse8 = jnp.where(lane8 == 2 * pr + 1, lses[1], lse8)
        if pr == n_pairs - 1:
            emit(j, c, jnp.concatenate(pairs, -1), lse8)
            pairs, lse8 = [], None

    s_next = scores(*units[0])
    pending = None
    for n, (j, c, pr) in enumerate(units):
        s = s_next
        if n + 1 < len(units):
            s_next = scores(*units[n + 1])
        p, rdens, lses = softmax(s, pr)
        o = _dot(p, vcat[j * blk:j * blk + nk, kv_cols(c, pr)])
        if pending is not None:
            finish(*pending)
        pending = ((j, c, pr), o, rdens, lses)
    finish(*pending)


def _dil_kernel(dil, tq, q_ref, kp_ref, km_ref, kn_ref, vp_ref, vm_ref, vn_ref, o_ref, lse_ref, kcat, vcat):
    _fill_halo_buffer(kcat, kp_ref[0], km_ref[0], kn_ref[0], DIL_SIDE, tq)
    _fill_halo_buffer(vcat, vp_ref[0], vm_ref[0], vn_ref[0], DIL_SIDE, tq)
    width = q_ref.shape[-1] // dil

    def emit(j, c, out, lse):
        rows = slice(j * Q_BLK, (j + 1) * Q_BLK)
        o_ref[0, rows, c * width:(c + 1) * width] = out.astype(BF16)
        lse_ref[0, c, rows, :] = lse

    _band_units(DIL_SIDE, dil, tq, pl.program_id(1), pl.num_programs(1), None, q_ref.at[0], kcat, vcat, emit,
                lse_w=lse_ref.shape[-1])


def _band_specs(tq, side, n_rows, width):
    per = tq // side
    nblk = n_rows // side
    main = pl.BlockSpec((1, tq, width), lambda bi, i: (bi, i, 0))
    prev = pl.BlockSpec((1, side, width), lambda bi, i: (bi, jnp.maximum(i * per - 1, 0), 0))
    nxt = pl.BlockSpec((1, side, width), lambda bi, i: (bi, jnp.minimum((i + 1) * per, nblk - 1), 0))
    return prev, main, nxt


def _win_kernel(tq, sink_ref, q_ref, kp_ref, km_ref, kn_ref, vp_ref, vm_ref, vn_ref, g_ref, o_ref, kcat, vcat):
    _fill_halo_buffer(kcat, kp_ref[0], km_ref[0], kn_ref[0], WIN_HALF, tq)
    _fill_halo_buffer(vcat, vp_ref[0], vm_ref[0], vn_ref[0], WIN_HALF, tq)

    def emit(j, c, out, lse):
        o_ref[0, j * Q_BLK:(j + 1) * Q_BLK, :] = _rms(out, g_ref[...]).astype(BF16)

    _band_units(WIN_HALF, 1, tq, pl.program_id(1), pl.num_programs(1), sink_ref, q_ref.at[0], kcat, vcat, emit)


def _win_attn(qa, ka, va, sink, g_win, tq):
    b, s, _ = qa.shape
    side = WIN_HALF
    kw = ka.shape[-1]
    prev, main, nxt = _band_specs(tq, side, s, kw)
    qspec = pl.BlockSpec((1, tq, WIN_WIDTH), lambda bi, i: (bi, i, 0))
    return pl.pallas_call(
        functools.partial(_win_kernel, tq),
        grid=(b, s // tq),
        in_specs=[pl.BlockSpec(memory_space=pltpu.SMEM), qspec, prev, main, nxt, prev, main, nxt,
                  _const_spec((1, WIN_WIDTH))],
        out_specs=qspec,
        out_shape=jax.ShapeDtypeStruct((b, s, WIN_WIDTH), BF16),
        scratch_shapes=[pltpu.VMEM((tq + 2 * side, kw), BF16), pltpu.VMEM((tq + 2 * side, kw), BF16)],
        compiler_params=_params(("parallel", "parallel")),
        name="win_attn",
    )(sink, qa, ka, ka, ka, va, va, va, g_win)


def _dil_attn(qv, kv, vv, dil, tq):
    b, ln, wr = qv.shape
    side = DIL_SIDE
    lse_w = DIL_SLOTS if dil == 1 else LANES
    prev, main, nxt = _band_specs(tq, side, ln, wr)
    return pl.pallas_call(
        functools.partial(_dil_kernel, dil, tq),
        grid=(b, ln // tq),
        in_specs=[main, prev, main, nxt, prev, main, nxt],
        out_specs=[main, pl.BlockSpec((1, dil, tq, lse_w), lambda bi, i: (bi, 0, i, 0))],
        out_shape=[jax.ShapeDtypeStruct((b, ln, wr), BF16),
                   jax.ShapeDtypeStruct((b, dil, ln, lse_w), F32)],
        scratch_shapes=[pltpu.VMEM((tq + 2 * side, wr), BF16), pltpu.VMEM((tq + 2 * side, wr), BF16)],
        compiler_params=_params(("parallel", "parallel")),
        name=f"dil_attn_{dil}",
    )(qv, kv, kv, kv, vv, vv, vv)


def _mem_kernel(mem_ref, g_ref, b_ref, wk_ref, wv_ref, k_ref, v_ref):
    mn = _ln(mem_ref[0], g_ref[...], b_ref[...]).astype(BF16)
    k_ref[0] = _dot(mn, wk_ref[...]).astype(BF16)
    v_ref[0] = _dot(mn, wv_ref[...]).astype(BF16)


def _mem_kv(mem, g, b, wk, wv):
    bsz, m, d = mem.shape
    blk = pl.BlockSpec((1, m, d), lambda bi: (bi, 0, 0))
    return pl.pallas_call(
        _mem_kernel,
        grid=(bsz,),
        in_specs=[blk, _const_spec((1, d)), _const_spec((1, d)), _const_spec((d, d)), _const_spec((d, d))],
        out_specs=[blk, blk],
        out_shape=[jax.ShapeDtypeStruct((bsz, m, d), BF16)] * 2,
        compiler_params=_params(("parallel",)),
        name="mem_kv",
    )(mem, g, b, wk, wv)


def _mix_kernel(tm, oa_ref, o0_ref, o1_ref, o2_ref, l0_ref, l1_ref, l2_ref, h_ref, e_ref, gd_ref, wo_ref,
                g1_ref, b1_ref, wq_ref, kx_ref, vx_ref, wxo_ref, g2_ref, b2_ref, out_ref, stage, lstage):
    ts = tm // N_SUB
    n_chunks = DIL_WIDTH // LANES

    def sub_tile(k):
        rows = slice(k * ts, (k + 1) * ts)
        ls = []
        for gi, (l_ref, (_, dil)) in enumerate(zip((l0_ref, l1_ref, l2_ref), DIL_PAIRS)):
            if dil == 1:
                ls.append(l_ref[0, rows, :])
            else:
                rrows = slice(k * ts // dil, (k + 1) * ts // dil)
                for c in range(dil):
                    lstage[gi - 1, pl.ds(k * ts + c, ts // dil, stride=dil), :] = l_ref[0, c, rrows, :]
                ls.append(lstage[gi - 1, rows, 0:DIL_SLOTS])
        mx = jnp.maximum(jnp.maximum(ls[0], ls[1]), ls[2])
        es = [jnp.exp2(l - mx) for l in ls]
        tot = es[0] + es[1] + es[2]
        ob = None
        for gi, (e, o_ref, (_, dil)) in enumerate(zip(es, (o0_ref, o1_ref, o2_ref), DIL_PAIRS)):
            wexp = _dot((e / tot).astype(BF16), e_ref[...])
            if dil == 1:
                og = o_ref[0, rows, :].astype(F32)
            else:
                rrows = slice(k * ts // dil, (k + 1) * ts // dil)
                for c in range(dil):
                    for j in range(n_chunks):
                        col = c * DIL_WIDTH + j * LANES
                        stage[gi - 1, j, pl.ds(k * ts + c, ts // dil, stride=dil), :] = (
                            o_ref[0, rrows, col:col + LANES].astype(F32))
                og = jnp.concatenate([stage[gi - 1, j, rows, :] for j in range(n_chunks)], -1)
            term = wexp * og
            ob = term if ob is None else ob + term
        obn = _rms(ob, gd_ref[...]).astype(BF16)
        yield
        mix = _dot(oa_ref[0, rows, :], wo_ref[0:WIN_WIDTH, :]) + _dot(obn, wo_ref[WIN_WIDTH:, :])
        yield
        h1 = _ln(DEEPNORM_ALPHA * h_ref[0, rows, :] + mix, g1_ref[...], b1_ref[...])
        yield
        q = _dot(h1.astype(BF16), wq_ref[...]).astype(BF16)
        yield
        heads = []
        for hh in range(X_HEADS):
            cols = slice(hh * X_HEAD_DIM, (hh + 1) * X_HEAD_DIM)
            s = _dot_nt(q[:, cols], kx_ref[0, :, cols])
            yield
            m = jnp.max(s, -1, keepdims=True)
            p = jnp.exp(s - m)
            denom = jnp.sum(p, -1, keepdims=True)
            yield
            heads.append((_dot(p.astype(BF16), vx_ref[0, :, cols]) / denom).astype(BF16))
        xa = _dot(jnp.concatenate(heads, -1), wxo_ref[...])
        yield
        out_ref[0, rows, :] = _ln(DEEPNORM_ALPHA * h1 + xa, g2_ref[...], b2_ref[...])

    _round_robin([sub_tile(k) for k in range(N_SUB)])


def _mix_xattn(oa, os_, ls, h, expand, g_dil, w_out, g1, b1, w_xq, kx, vx, w_xo, g2, b2, tm):
    b, s, d = h.shape
    m = kx.shape[1]
    tok = lambda w: pl.BlockSpec((1, tm, w), lambda bi, i: (bi, i, 0))
    res = lambda dil: pl.BlockSpec((1, tm // dil, dil * DIL_WIDTH), lambda bi, i: (bi, i, 0))
    memspec = pl.BlockSpec((1, m, d), lambda bi, i: (bi, 0, 0))
    return pl.pallas_call(
        functools.partial(_mix_kernel, tm),
        grid=(b, s // tm),
        in_specs=[tok(WIN_WIDTH)] + [res(dil) for _, dil in DIL_PAIRS] +
                 [tok(DIL_SLOTS) if dil == 1 else
                  pl.BlockSpec((1, dil, tm // dil, LANES), lambda bi, i: (bi, 0, i, 0)) for _, dil in DIL_PAIRS] +
                 [tok(d),
                  _const_spec(expand.shape), _const_spec((1, DIL_WIDTH)), _const_spec((d, d)),
                  _const_spec((1, d)), _const_spec((1, d)), _const_spec((d, d)), memspec, memspec,
                  _const_spec((d, d)), _const_spec((1, d)), _const_spec((1, d))],
        out_specs=tok(d),
        out_shape=jax.ShapeDtypeStruct((b, s, d), F32),
        scratch_shapes=[pltpu.VMEM((N_DIL - 1, DIL_WIDTH // LANES, tm, LANES), F32),
                        pltpu.VMEM((N_DIL - 1, tm, LANES), F32)],
        compiler_params=_params(("parallel", "parallel")),
        name="mix_xattn",
    )(oa, *os_, *ls, h, expand, g_dil, w_out, g1, b1, w_xq, kx, vx, w_xo, g2, b2)


FF_CHUNKS = ((0, 1024), (1024, 1024), (2048, 768))
HALO = 8


def _ffn_kernel(tm, hp_ref, hm_ref, hn_ref, wg_ref, wu_ref, cw_ref, cb_ref, wd_ref, g3_ref, b3_ref, out_ref, gs):
    i = pl.program_id(1)
    n = pl.num_programs(1)
    hm = hm_ref[0]
    hb = hm.astype(BF16)
    hp = jnp.where(i > 0, hp_ref[0], 0.0).astype(BF16)
    hn = jnp.where(i < n - 1, hn_ref[0], 0.0).astype(BF16)
    hcat = jnp.concatenate([hp, hb, hn], 0)
    acc = None
    for c0, width in FF_CHUNKS:
        cols = slice(c0, c0 + width)
        gs[:, 0:width] = _dot(hcat, wg_ref[:, cols])
        g = cb_ref[:, cols]
        for j in range(CONV_WIDTH):
            g = g + gs[HALO - 1 + j:HALO - 1 + j + tm, 0:width] * cw_ref[j:j + 1, cols]
        u = _dot(hb, wu_ref[:, cols])
        act = 0.5 * g * (1.0 + lax.erf(g * np.float32(np.sqrt(0.5)))) * u
        part = _dot(act.astype(BF16), wd_ref[cols, :])
        acc = part if acc is None else acc + part
    out_ref[0] = _ln(DEEPNORM_ALPHA * hm + acc, g3_ref[...], b3_ref[...])


def _conv_glu(h2, wg, wu, cw, cb, wd, g3, b3, tm):
    b, s, d = h2.shape
    per = tm // HALO
    nblk = s // HALO
    main = pl.BlockSpec((1, tm, d), lambda bi, i: (bi, i, 0))
    prev = pl.BlockSpec((1, HALO, d), lambda bi, i: (bi, jnp.maximum(i * per - 1, 0), 0))
    nxt = pl.BlockSpec((1, HALO, d), lambda bi, i: (bi, jnp.minimum((i + 1) * per, nblk - 1), 0))
    wmax = max(w for _, w in FF_CHUNKS)
    return pl.pallas_call(
        functools.partial(_ffn_kernel, tm),
        grid=(b, s // tm),
        in_specs=[prev, main, nxt, _const_spec((d, D_FF)), _const_spec((d, D_FF)),
                  _const_spec((CONV_WIDTH, D_FF)), _const_spec((1, D_FF)), _const_spec((D_FF, d)),
                  _const_spec((1, d)), _const_spec((1, d))],
        out_specs=main,
        out_shape=jax.ShapeDtypeStruct((b, s, d), F32),
        scratch_shapes=[pltpu.VMEM((tm + 2 * HALO, wmax), F32)],
        compiler_params=_params(("parallel", "parallel")),
        name="conv_glu",
    )(h2, h2, h2, wg, wu, cw, cb, wd, g3, b3)


def kernel(x, mem, positions, ln_in_g, ln_in_b, w_in, attn_sink, g_win, g_dil, w_mix_out, ln1_g, ln1_b, mem_ln_g, mem_ln_b, w_xq, w_xk, w_xv, w_xo, ln2_g, ln2_b, w_gate, w_up, conv_w, conv_b, w_down, ln3_g, ln3_b):
    b, s, d = x.shape
    assert DEPTH == 1
    l = 0
    row = lambda v: v.reshape(1, -1)
    cos, sin = _rope_tables(positions)

    qscale = HEAD_DIM ** -0.5 * LOG2E
    o_ka, o_va, o_qb = A_Q, A_Q + A_KV, A_Q + 2 * A_KV
    o_kb, o_vb = o_qb + B_QKV, o_qb + 2 * B_QKV
    segs = [(0, A_Q, True, 1, qscale, False),
            (o_ka, A_KV, True, 1, 1.0, True),
            (o_va, A_KV, False, 1, 1.0, True)]
    for gi, (_, dil) in enumerate(DIL_PAIRS):
        c = gi * DIL_WIDTH
        segs += [(o_qb + c, DIL_WIDTH, True, dil, qscale, False),
                 (o_kb + c, DIL_WIDTH, True, dil, 1.0, False),
                 (o_vb + c, DIL_WIDTH, False, dil, 1.0, False)]

    xscale = X_HEAD_DIM ** -0.5
    to_cast = [(w_mix_out[l], 1.0), (w_xq[l], xscale), (w_xk[l], 1.0), (w_xv[l], 1.0), (w_xo[l], 1.0),
               (w_gate[l], 1.0), (w_up[l], 1.0), (w_down[l], 1.0)]
    outs = _qkv_proj(x.reshape(b * s, d), row(ln_in_g), row(ln_in_b), w_in[l].astype(BF16), tuple(segs),
                     cos, sin, to_cast, tm=512)
    n_seg = 1 + len(segs)
    wo_b, wq_b, wk_b, wv_b, wxo_b, wg_b, wu_b, wd_b = outs[n_seg:]
    hn = outs[0].reshape(b, s, d)
    batched = lambda t: t.reshape(b, t.shape[0] // b, t.shape[1])
    qa, ka, va = (batched(t) for t in outs[1:4])
    oa = _win_attn(qa, ka, va, attn_sink[l], row(g_win[l]), tq=512)
    os_, ls = [], []
    for gi, (_, dil) in enumerate(DIL_PAIRS):
        qg, kg, vg = (batched(t) for t in outs[4 + 3 * gi:7 + 3 * gi])
        o, lse = _dil_attn(qg, kg, vg, dil, tq=max(2048 // dil, Q_BLK) if dil > 1 else 512)
        os_.append(o)
        ls.append(lse.reshape(b, s, DIL_SLOTS) if dil == 1 else lse)

    kx, vx = _mem_kv(mem, row(mem_ln_g[l]), row(mem_ln_b[l]), wk_b, wv_b)
    expand = jnp.repeat(jnp.eye(DIL_SLOTS, dtype=BF16), HEAD_DIM, axis=1)
    h2 = _mix_xattn(oa, os_, ls, hn, expand, row(g_dil[l]), wo_b, row(ln1_g[l]), row(ln1_b[l]), wq_b, kx, vx,
                    wxo_b, row(ln2_g[l]), row(ln2_b[l]), tm=512)
    return _conv_glu(h2, wg_b, wu_b, conv_w[l], row(conv_b[l]), wd_b, row(ln3_g[l]), row(ln3_b[l]), tm=1024)
```

```python
import functools

import numpy as np
import jax
import jax.numpy as jnp
from jax import lax
from jax.experimental import pallas as pl
from jax.experimental.pallas import tpu as pltpu

D_MODEL = 1024
HEAD_DIM = 64
WIN_Q_HEADS = 8
WIN_KV_HEADS = 2
WIN_HALF = 128
DIL_SLOTS = 8
DIL_PAIRS = ((128, 1), (512, 4), (2048, 16))
N_DIL = len(DIL_PAIRS)
ROT_DIM = HEAD_DIM // 4
ROT_HALF = ROT_DIM // 2
ROPE_THETA = 500000.0
X_HEADS = 4
X_HEAD_DIM = D_MODEL // X_HEADS
D_FF = 2816
CONV_WIDTH = 3
WIN_WIDTH = WIN_Q_HEADS * HEAD_DIM
DIL_WIDTH = DIL_SLOTS * HEAD_DIM
A_Q = WIN_WIDTH
A_KV = WIN_KV_HEADS * HEAD_DIM
B_QKV = N_DIL * DIL_WIDTH
DEPTH = 1
DEEPNORM_ALPHA = (2 * DEPTH) ** 0.25
LN_EPS = 1e-5
NEG_INF = -1e30
LOG2E = float(np.log2(np.e))

LANES = 128
TOK_PER_ROW = LANES // ROT_HALF
Q_BLK = 128
N_SUB = 2
DIL_SIDE = 64
VMEM_LIMIT = 56 * 1024 * 1024

F32 = jnp.float32
BF16 = jnp.bfloat16


def _ln(x, g, b):
    mu = jnp.mean(x, -1, keepdims=True)
    xc = x - mu
    var = jnp.mean(xc * xc, -1, keepdims=True)
    return xc * lax.rsqrt(var + LN_EPS) * g + b


def _rms(x, g):
    return x * lax.rsqrt(jnp.mean(x * x, -1, keepdims=True) + LN_EPS) * g


def _dot(a, b):
    return jnp.dot(a, b, preferred_element_type=F32)


def _dot_nt(a, b):
    return lax.dot_general(a, b, (((1,), (1,)), ((), ())), preferred_element_type=F32)


def _const_spec(shape):
    nd = len(shape)
    return pl.BlockSpec(shape, lambda *_: (0,) * nd, pipeline_mode=pl.Buffered(1))


def _params(sem):
    return pltpu.CompilerParams(dimension_semantics=sem, vmem_limit_bytes=VMEM_LIMIT)


def _round_robin(gens):
    live = list(gens)
    while live:
        for g in list(live):
            try:
                next(g)
            except StopIteration:
                live.remove(g)


def _rope_tab_kernel(pos_ref, invf_ref, cos_ref, sin_ref):
    ang = pos_ref[...].astype(F32) * invf_ref[...]
    cos_ref[...] = jnp.cos(ang)
    sin_ref[...] = jnp.sin(ang)


def _rope_tables(positions):
    inv_freq = ROPE_THETA ** (-jnp.arange(0, ROT_DIM, 2, dtype=F32) / ROT_DIM)
    pos_rep = jnp.repeat(positions.reshape(-1), ROT_HALF).reshape(-1, LANES)
    invf = jnp.tile(inv_freq, TOK_PER_ROW).reshape(1, LANES)
    return pl.pallas_call(
        _rope_tab_kernel,
        out_shape=(jax.ShapeDtypeStruct(pos_rep.shape, F32),) * 2,
        name="rope_tables",
    )(pos_rep, invf)


def _expand_rope(tab, tm):
    rows = jnp.broadcast_to(tab[:, None, :], (tm // TOK_PER_ROW, TOK_PER_ROW, LANES)).reshape(tm, LANES)
    tok = lax.broadcasted_iota(jnp.int32, (tm, LANES), 0) % TOK_PER_ROW
    lane = lax.broadcasted_iota(jnp.int32, (tm, LANES), 1)
    idx = tok * ROT_HALF + lane % ROT_HALF
    return jnp.take_along_axis(rows, idx, axis=1, mode="promise_in_bounds")


def _qkv_kernel(segs, tm, scales, x_ref, g_ref, b_ref, w_ref, cos_ref, sin_ref, *rest):
    n_cast = len(scales)
    cast_in, rest = rest[:n_cast], rest[n_cast:]
    h_ref, out_refs, cast_out, stage = rest[0], rest[1:1 + len(segs)], rest[1 + len(segs):-1], rest[-1]
    for scale, wi_ref, wo_ref in zip(scales, cast_in, cast_out):
        wo_ref[...] = (wi_ref[...] * scale if scale != 1.0 else wi_ref[...]).astype(BF16)
    h = _ln(x_ref[...], g_ref[...], b_ref[...])
    h_ref[...] = h
    hb = h.astype(BF16)
    slot = lax.broadcasted_iota(jnp.int32, (tm, LANES), 1) % HEAD_DIM
    cos = _expand_rope(cos_ref[...], tm)
    sin = _expand_rope(sin_ref[...], tm)
    c_tab = jnp.where(slot < ROT_DIM, cos, 1.0)
    sa_tab = jnp.where(slot < ROT_HALF, sin, 0.0)
    sb_tab = jnp.where((slot >= ROT_HALF) & (slot < ROT_DIM), sin, 0.0)
    lo = slot == lax.broadcasted_iota(jnp.int32, (tm, LANES), 1)
    for (c0, width, rope, dil, scale, dup), o_ref in zip(segs, out_refs):
        z = _dot(hb, w_ref[:, c0:c0 + width])
        for j in range(width // LANES):
            zc = z[:, j * LANES:(j + 1) * LANES]
            if rope:
                up = pltpu.roll(zc, LANES - ROT_HALF, 1)
                dn = pltpu.roll(zc, ROT_HALF, 1)
                zc = zc * c_tab - up * sa_tab + dn * sb_tab
            if scale != 1.0:
                zc = zc * scale
            if dup:
                sw = pltpu.roll(zc, HEAD_DIM, 1)
                o_ref[:, 2 * j * LANES:(2 * j + 1) * LANES] = jnp.where(lo, zc, sw).astype(BF16)
                o_ref[:, (2 * j + 1) * LANES:(2 * j + 2) * LANES] = jnp.where(lo, sw, zc).astype(BF16)
            elif dil == 1:
                o_ref[:, j * LANES:(j + 1) * LANES] = zc.astype(BF16)
            else:
                stage[...] = zc
                for c in range(dil):
                    col = c * width + j * LANES
                    o_ref[:, col:col + LANES] = stage[pl.ds(c, tm // dil, stride=dil), :].astype(BF16)


BF16_ROWS = 16


def _qkv_proj(x2, ln_g, ln_b, w_all, segs, cos, sin, to_cast, tm):
    t = x2.shape[0]
    steps = t // tm
    wtot = w_all.shape[1]
    row = lambda r, w: pl.BlockSpec((r, w), lambda i: (i, 0))
    out_shape = [jax.ShapeDtypeStruct((t, D_MODEL), F32)]
    out_specs = [row(tm, D_MODEL)]
    for _, width, _, dil, _, dup in segs:
        wout = 2 * width if dup else dil * width
        out_shape.append(jax.ShapeDtypeStruct((t // dil, wout), BF16))
        out_specs.append(row(tm // dil, wout))
    cast_specs = []
    for w, _ in to_cast:
        rows, cols = w.shape
        nblk = max(n for n in range(1, steps + 1) if steps % n == 0 and rows % (n * BF16_ROWS) == 0)
        spec = pl.BlockSpec((rows // nblk, cols), lambda i, every=steps // nblk: (i // every, 0))
        cast_specs.append(spec)
        out_shape.append(jax.ShapeDtypeStruct((rows, cols), BF16))
        out_specs.append(spec)
    return pl.pallas_call(
        functools.partial(_qkv_kernel, segs, tm, tuple(s for _, s in to_cast)),
        grid=(steps,),
        in_specs=[row(tm, D_MODEL), _const_spec((1, D_MODEL)), _const_spec((1, D_MODEL)),
                  _const_spec((D_MODEL, wtot)), row(tm // TOK_PER_ROW, LANES), row(tm // TOK_PER_ROW, LANES)]
                 + cast_specs,
        out_specs=out_specs,
        out_shape=out_shape,
        scratch_shapes=[pltpu.VMEM((tm, LANES), F32)],
        compiler_params=_params(("arbitrary",)),
        name="qkv_proj",
    )(x2, ln_g, ln_b, w_all, cos, sin, *[w for w, _ in to_cast])


def _band_bias(nq, nk, side):
    qi = lax.broadcasted_iota(jnp.int32, (nq, nk), 0)
    kj = lax.broadcasted_iota(jnp.int32, (nq, nk), 1)
    rel = kj - side - qi
    return jnp.where((rel >= -side) & (rel <= side), 0.0, NEG_INF).astype(F32)


def _fill_halo_buffer(cat, prev, main, nxt, side, tq):
    cat[0:side] = prev
    cat[side:side + tq] = main
    cat[side + tq:] = nxt


def _band_units(side, dil, tq, tile, n_tiles, sink_ref, q_ref, kcat, vcat, emit, lse_w=0):
    is_win = sink_ref is not None
    blk = Q_BLK
    nk = blk + 2 * side
    n_blk = tq // blk
    width = q_ref.shape[-1] // dil
    n_pairs = width // LANES
    lo = lax.broadcasted_iota(jnp.int32, (blk, LANES), 1) < HEAD_DIM
    lane8 = lax.broadcasted_iota(jnp.int32, (blk, max(lse_w, 1)), 1)
    band = _band_bias(blk, nk, side)
    kj = lax.broadcasted_iota(jnp.int32, (blk, nk), 1)
    band_first = jnp.where((tile == 0) & (kj < side), NEG_INF, band)
    band_last = jnp.where((tile == n_tiles - 1) & (kj >= side + blk), NEG_INF, band)
    biases = []
    for j in range(n_blk):
        bias = band
        if j == 0:
            bias = band_first
        if j == n_blk - 1:
            bias = band_last if j > 0 else jnp.minimum(band_first, band_last)
        biases.append(jnp.concatenate([bias, bias], 0))

    units = [(j, c, pr) for j in range(n_blk) for c in range(dil) for pr in range(n_pairs)]

    def kv_cols(c, pr):
        if is_win:
            return slice((pr // 2) * LANES, (pr // 2 + 1) * LANES)
        return slice(c * width + pr * LANES, c * width + (pr + 1) * LANES)

    def scores(j, c, pr):
        r0 = j * blk
        q2 = q_ref[r0:r0 + blk, c * width + pr * LANES:c * width + (pr + 1) * LANES]
        zero = jnp.zeros_like(q2)
        qs = jnp.concatenate([jnp.where(lo, q2, zero), jnp.where(lo, zero, q2)], 0)
        return _dot_nt(qs, kcat[r0:r0 + nk, kv_cols(c, pr)]) + biases[j]

    def softmax(s, pr):
        ps, rdens, lses = [], [], []
        for hf in range(2):
            sh = s[hf * blk:(hf + 1) * blk]
            m = jnp.max(sh, -1, keepdims=True)
            if is_win:
                sink = sink_ref[2 * pr + hf] * LOG2E
                m = jnp.maximum(m, sink)
            ph = jnp.exp2(sh - m)
            denom = jnp.sum(ph, -1, keepdims=True)
            if is_win:
                denom = denom + jnp.exp2(sink - m)
            else:
                lses.append(m + jnp.log2(denom))
            ps.append(ph.astype(BF16))
            rdens.append(1.0 / denom)
        return jnp.concatenate(ps, 0), rdens, lses

    pairs, lse8 = [], None

    def finish(unit, o, rdens, lses):
        nonlocal pairs, lse8
        j, c, pr = unit
        pairs.append(jnp.where(lo, o[:blk] * rdens[0], o[blk:] * rdens[1]))
        if lse_w:
            if lse8 is None:
                lse8 = jnp.zeros((blk, lse_w), F32)
            lse8 = jnp.where(lane8 == 2 * pr, lses[0], lse8)
            lse8 = jnp.where(lane8 == 2 * pr + 1, lses[1], lse8)
        if pr == n_pairs - 1:
            emit(j, c, jnp.concatenate(pairs, -1), lse8)
            pairs, lse8 = [], None

    s_next = scores(*units[0])
    pending = None
    for n, (j, c, pr) in enumerate(units):
        s = s_next
        if n + 1 < len(units):
            s_next = scores(*units[n + 1])
        p, rdens, lses = softmax(s, pr)
        o = _dot(p, vcat[j * blk:j * blk + nk, kv_cols(c, pr)])
        if pending is not None:
            finish(*pending)
        pending = ((j, c, pr), o, rdens, lses)
    finish(*pending)


def _dil_kernel(dil, tq, q_ref, kp_ref, km_ref, kn_ref, vp_ref, vm_ref, vn_ref, o_ref, lse_ref, kcat, vcat):
    _fill_halo_buffer(kcat, kp_ref[0], km_ref[0], kn_ref[0], DIL_SIDE, tq)
    _fill_halo_buffer(vcat, vp_ref[0], vm_ref[0], vn_ref[0], DIL_SIDE, tq)
    width = q_ref.shape[-1] // dil

    def emit(j, c, out, lse):
        rows = slice(j * Q_BLK, (j + 1) * Q_BLK)
        o_ref[0, rows, c * width:(c + 1) * width] = out.astype(BF16)
        lse_ref[0, c, rows, :] = lse

    _band_units(DIL_SIDE, dil, tq, pl.program_id(1), pl.num_programs(1), None, q_ref.at[0], kcat, vcat, emit,
                lse_w=lse_ref.shape[-1])


def _band_specs(tq, side, n_rows, width):
    per = tq // side
    nblk = n_rows // side
    main = pl.BlockSpec((1, tq, width), lambda bi, i: (bi, i, 0))
    prev = pl.BlockSpec((1, side, width), lambda bi, i: (bi, jnp.maximum(i * per - 1, 0), 0))
    nxt = pl.BlockSpec((1, side, width), lambda bi, i: (bi, jnp.minimum((i + 1) * per, nblk - 1), 0))
    return prev, main, nxt


def _win_kernel(tq, sink_ref, q_ref, kp_ref, km_ref, kn_ref, vp_ref, vm_ref, vn_ref, g_ref, o_ref, kcat, vcat):
    _fill_halo_buffer(kcat, kp_ref[0], km_ref[0], kn_ref[0], WIN_HALF, tq)
    _fill_halo_buffer(vcat, vp_ref[0], vm_ref[0], vn_ref[0], WIN_HALF, tq)

    def emit(j, c, out, lse):
        o_ref[0, j * Q_BLK:(j + 1) * Q_BLK, :] = _rms(out, g_ref[...]).astype(BF16)

    _band_units(WIN_HALF, 1, tq, pl.program_id(1), pl.num_programs(1), sink_ref, q_ref.at[0], kcat, vcat, emit)


def _win_attn(qa, ka, va, sink, g_win, tq):
    b, s, _ = qa.shape
    side = WIN_HALF
    kw = ka.shape[-1]
    prev, main, nxt = _band_specs(tq, side, s, kw)
    qspec = pl.BlockSpec((1, tq, WIN_WIDTH), lambda bi, i: (bi, i, 0))
    return pl.pallas_call(
        functools.partial(_win_kernel, tq),
        grid=(b, s // tq),
        in_specs=[pl.BlockSpec(memory_space=pltpu.SMEM), qspec, prev, main, nxt, prev, main, nxt,
                  _const_spec((1, WIN_WIDTH))],
        out_specs=qspec,
        out_shape=jax.ShapeDtypeStruct((b, s, WIN_WIDTH), BF16),
        scratch_shapes=[pltpu.VMEM((tq + 2 * side, kw), BF16), pltpu.VMEM((tq + 2 * side, kw), BF16)],
        compiler_params=_params(("parallel", "parallel")),
        name="win_attn",
    )(sink, qa, ka, ka, ka, va, va, va, g_win)


def _dil_attn(qv, kv, vv, dil, tq):
    b, ln, wr = qv.shape
    side = DIL_SIDE
    lse_w = DIL_SLOTS if dil == 1 else LANES
    prev, main, nxt = _band_specs(tq, side, ln, wr)
    return pl.pallas_call(
        functools.partial(_dil_kernel, dil, tq),
        grid=(b, ln // tq),
        in_specs=[main, prev, main, nxt, prev, main, nxt],
        out_specs=[main, pl.BlockSpec((1, dil, tq, lse_w), lambda bi, i: (bi, 0, i, 0))],
        out_shape=[jax.ShapeDtypeStruct((b, ln, wr), BF16),
                   jax.ShapeDtypeStruct((b, dil, ln, lse_w), F32)],
        scratch_shapes=[pltpu.VMEM((tq + 2 * side, wr), BF16), pltpu.VMEM((tq + 2 * side, wr), BF16)],
        compiler_params=_params(("parallel", "parallel")),
        name=f"dil_attn_{dil}",
    )(qv, kv, kv, kv, vv, vv, vv)


def _mem_kernel(mem_ref, g_ref, b_ref, wk_ref, wv_ref, k_ref, v_ref):
    mn = _ln(mem_ref[0], g_ref[...], b_ref[...]).astype(BF16)
    k_ref[0] = _dot(mn, wk_ref[...]).astype(BF16)
    v_ref[0] = _dot(mn, wv_ref[...]).astype(BF16)


def _mem_kv(mem, g, b, wk, wv):
    bsz, m, d = mem.shape
    blk = pl.BlockSpec((1, m, d), lambda bi: (bi, 0, 0))
    return pl.pallas_call(
        _mem_kernel,
        grid=(bsz,),
        in_specs=[blk, _const_spec((1, d)), _const_spec((1, d)), _const_spec((d, d)), _const_spec((d, d))],
        out_specs=[blk, blk],
        out_shape=[jax.ShapeDtypeStruct((bsz, m, d), BF16)] * 2,
        compiler_params=_params(("parallel",)),
        name="mem_kv",
    )(mem, g, b, wk, wv)


def _mix_kernel(tm, oa_ref, o0_ref, o1_ref, o2_ref, l0_ref, l1_ref, l2_ref, h_ref, e_ref, gd_ref, wo_ref,
                g1_ref, b1_ref, wq_ref, kx_ref, vx_ref, wxo_ref, g2_ref, b2_ref, out_ref, stage, lstage):
    ts = tm // N_SUB
    n_chunks = DIL_WIDTH // LANES

    def sub_tile(k):
        rows = slice(k * ts, (k + 1) * ts)
        ls = []
        for gi, (l_ref, (_, dil)) in enumerate(zip((l0_ref, l1_ref, l2_ref), DIL_PAIRS)):
            if dil == 1:
                ls.append(l_ref[0, rows, :])
            else:
                rrows = slice(k * ts // dil, (k + 1) * ts // dil)
                for c in range(dil):
                    lstage[gi - 1, pl.ds(k * ts + c, ts // dil, stride=dil), :] = l_ref[0, c, rrows, :]
                ls.append(lstage[gi - 1, rows, 0:DIL_SLOTS])
        mx = jnp.maximum(jnp.maximum(ls[0], ls[1]), ls[2])
        es = [jnp.exp2(l - mx) for l in ls]
        tot = es[0] + es[1] + es[2]
        ob = None
        for gi, (e, o_ref, (_, dil)) in enumerate(zip(es, (o0_ref, o1_ref, o2_ref), DIL_PAIRS)):
            wexp = _dot((e / tot).astype(BF16), e_ref[...])
            if dil == 1:
                og = o_ref[0, rows, :].astype(F32)
            else:
                rrows = slice(k * ts // dil, (k + 1) * ts // dil)
                for c in range(dil):
                    for j in range(n_chunks):
                        col = c * DIL_WIDTH + j * LANES
                        stage[gi - 1, j, pl.ds(k * ts + c, ts // dil, stride=dil), :] = (
                            o_ref[0, rrows, col:col + LANES].astype(F32))
                og = jnp.concatenate([stage[gi - 1, j, rows, :] for j in range(n_chunks)], -1)
            term = wexp * og
            ob = term if ob is None else ob + term
        obn = _rms(ob, gd_ref[...]).astype(BF16)
        yield
        mix = _dot(oa_ref[0, rows, :], wo_ref[0:WIN_WIDTH, :]) + _dot(obn, wo_ref[WIN_WIDTH:, :])
        yield
        h1 = _ln(DEEPNORM_ALPHA * h_ref[0, rows, :] + mix, g1_ref[...], b1_ref[...])
        yield
        q = _dot(h1.astype(BF16), wq_ref[...]).astype(BF16)
        yield
        heads = []
        for hh in range(X_HEADS):
            cols = slice(hh * X_HEAD_DIM, (hh + 1) * X_HEAD_DIM)
            s = _dot_nt(q[:, cols], kx_ref[0, :, cols])
            yield
            m = jnp.max(s, -1, keepdims=True)
            p = jnp.exp(s - m)
            denom = jnp.sum(p, -1, keepdims=True)
            yield
            heads.append((_dot(p.astype(BF16), vx_ref[0, :, cols]) / denom).astype(BF16))
        xa = _dot(jnp.concatenate(heads, -1), wxo_ref[...])
        yield
        out_ref[0, rows, :] = _ln(DEEPNORM_ALPHA * h1 + xa, g2_ref[...], b2_ref[...])

    _round_robin([sub_tile(k) for k in range(N_SUB)])


def _mix_xattn(oa, os_, ls, h, expand, g_dil, w_out, g1, b1, w_xq, kx, vx, w_xo, g2, b2, tm):
    b, s, d = h.shape
    m = kx.shape[1]
    tok = lambda w: pl.BlockSpec((1, tm, w), lambda bi, i: (bi, i, 0))
    res = lambda dil: pl.BlockSpec((1, tm // dil, dil * DIL_WIDTH), lambda bi, i: (bi, i, 0))
    memspec = pl.BlockSpec((1, m, d), lambda bi, i: (bi, 0, 0))
    return pl.pallas_call(
        functools.partial(_mix_kernel, tm),
        grid=(b, s // tm),
        in_specs=[tok(WIN_WIDTH)] + [res(dil) for _, dil in DIL_PAIRS] +
                 [tok(DIL_SLOTS) if dil == 1 else
                  pl.BlockSpec((1, dil, tm // dil, LANES), lambda bi, i: (bi, 0, i, 0)) for _, dil in DIL_PAIRS] +
                 [tok(d),
                  _const_spec(expand.shape), _const_spec((1, DIL_WIDTH)), _const_spec((d, d)),
                  _const_spec((1, d)), _const_spec((1, d)), _const_spec((d, d)), memspec, memspec,
                  _const_spec((d, d)), _const_spec((1, d)), _const_spec((1, d))],
        out_specs=tok(d),
        out_shape=jax.ShapeDtypeStruct((b, s, d), F32),
        scratch_shapes=[pltpu.VMEM((N_DIL - 1, DIL_WIDTH // LANES, tm, LANES), F32),
                        pltpu.VMEM((N_DIL - 1, tm, LANES), F32)],
        compiler_params=_params(("parallel", "parallel")),
        name="mix_xattn",
    )(oa, *os_, *ls, h, expand, g_dil, w_out, g1, b1, w_xq, kx, vx, w_xo, g2, b2)


FF_CHUNKS = ((0, 1024), (1024, 1024), (2048, 768))
HALO = 8


def _ffn_kernel(tm, hp_ref, hm_ref, hn_ref, wg_ref, wu_ref, cw_ref, cb_ref, wd_ref, g3_ref, b3_ref, out_ref, gs):
    i = pl.program_id(1)
    n = pl.num_programs(1)
    hm = hm_ref[0]
    hb = hm.astype(BF16)
    hp = jnp.where(i > 0, hp_ref[0], 0.0).astype(BF16)
    hn = jnp.where(i < n - 1, hn_ref[0], 0.0).astype(BF16)
    hcat = jnp.concatenate([hp, hb, hn], 0)
    acc = None
    for c0, width in FF_CHUNKS:
        cols = slice(c0, c0 + width)
        gs[:, 0:width] = _dot(hcat, wg_ref[:, cols])
        g = cb_ref[:, cols]
        for j in range(CONV_WIDTH):
            g = g + gs[HALO - 1 + j:HALO - 1 + j + tm, 0:width] * cw_ref[j:j + 1, cols]
        u = _dot(hb, wu_ref[:, cols])
        act = 0.5 * g * (1.0 + lax.erf(g * np.float32(np.sqrt(0.5)))) * u
        part = _dot(act.astype(BF16), wd_ref[cols, :])
        acc = part if acc is None else acc + part
    out_ref[0] = _ln(DEEPNORM_ALPHA * hm + acc, g3_ref[...], b3_ref[...])


def _conv_glu(h2, wg, wu, cw, cb, wd, g3, b3, tm):
    b, s, d = h2.shape
    per = tm // HALO
    nblk = s // HALO
    main = pl.BlockSpec((1, tm, d), lambda bi, i: (bi, i, 0))
    prev = pl.BlockSpec((1, HALO, d), lambda bi, i: (bi, jnp.maximum(i * per - 1, 0), 0))
    nxt = pl.BlockSpec((1, HALO, d), lambda bi, i: (bi, jnp.minimum((i + 1) * per, nblk - 1), 0))
    wmax = max(w for _, w in FF_CHUNKS)
    return pl.pallas_call(
        functools.partial(_ffn_kernel, tm),
        grid=(b, s // tm),
        in_specs=[prev, main, nxt, _const_spec((d, D_FF)), _const_spec((d, D_FF)),
                  _const_spec((CONV_WIDTH, D_FF)), _const_spec((1, D_FF)), _const_spec((D_FF, d)),
                  _const_spec((1, d)), _const_spec((1, d))],
        out_specs=main,
        out_shape=jax.ShapeDtypeStruct((b, s, d), F32),
        scratch_shapes=[pltpu.VMEM((tm + 2 * HALO, wmax), F32)],
        compiler_params=_params(("parallel", "parallel")),
        name="conv_glu",
    )(h2, h2, h2, wg, wu, cw, cb, wd, g3, b3)


def kernel(x, mem, positions, ln_in_g, ln_in_b, w_in, attn_sink, g_win, g_dil, w_mix_out, ln1_g, ln1_b, mem_ln_g, mem_ln_b, w_xq, w_xk, w_xv, w_xo, ln2_g, ln2_b, w_gate, w_up, conv_w, conv_b, w_down, ln3_g, ln3_b):
    b, s, d = x.shape
    assert DEPTH == 1
    l = 0
    row = lambda v: v.reshape(1, -1)
    cos, sin = _rope_tables(positions)

    qscale = HEAD_DIM ** -0.5 * LOG2E
    o_ka, o_va, o_qb = A_Q, A_Q + A_KV, A_Q + 2 * A_KV
    o_kb, o_vb = o_qb + B_QKV, o_qb + 2 * B_QKV
    segs = [(0, A_Q, True, 1, qscale, False),
            (o_ka, A_KV, True, 1, 1.0, True),
            (o_va, A_KV, False, 1, 1.0, True)]
    for gi, (_, dil) in enumerate(DIL_PAIRS):
        c = gi * DIL_WIDTH
        segs += [(o_qb + c, DIL_WIDTH, True, dil, qscale, False),
                 (o_kb + c, DIL_WIDTH, True, dil, 1.0, False),
                 (o_vb + c, DIL_WIDTH, False, dil, 1.0, False)]

    xscale = X_HEAD_DIM ** -0.5
    to_cast = [(w_mix_out[l], 1.0), (w_xq[l], xscale), (w_xk[l], 1.0), (w_xv[l], 1.0), (w_xo[l], 1.0),
               (w_gate[l], 1.0), (w_up[l], 1.0), (w_down[l], 1.0)]
    outs = _qkv_proj(x.reshape(b * s, d), row(ln_in_g), row(ln_in_b), w_in[l].astype(BF16), tuple(segs),
                     cos, sin, to_cast, tm=512)
    n_seg = 1 + len(segs)
    wo_b, wq_b, wk_b, wv_b, wxo_b, wg_b, wu_b, wd_b = outs[n_seg:]
    hn = outs[0].reshape(b, s, d)
    batched = lambda t: t.reshape(b, t.shape[0] // b, t.shape[1])
    qa, ka, va = (batched(t) for t in outs[1:4])
    oa = _win_attn(qa, ka, va, attn_sink[l], row(g_win[l]), tq=1024)
    os_, ls = [], []
    for gi, (_, dil) in enumerate(DIL_PAIRS):
        qg, kg, vg = (batched(t) for t in outs[4 + 3 * gi:7 + 3 * gi])
        o, lse = _dil_attn(qg, kg, vg, dil, tq=max(2048 // dil, Q_BLK))
        os_.append(o)
        ls.append(lse.reshape(b, s, DIL_SLOTS) if dil == 1 else lse)

    kx, vx = _mem_kv(mem, row(mem_ln_g[l]), row(mem_ln_b[l]), wk_b, wv_b)
    expand = jnp.repeat(jnp.eye(DIL_SLOTS, dtype=BF16), HEAD_DIM, axis=1)
    h2 = _mix_xattn(oa, os_, ls, hn, expand, row(g_dil[l]), wo_b, row(ln1_g[l]), row(ln1_b[l]), wq_b, kx, vx,
                    wxo_b, row(ln2_g[l]), row(ln2_b[l]), tm=1024)
    return _conv_glu(h2, wg_b, wu_b, conv_w[l], row(conv_b[l]), wd_b, row(ln3_g[l]), row(ln3_b[l]), tm=1024)
```

```python
import functools

import numpy as np
import jax
import jax.numpy as jnp
from jax import lax
from jax.experimental import pallas as pl
from jax.experimental.pallas import tpu as pltpu

D_MODEL = 1024
HEAD_DIM = 64
WIN_Q_HEADS = 8
WIN_KV_HEADS = 2
WIN_HALF = 128
DIL_SLOTS = 8
DIL_PAIRS = ((128, 1), (512, 4), (2048, 16))
N_DIL = len(DIL_PAIRS)
ROT_DIM = HEAD_DIM // 4
ROT_HALF = ROT_DIM // 2
ROPE_THETA = 500000.0
X_HEADS = 4
X_HEAD_DIM = D_MODEL // X_HEADS
D_FF = 2816
CONV_WIDTH = 3
WIN_WIDTH = WIN_Q_HEADS * HEAD_DIM
DIL_WIDTH = DIL_SLOTS * HEAD_DIM
A_Q = WIN_WIDTH
A_KV = WIN_KV_HEADS * HEAD_DIM
B_QKV = N_DIL * DIL_WIDTH
DEPTH = 1
DEEPNORM_ALPHA = (2 * DEPTH) ** 0.25
LN_EPS = 1e-5
NEG_INF = -1e30
LOG2E = float(np.log2(np.e))

LANES = 128
TOK_PER_ROW = LANES // ROT_HALF
Q_BLK = 128
N_SUB = 2
DIL_SIDE = 64
VMEM_LIMIT = 56 * 1024 * 1024

F32 = jnp.float32
BF16 = jnp.bfloat16


def _ln(x, g, b):
    mu = jnp.mean(x, -1, keepdims=True)
    xc = x - mu
    var = jnp.mean(xc * xc, -1, keepdims=True)
    return xc * lax.rsqrt(var + LN_EPS) * g + b


def _rms(x, g):
    return x * lax.rsqrt(jnp.mean(x * x, -1, keepdims=True) + LN_EPS) * g


def _dot(a, b):
    return jnp.dot(a, b, preferred_element_type=F32)


def _dot_nt(a, b):
    return lax.dot_general(a, b, (((1,), (1,)), ((), ())), preferred_element_type=F32)


def _const_spec(shape):
    nd = len(shape)
    return pl.BlockSpec(shape, lambda *_: (0,) * nd, pipeline_mode=pl.Buffered(1))


def _params(sem):
    return pltpu.CompilerParams(dimension_semantics=sem, vmem_limit_bytes=VMEM_LIMIT)


def _round_robin(gens):
    live = list(gens)
    while live:
        for g in list(live):
            try:
                next(g)
            except StopIteration:
                live.remove(g)


def _rope_tab_kernel(pos_ref, invf_ref, cos_ref, sin_ref):
    ang = pos_ref[...].astype(F32) * invf_ref[...]
    cos_ref[...] = jnp.cos(ang)
    sin_ref[...] = jnp.sin(ang)


def _rope_tables(positions):
    inv_freq = ROPE_THETA ** (-jnp.arange(0, ROT_DIM, 2, dtype=F32) / ROT_DIM)
    pos_rep = jnp.repeat(positions.reshape(-1), ROT_HALF).reshape(-1, LANES)
    invf = jnp.tile(inv_freq, TOK_PER_ROW).reshape(1, LANES)
    return pl.pallas_call(
        _rope_tab_kernel,
        out_shape=(jax.ShapeDtypeStruct(pos_rep.shape, F32),) * 2,
        name="rope_tables",
    )(pos_rep, invf)


def _expand_rope(tab, tm):
    rows = jnp.broadcast_to(tab[:, None, :], (tm // TOK_PER_ROW, TOK_PER_ROW, LANES)).reshape(tm, LANES)
    tok = lax.broadcasted_iota(jnp.int32, (tm, LANES), 0) % TOK_PER_ROW
    lane = lax.broadcasted_iota(jnp.int32, (tm, LANES), 1)
    idx = tok * ROT_HALF + lane % ROT_HALF
    return jnp.take_along_axis(rows, idx, axis=1, mode="promise_in_bounds")


def _qkv_kernel(segs, tm, scales, x_ref, g_ref, b_ref, w_ref, cos_ref, sin_ref, *rest):
    n_cast = len(scales)
    cast_in, rest = rest[:n_cast], rest[n_cast:]
    h_ref, out_refs, cast_out, stage = rest[0], rest[1:1 + len(segs)], rest[1 + len(segs):-1], rest[-1]
    for scale, wi_ref, wo_ref in zip(scales, cast_in, cast_out):
        wo_ref[...] = (wi_ref[...] * scale if scale != 1.0 else wi_ref[...]).astype(BF16)
    h = _ln(x_ref[...], g_ref[...], b_ref[...])
    h_ref[...] = h
    hb = h.astype(BF16)
    slot = lax.broadcasted_iota(jnp.int32, (tm, LANES), 1) % HEAD_DIM
    cos = _expand_rope(cos_ref[...], tm)
    sin = _expand_rope(sin_ref[...], tm)
    c_tab = jnp.where(slot < ROT_DIM, cos, 1.0)
    sa_tab = jnp.where(slot < ROT_HALF, sin, 0.0)
    sb_tab = jnp.where((slot >= ROT_HALF) & (slot < ROT_DIM), sin, 0.0)
    lo = slot == lax.broadcasted_iota(jnp.int32, (tm, LANES), 1)
    for (c0, width, rope, dil, scale, dup), o_ref in zip(segs, out_refs):
        z = _dot(hb, w_ref[:, c0:c0 + width])
        for j in range(width // LANES):
            zc = z[:, j * LANES:(j + 1) * LANES]
            if rope:
                up = pltpu.roll(zc, LANES - ROT_HALF, 1)
                dn = pltpu.roll(zc, ROT_HALF, 1)
                zc = zc * c_tab - up * sa_tab + dn * sb_tab
            if scale != 1.0:
                zc = zc * scale
            if dup:
                sw = pltpu.roll(zc, HEAD_DIM, 1)
                o_ref[:, 2 * j * LANES:(2 * j + 1) * LANES] = jnp.where(lo, zc, sw).astype(BF16)
                o_ref[:, (2 * j + 1) * LANES:(2 * j + 2) * LANES] = jnp.where(lo, sw, zc).astype(BF16)
            elif dil == 1:
                o_ref[:, j * LANES:(j + 1) * LANES] = zc.astype(BF16)
            else:
                stage[...] = zc
                for c in range(dil):
                    col = c * width + j * LANES
                    o_ref[:, col:col + LANES] = stage[pl.ds(c, tm // dil, stride=dil), :].astype(BF16)


BF16_ROWS = 16


def _qkv_proj(x2, ln_g, ln_b, w_all, segs, cos, sin, to_cast, tm):
    t = x2.shape[0]
    steps = t // tm
    wtot = w_all.shape[1]
    row = lambda r, w: pl.BlockSpec((r, w), lambda i: (i, 0))
    out_shape = [jax.ShapeDtypeStruct((t, D_MODEL), F32)]
    out_specs = [row(tm, D_MODEL)]
    for _, width, _, dil, _, dup in segs:
        wout = 2 * width if dup else dil * width
        out_shape.append(jax.ShapeDtypeStruct((t // dil, wout), BF16))
        out_specs.append(row(tm // dil, wout))
    cast_specs = []
    for w, _ in to_cast:
        rows, cols = w.shape
        nblk = max(n for n in range(1, steps + 1) if steps % n == 0 and rows % (n * BF16_ROWS) == 0)
        spec = pl.BlockSpec((rows // nblk, cols), lambda i, every=steps // nblk: (i // every, 0))
        cast_specs.append(spec)
        out_shape.append(jax.ShapeDtypeStruct((rows, cols), BF16))
        out_specs.append(spec)
    return pl.pallas_call(
        functools.partial(_qkv_kernel, segs, tm, tuple(s for _, s in to_cast)),
        grid=(steps,),
        in_specs=[row(tm, D_MODEL), _const_spec((1, D_MODEL)), _const_spec((1, D_MODEL)),
                  _const_spec((D_MODEL, wtot)), row(tm // TOK_PER_ROW, LANES), row(tm // TOK_PER_ROW, LANES)]
                 + cast_specs,
        out_specs=out_specs,
        out_shape=out_shape,
        scratch_shapes=[pltpu.VMEM((tm, LANES), F32)],
        compiler_params=_params(("arbitrary",)),
        name="qkv_proj",
    )(x2, ln_g, ln_b, w_all, cos, sin, *[w for w, _ in to_cast])


def _band_bias(nq, nk, side):
    qi = lax.broadcasted_iota(jnp.int32, (nq, nk), 0)
    kj = lax.broadcasted_iota(jnp.int32, (nq, nk), 1)
    rel = kj - side - qi
    return jnp.where((rel >= -side) & (rel <= side), 0.0, NEG_INF).astype(F32)


def _fill_halo_buffer(cat, prev, main, nxt, side, tq):
    cat[0:side] = prev
    cat[side:side + tq] = main
    cat[side + tq:] = nxt


def _band_units(side, dil, tq, tile, n_tiles, sink_ref, q_ref, kcat, vcat, emit, lse_w=0):
    is_win = sink_ref is not None
    blk = Q_BLK
    nk = blk + 2 * side
    n_blk = tq // blk
    width = q_ref.shape[-1] // dil
    n_pairs = width // LANES
    lo = lax.broadcasted_iota(jnp.int32, (blk, LANES), 1) < HEAD_DIM
    lane8 = lax.broadcasted_iota(jnp.int32, (blk, max(lse_w, 1)), 1)
    band = _band_bias(blk, nk, side)
    kj = lax.broadcasted_iota(jnp.int32, (blk, nk), 1)
    band_first = jnp.where((tile == 0) & (kj < side), NEG_INF, band)
    band_last = jnp.where((tile == n_tiles - 1) & (kj >= side + blk), NEG_INF, band)
    biases = []
    for j in range(n_blk):
        bias = band
        if j == 0:
            bias = band_first
        if j == n_blk - 1:
            bias = band_last if j > 0 else jnp.minimum(band_first, band_last)
        biases.append(jnp.concatenate([bias, bias], 0))

    units = [(j, c, pr) for j in range(n_blk) for c in range(dil) for pr in range(n_pairs)]

    def kv_cols(c, pr):
        if is_win:
            return slice((pr // 2) * LANES, (pr // 2 + 1) * LANES)
        return slice(c * width + pr * LANES, c * width + (pr + 1) * LANES)

    def scores(j, c, pr):
        r0 = j * blk
        q2 = q_ref[r0:r0 + blk, c * width + pr * LANES:c * width + (pr + 1) * LANES]
        zero = jnp.zeros_like(q2)
        qs = jnp.concatenate([jnp.where(lo, q2, zero), jnp.where(lo, zero, q2)], 0)
        return _dot_nt(qs, kcat[r0:r0 + nk, kv_cols(c, pr)]) + biases[j]

    def softmax(s, pr):
        ps, rdens, lses = [], [], []
        for hf in range(2):
            sh = s[hf * blk:(hf + 1) * blk]
            m = jnp.max(sh, -1, keepdims=True)
            if is_win:
                sink = sink_ref[2 * pr + hf] * LOG2E
                m = jnp.maximum(m, sink)
            ph = jnp.exp2(sh - m)
            denom = jnp.sum(ph, -1, keepdims=True)
            if is_win:
                denom = denom + jnp.exp2(sink - m)
            else:
                lses.append(m + jnp.log2(denom))
            ps.append(ph.astype(BF16))
            rdens.append(1.0 / denom)
        return jnp.concatenate(ps, 0), rdens, lses

    pairs, lse8 = [], None

    def finish(unit, o, rdens, lses):
        nonlocal pairs, lse8
        j, c, pr = unit
        pairs.append(jnp.where(lo, o[:blk] * rdens[0], o[blk:] * rdens[1]))
        if lse_w:
            if lse8 is None:
                lse8 = jnp.zeros((blk, lse_w), F32)
            lse8 = jnp.where(lane8 == 2 * pr, lses[0], lse8)
            lse8 = jnp.where(lane8 == 2 * pr + 1, lses[1], lse8)
        if pr == n_pairs - 1:
            emit(j, c, jnp.concatenate(pairs, -1), lse8)
            pairs, lse8 = [], None

    s_next = scores(*units[0])
    pending = None
    for n, (j, c, pr) in enumerate(units):
        s = s_next
        if n + 1 < len(units):
            s_next = scores(*units[n + 1])
        p, rdens, lses = softmax(s, pr)
        o = _dot(p, vcat[j * blk:j * blk + nk, kv_cols(c, pr)])
        if pending is not None:
            finish(*pending)
        pending = ((j, c, pr), o, rdens, lses)
    finish(*pending)


def _dil_kernel(dil, tq, q_ref, kp_ref, km_ref, kn_ref, vp_ref, vm_ref, vn_ref, o_ref, lse_ref, kcat, vcat):
    _fill_halo_buffer(kcat, kp_ref[0], km_ref[0], kn_ref[0], DIL_SIDE, tq)
    _fill_halo_buffer(vcat, vp_ref[0], vm_ref[0], vn_ref[0], DIL_SIDE, tq)
    width = q_ref.shape[-1] // dil

    def emit(j, c, out, lse):
        rows = slice(j * Q_BLK, (j + 1) * Q_BLK)
        o_ref[0, rows, c * width:(c + 1) * width] = out.astype(BF16)
        lse_ref[0, c, rows, :] = lse

    _band_units(DIL_SIDE, dil, tq, pl.program_id(1), pl.num_programs(1), None, q_ref.at[0], kcat, vcat, emit,
                lse_w=lse_ref.shape[-1])


def _band_specs(tq, side, n_rows, width):
    per = tq // side
    nblk = n_rows // side
    main = pl.BlockSpec((1, tq, width), lambda bi, i: (bi, i, 0))
    prev = pl.BlockSpec((1, side, width), lambda bi, i: (bi, jnp.maximum(i * per - 1, 0), 0))
    nxt = pl.BlockSpec((1, side, width), lambda bi, i: (bi, jnp.minimum((i + 1) * per, nblk - 1), 0))
    return prev, main, nxt


def _win_kernel(tq, sink_ref, q_ref, kp_ref, km_ref, kn_ref, vp_ref, vm_ref, vn_ref, g_ref, o_ref, kcat, vcat):
    _fill_halo_buffer(kcat, kp_ref[0], km_ref[0], kn_ref[0], WIN_HALF, tq)
    _fill_halo_buffer(vcat, vp_ref[0], vm_ref[0], vn_ref[0], WIN_HALF, tq)

    def emit(j, c, out, lse):
        o_ref[0, j * Q_BLK:(j + 1) * Q_BLK, :] = _rms(out, g_ref[...]).astype(BF16)

    _band_units(WIN_HALF, 1, tq, pl.program_id(1), pl.num_programs(1), sink_ref, q_ref.at[0], kcat, vcat, emit)


def _win_attn(qa, ka, va, sink, g_win, tq):
    b, s, _ = qa.shape
    side = WIN_HALF
    kw = ka.shape[-1]
    prev, main, nxt = _band_specs(tq, side, s, kw)
    qspec = pl.BlockSpec((1, tq, WIN_WIDTH), lambda bi, i: (bi, i, 0))
    return pl.pallas_call(
        functools.partial(_win_kernel, tq),
        grid=(b, s // tq),
        in_specs=[pl.BlockSpec(memory_space=pltpu.SMEM), qspec, prev, main, nxt, prev, main, nxt,
                  _const_spec((1, WIN_WIDTH))],
        out_specs=qspec,
        out_shape=jax.ShapeDtypeStruct((b, s, WIN_WIDTH), BF16),
        scratch_shapes=[pltpu.VMEM((tq + 2 * side, kw), BF16), pltpu.VMEM((tq + 2 * side, kw), BF16)],
        compiler_params=_params(("parallel", "parallel")),
        name="win_attn",
    )(sink, qa, ka, ka, ka, va, va, va, g_win)


def _dil_attn(qv, kv, vv, dil, tq):
    b, ln, wr = qv.shape
    side = DIL_SIDE
    lse_w = DIL_SLOTS if dil == 1 else LANES
    prev, main, nxt = _band_specs(tq, side, ln, wr)
    return pl.pallas_call(
        functools.partial(_dil_kernel, dil, tq),
        grid=(b, ln // tq),
        in_specs=[main, prev, main, nxt, prev, main, nxt],
        out_specs=[main, pl.BlockSpec((1, dil, tq, lse_w), lambda bi, i: (bi, 0, i, 0))],
        out_shape=[jax.ShapeDtypeStruct((b, ln, wr), BF16),
                   jax.ShapeDtypeStruct((b, dil, ln, lse_w), F32)],
        scratch_shapes=[pltpu.VMEM((tq + 2 * side, wr), BF16), pltpu.VMEM((tq + 2 * side, wr), BF16)],
        compiler_params=_params(("parallel", "parallel")),
        name=f"dil_attn_{dil}",
    )(qv, kv, kv, kv, vv, vv, vv)


def _mem_kernel(mem_ref, g_ref, b_ref, wk_ref, wv_ref, k_ref, v_ref):
    mn = _ln(mem_ref[0], g_ref[...], b_ref[...]).astype(BF16)
    k_ref[0] = _dot(mn, wk_ref[...]).astype(BF16)
    v_ref[0] = _dot(mn, wv_ref[...]).astype(BF16)


def _mem_kv(mem, g, b, wk, wv):
    bsz, m, d = mem.shape
    blk = pl.BlockSpec((1, m, d), lambda bi: (bi, 0, 0))
    return pl.pallas_call(
        _mem_kernel,
        grid=(bsz,),
        in_specs=[blk, _const_spec((1, d)), _const_spec((1, d)), _const_spec((d, d)), _const_spec((d, d))],
        out_specs=[blk, blk],
        out_shape=[jax.ShapeDtypeStruct((bsz, m, d), BF16)] * 2,
        compiler_params=_params(("parallel",)),
        name="mem_kv",
    )(mem, g, b, wk, wv)


def _mix_kernel(tm, oa_ref, o0_ref, o1_ref, o2_ref, l0_ref, l1_ref, l2_ref, h_ref, e_ref, gd_ref, wo_ref,
                g1_ref, b1_ref, wq_ref, kx_ref, vx_ref, wxo_ref, g2_ref, b2_ref, out_ref, stage, lstage):
    ts = tm // N_SUB
    n_chunks = DIL_WIDTH // LANES

    def sub_tile(k):
        rows = slice(k * ts, (k + 1) * ts)
        ls = []
        for gi, (l_ref, (_, dil)) in enumerate(zip((l0_ref, l1_ref, l2_ref), DIL_PAIRS)):
            if dil == 1:
                ls.append(l_ref[0, rows, :])
            else:
                rrows = slice(k * ts // dil, (k + 1) * ts // dil)
                for c in range(dil):
                    lstage[gi - 1, pl.ds(k * ts + c, ts // dil, stride=dil), :] = l_ref[0, c, rrows, :]
                ls.append(lstage[gi - 1, rows, 0:DIL_SLOTS])
        mx = jnp.maximum(jnp.maximum(ls[0], ls[1]), ls[2])
        es = [jnp.exp2(l - mx) for l in ls]
        tot = es[0] + es[1] + es[2]
        ob = None
        for gi, (e, o_ref, (_, dil)) in enumerate(zip(es, (o0_ref, o1_ref, o2_ref), DIL_PAIRS)):
            wexp = _dot((e / tot).astype(BF16), e_ref[...])
            if dil == 1:
                og = o_ref[0, rows, :].astype(F32)
            else:
                rrows = slice(k * ts // dil, (k + 1) * ts // dil)
                for c in range(dil):
                    for j in range(n_chunks):
                        col = c * DIL_WIDTH + j * LANES
                        stage[gi - 1, j, pl.ds(k * ts + c, ts // dil, stride=dil), :] = (
                            o_ref[0, rrows, col:col + LANES].astype(F32))
                og = jnp.concatenate([stage[gi - 1, j, rows, :] for j in range(n_chunks)], -1)
            term = wexp * og
            ob = term if ob is None else ob + term
        obn = _rms(ob, gd_ref[...]).astype(BF16)
        yield
        mix = _dot(oa_ref[0, rows, :], wo_ref[0:WIN_WIDTH, :]) + _dot(obn, wo_ref[WIN_WIDTH:, :])
        yield
        h1 = _ln(DEEPNORM_ALPHA * h_ref[0, rows, :] + mix, g1_ref[...], b1_ref[...])
        yield
        q = _dot(h1.astype(BF16), wq_ref[...]).astype(BF16)
        yield
        heads = []
        for hh in range(X_HEADS):
            cols = slice(hh * X_HEAD_DIM, (hh + 1) * X_HEAD_DIM)
            s = _dot_nt(q[:, cols], kx_ref[0, :, cols])
            yield
            m = jnp.max(s, -1, keepdims=True)
            p = jnp.exp(s - m)
            denom = jnp.sum(p, -1, keepdims=True)
            yield
            heads.append((_dot(p.astype(BF16), vx_ref[0, :, cols]) / denom).astype(BF16))
        xa = _dot(jnp.concatenate(heads, -1), wxo_ref[...])
        yield
        out_ref[0, rows, :] = _ln(DEEPNORM_ALPHA * h1 + xa, g2_ref[...], b2_ref[...])

    _round_robin([sub_tile(k) for k in range(N_SUB)])


def _mix_xattn(oa, os_, ls, h, expand, g_dil, w_out, g1, b1, w_xq, kx, vx, w_xo, g2, b2, tm):
    b, s, d = h.shape
    m = kx.shape[1]
    tok = lambda w: pl.BlockSpec((1, tm, w), lambda bi, i: (bi, i, 0))
    res = lambda dil: pl.BlockSpec((1, tm // dil, dil * DIL_WIDTH), lambda bi, i: (bi, i, 0))
    memspec = pl.BlockSpec((1, m, d), lambda bi, i: (bi, 0, 0))
    return pl.pallas_call(
        functools.partial(_mix_kernel, tm),
        grid=(b, s // tm),
        in_specs=[tok(WIN_WIDTH)] + [res(dil) for _, dil in DIL_PAIRS] +
                 [tok(DIL_SLOTS) if dil == 1 else
                  pl.BlockSpec((1, dil, tm // dil, LANES), lambda bi, i: (bi, 0, i, 0)) for _, dil in DIL_PAIRS] +
                 [tok(d),
                  _const_spec(expand.shape), _const_spec((1, DIL_WIDTH)), _const_spec((d, d)),
                  _const_spec((1, d)), _const_spec((1, d)), _const_spec((d, d)), memspec, memspec,
                  _const_spec((d, d)), _const_spec((1, d)), _const_spec((1, d))],
        out_specs=tok(d),
        out_shape=jax.ShapeDtypeStruct((b, s, d), F32),
        scratch_shapes=[pltpu.VMEM((N_DIL - 1, DIL_WIDTH // LANES, tm, LANES), F32),
                        pltpu.VMEM((N_DIL - 1, tm, LANES), F32)],
        compiler_params=_params(("parallel", "parallel")),
        name="mix_xattn",
    )(oa, *os_, *ls, h, expand, g_dil, w_out, g1, b1, w_xq, kx, vx, w_xo, g2, b2)


FF_CHUNKS = ((0, 1024), (1024, 1024), (2048, 768))
HALO = 16
CONV_PHASES = 4


def _ffn_kernel(tm, hp_ref, hm_ref, hn_ref, wg_ref, wu_ref, cw_ref, cb_ref, wd_ref, g3_ref, b3_ref, out_ref,
                gs, us, ys):
    i = pl.program_id(1)
    n = pl.num_programs(1)
    hm = hm_ref[0]
    hb = hm.astype(BF16)
    hp = jnp.where(i > 0, hp_ref[0], 0.0).astype(BF16)
    hn = jnp.where(i < n - 1, hn_ref[0], 0.0).astype(BF16)
    hcat = jnp.concatenate([hp, hb, hn], 0)
    q = tm // CONV_PHASES
    acc = None
    for c0, width in FF_CHUNKS:
        gate = _dot(hcat, wg_ref[:, c0:c0 + width])
        up = _dot(hb, wu_ref[:, c0:c0 + width])
        n_slabs = width // LANES
        for sl in range(n_slabs):
            gs[sl] = gate[:, sl * LANES:(sl + 1) * LANES]
            us[sl] = up[:, sl * LANES:(sl + 1) * LANES]
        phases = []
        for ph in range(CONV_PHASES):
            slabs = []
            for sl in range(n_slabs):
                lanes = slice(c0 + sl * LANES, c0 + (sl + 1) * LANES)
                g = cb_ref[:, lanes]
                for j in range(CONV_WIDTH):
                    g = g + gs[sl, pl.ds(HALO - 1 + j + ph, q, stride=CONV_PHASES), :] * cw_ref[j:j + 1, lanes]
                u = us[sl, pl.ds(ph, q, stride=CONV_PHASES), :]
                slabs.append((0.5 * g * (1.0 + lax.erf(g * np.float32(np.sqrt(0.5)))) * u).astype(BF16))
            phases.append(jnp.concatenate(slabs, -1))
        part = _dot(jnp.concatenate(phases, 0), wd_ref[c0:c0 + width, :])
        acc = part if acc is None else acc + part
    for ph in range(CONV_PHASES):
        for sl in range(D_MODEL // LANES):
            ys[sl, pl.ds(ph, q, stride=CONV_PHASES), :] = acc[ph * q:(ph + 1) * q, sl * LANES:(sl + 1) * LANES]
    ff = jnp.concatenate([ys[sl] for sl in range(D_MODEL // LANES)], -1)
    out_ref[0] = _ln(DEEPNORM_ALPHA * hm + ff, g3_ref[...], b3_ref[...])


def _conv_glu(h2, wg, wu, cw, cb, wd, g3, b3, tm):
    b, s, d = h2.shape
    per = tm // HALO
    nblk = s // HALO
    main = pl.BlockSpec((1, tm, d), lambda bi, i: (bi, i, 0))
    prev = pl.BlockSpec((1, HALO, d), lambda bi, i: (bi, jnp.maximum(i * per - 1, 0), 0))
    nxt = pl.BlockSpec((1, HALO, d), lambda bi, i: (bi, jnp.minimum((i + 1) * per, nblk - 1), 0))
    n_slabs = max(w for _, w in FF_CHUNKS) // LANES
    return pl.pallas_call(
        functools.partial(_ffn_kernel, tm),
        grid=(b, s // tm),
        in_specs=[prev, main, nxt, _const_spec((d, D_FF)), _const_spec((d, D_FF)),
                  _const_spec((CONV_WIDTH, D_FF)), _const_spec((1, D_FF)), _const_spec((D_FF, d)),
                  _const_spec((1, d)), _const_spec((1, d))],
        out_specs=main,
        out_shape=jax.ShapeDtypeStruct((b, s, d), F32),
        scratch_shapes=[pltpu.VMEM((n_slabs, tm + 2 * HALO, LANES), F32), pltpu.VMEM((n_slabs, tm, LANES), F32),
                        pltpu.VMEM((d // LANES, tm, LANES), F32)],
        compiler_params=_params(("parallel", "parallel")),
        name="conv_glu",
    )(h2, h2, h2, wg, wu, cw, cb, wd, g3, b3)


def kernel(x, mem, positions, ln_in_g, ln_in_b, w_in, attn_sink, g_win, g_dil, w_mix_out, ln1_g, ln1_b, mem_ln_g, mem_ln_b, w_xq, w_xk, w_xv, w_xo, ln2_g, ln2_b, w_gate, w_up, conv_w, conv_b, w_down, ln3_g, ln3_b):
    b, s, d = x.shape
    assert DEPTH == 1
    l = 0
    row = lambda v: v.reshape(1, -1)
    cos, sin = _rope_tables(positions)

    qscale = HEAD_DIM ** -0.5 * LOG2E
    o_ka, o_va, o_qb = A_Q, A_Q + A_KV, A_Q + 2 * A_KV
    o_kb, o_vb = o_qb + B_QKV, o_qb + 2 * B_QKV
    segs = [(0, A_Q, True, 1, qscale, False),
            (o_ka, A_KV, True, 1, 1.0, True),
            (o_va, A_KV, False, 1, 1.0, True)]
    for gi, (_, dil) in enumerate(DIL_PAIRS):
        c = gi * DIL_WIDTH
        segs += [(o_qb + c, DIL_WIDTH, True, dil, qscale, False),
                 (o_kb + c, DIL_WIDTH, True, dil, 1.0, False),
                 (o_vb + c, DIL_WIDTH, False, dil, 1.0, False)]

    xscale = X_HEAD_DIM ** -0.5
    to_cast = [(w_mix_out[l], 1.0), (w_xq[l], xscale), (w_xk[l], 1.0), (w_xv[l], 1.0), (w_xo[l], 1.0),
               (w_gate[l], 1.0), (w_up[l], 1.0), (w_down[l], 1.0)]
    outs = _qkv_proj(x.reshape(b * s, d), row(ln_in_g), row(ln_in_b), w_in[l].astype(BF16), tuple(segs),
                     cos, sin, to_cast, tm=512)
    n_seg = 1 + len(segs)
    wo_b, wq_b, wk_b, wv_b, wxo_b, wg_b, wu_b, wd_b = outs[n_seg:]
    hn = outs[0].reshape(b, s, d)
    batched = lambda t: t.reshape(b, t.shape[0] // b, t.shape[1])
    qa, ka, va = (batched(t) for t in outs[1:4])
    oa = _win_attn(qa, ka, va, attn_sink[l], row(g_win[l]), tq=1024)
    os_, ls = [], []
    for gi, (_, dil) in enumerate(DIL_PAIRS):
        qg, kg, vg = (batched(t) for t in outs[4 + 3 * gi:7 + 3 * gi])
        o, lse = _dil_attn(qg, kg, vg, dil, tq=max(2048 // dil, Q_BLK))
        os_.append(o)
        ls.append(lse.reshape(b, s, DIL_SLOTS) if dil == 1 else lse)

    kx, vx = _mem_kv(mem, row(mem_ln_g[l]), row(mem_ln_b[l]), wk_b, wv_b)
    expand = jnp.repeat(jnp.eye(DIL_SLOTS, dtype=BF16), HEAD_DIM, axis=1)
    h2 = _mix_xattn(oa, os_, ls, hn, expand, row(g_dil[l]), wo_b, row(ln1_g[l]), row(ln1_b[l]), wq_b, kx, vx,
                    wxo_b, row(ln2_g[l]), row(ln2_b[l]), tm=1024)
    return _conv_glu(h2, wg_b, wu_b, conv_w[l], row(conv_b[l]), wd_b, row(ln3_g[l]), row(ln3_b[l]), tm=512)
```

```python
import functools

import numpy as np
import jax
import jax.numpy as jnp
from jax import lax
from jax.experimental import pallas as pl
from jax.experimental.pallas import tpu as pltpu

D_MODEL = 1024
HEAD_DIM = 64
WIN_Q_HEADS = 8
WIN_KV_HEADS = 2
WIN_HALF = 128
DIL_SLOTS = 8
DIL_PAIRS = ((128, 1), (512, 4), (2048, 16))
N_DIL = len(DIL_PAIRS)
ROT_DIM = HEAD_DIM // 4
ROT_HALF = ROT_DIM // 2
ROPE_THETA = 500000.0
X_HEADS = 4
X_HEAD_DIM = D_MODEL // X_HEADS
D_FF = 2816
CONV_WIDTH = 3
WIN_WIDTH = WIN_Q_HEADS * HEAD_DIM
DIL_WIDTH = DIL_SLOTS * HEAD_DIM
A_Q = WIN_WIDTH
A_KV = WIN_KV_HEADS * HEAD_DIM
B_QKV = N_DIL * DIL_WIDTH
DEPTH = 1
DEEPNORM_ALPHA = (2 * DEPTH) ** 0.25
LN_EPS = 1e-5
NEG_INF = -1e30
LOG2E = float(np.log2(np.e))

LANES = 128
TOK_PER_ROW = LANES // ROT_HALF
Q_BLK = 128
N_SUB = 2
DIL_SIDE = 64
VMEM_LIMIT = 56 * 1024 * 1024

F32 = jnp.float32
BF16 = jnp.bfloat16


def _ln(x, g, b):
    mu = jnp.mean(x, -1, keepdims=True)
    xc = x - mu
    var = jnp.mean(xc * xc, -1, keepdims=True)
    return xc * lax.rsqrt(var + LN_EPS) * g + b


def _rms(x, g):
    return x * lax.rsqrt(jnp.mean(x * x, -1, keepdims=True) + LN_EPS) * g


def _dot(a, b):
    return jnp.dot(a, b, preferred_element_type=F32)


def _dot_nt(a, b):
    return lax.dot_general(a, b, (((1,), (1,)), ((), ())), preferred_element_type=F32)


def _const_spec(shape):
    nd = len(shape)
    return pl.BlockSpec(shape, lambda *_: (0,) * nd, pipeline_mode=pl.Buffered(1))


def _params(sem):
    return pltpu.CompilerParams(dimension_semantics=sem, vmem_limit_bytes=VMEM_LIMIT)


def _round_robin(gens):
    live = list(gens)
    while live:
        for g in list(live):
            try:
                next(g)
            except StopIteration:
                live.remove(g)


def _rope_tab_kernel(pos_ref, invf_ref, cos_ref, sin_ref):
    ang = pos_ref[...].astype(F32) * invf_ref[...]
    cos_ref[...] = jnp.cos(ang)
    sin_ref[...] = jnp.sin(ang)


def _rope_tables(positions):
    inv_freq = ROPE_THETA ** (-jnp.arange(0, ROT_DIM, 2, dtype=F32) / ROT_DIM)
    pos_rep = jnp.repeat(positions.reshape(-1), ROT_HALF).reshape(-1, LANES)
    invf = jnp.tile(inv_freq, TOK_PER_ROW).reshape(1, LANES)
    return pl.pallas_call(
        _rope_tab_kernel,
        out_shape=(jax.ShapeDtypeStruct(pos_rep.shape, F32),) * 2,
        name="rope_tables",
    )(pos_rep, invf)


def _expand_rope(tab, tm):
    rows = jnp.broadcast_to(tab[:, None, :], (tm // TOK_PER_ROW, TOK_PER_ROW, LANES)).reshape(tm, LANES)
    tok = lax.broadcasted_iota(jnp.int32, (tm, LANES), 0) % TOK_PER_ROW
    lane = lax.broadcasted_iota(jnp.int32, (tm, LANES), 1)
    idx = tok * ROT_HALF + lane % ROT_HALF
    return jnp.take_along_axis(rows, idx, axis=1, mode="promise_in_bounds")


def _qkv_kernel(segs, tm, scales, x_ref, g_ref, b_ref, w_ref, cos_ref, sin_ref, *rest):
    n_cast = len(scales)
    cast_in, rest = rest[:n_cast], rest[n_cast:]
    h_ref, out_refs, cast_out, stage = rest[0], rest[1:1 + len(segs)], rest[1 + len(segs):-1], rest[-1]
    h = _ln(x_ref[...], g_ref[...], b_ref[...])
    h_ref[...] = h
    hb = h.astype(BF16)
    for scale, wi_ref, wo_ref in zip(scales, cast_in, cast_out):
        wo_ref[...] = (wi_ref[...] * scale if scale != 1.0 else wi_ref[...]).astype(BF16)
    slot = lax.broadcasted_iota(jnp.int32, (tm, LANES), 1) % HEAD_DIM
    cos = _expand_rope(cos_ref[...], tm)
    sin = _expand_rope(sin_ref[...], tm)
    c_tab = jnp.where(slot < ROT_DIM, cos, 1.0)
    sa_tab = jnp.where(slot < ROT_HALF, sin, 0.0)
    sb_tab = jnp.where((slot >= ROT_HALF) & (slot < ROT_DIM), sin, 0.0)
    lo = slot == lax.broadcasted_iota(jnp.int32, (tm, LANES), 1)
    for (c0, width, rope, dil, scale, dup), o_ref in zip(segs, out_refs):
        z = _dot(hb, w_ref[:, c0:c0 + width])
        for j in range(width // LANES):
            zc = z[:, j * LANES:(j + 1) * LANES]
            if rope:
                up = pltpu.roll(zc, LANES - ROT_HALF, 1)
                dn = pltpu.roll(zc, ROT_HALF, 1)
                zc = zc * c_tab - up * sa_tab + dn * sb_tab
            if scale != 1.0:
                zc = zc * scale
            if dup:
                sw = pltpu.roll(zc, HEAD_DIM, 1)
                o_ref[:, 2 * j * LANES:(2 * j + 1) * LANES] = jnp.where(lo, zc, sw).astype(BF16)
                o_ref[:, (2 * j + 1) * LANES:(2 * j + 2) * LANES] = jnp.where(lo, sw, zc).astype(BF16)
            elif dil == 1:
                o_ref[:, j * LANES:(j + 1) * LANES] = zc.astype(BF16)
            else:
                stage[...] = zc
                for c in range(dil):
                    col = c * width + j * LANES
                    o_ref[:, col:col + LANES] = stage[pl.ds(c, tm // dil, stride=dil), :].astype(BF16)


BF16_ROWS = 16


def _qkv_proj(x2, ln_g, ln_b, w_all, segs, cos, sin, to_cast, tm):
    t = x2.shape[0]
    steps = t // tm
    wtot = w_all.shape[1]
    row = lambda r, w: pl.BlockSpec((r, w), lambda i: (i, 0))
    out_shape = [jax.ShapeDtypeStruct((t, D_MODEL), F32)]
    out_specs = [row(tm, D_MODEL)]
    for _, width, _, dil, _, dup in segs:
        wout = 2 * width if dup else dil * width
        out_shape.append(jax.ShapeDtypeStruct((t // dil, wout), BF16))
        out_specs.append(row(tm // dil, wout))
    cast_specs = []
    for w, _ in to_cast:
        rows, cols = w.shape
        nblk = max(n for n in range(1, steps + 1) if steps % n == 0 and rows % (n * BF16_ROWS) == 0)
        spec = pl.BlockSpec((rows // nblk, cols), lambda i, every=steps // nblk: (i // every, 0))
        cast_specs.append(spec)
        out_shape.append(jax.ShapeDtypeStruct((rows, cols), BF16))
        out_specs.append(spec)
    return pl.pallas_call(
        functools.partial(_qkv_kernel, segs, tm, tuple(s for _, s in to_cast)),
        grid=(steps,),
        in_specs=[row(tm, D_MODEL), _const_spec((1, D_MODEL)), _const_spec((1, D_MODEL)),
                  _const_spec((D_MODEL, wtot)), row(tm // TOK_PER_ROW, LANES), row(tm // TOK_PER_ROW, LANES)]
                 + cast_specs,
        out_specs=out_specs,
        out_shape=out_shape,
        scratch_shapes=[pltpu.VMEM((tm, LANES), F32)],
        compiler_params=_params(("arbitrary",)),
        name="qkv_proj",
    )(x2, ln_g, ln_b, w_all, cos, sin, *[w for w, _ in to_cast])


def _band_bias(nq, nk, side):
    qi = lax.broadcasted_iota(jnp.int32, (nq, nk), 0)
    kj = lax.broadcasted_iota(jnp.int32, (nq, nk), 1)
    rel = kj - side - qi
    return jnp.where((rel >= -side) & (rel <= side), 0.0, NEG_INF).astype(F32)


def _fill_halo_buffer(cat, prev, main, nxt, side, tq):
    cat[0:side] = prev
    cat[side:side + tq] = main
    cat[side + tq:] = nxt


def _band_units(side, dil, tq, tile, n_tiles, sink_ref, q_ref, kcat, vcat, emit, lse_w=0):
    is_win = sink_ref is not None
    blk = Q_BLK
    nk = blk + 2 * side
    n_blk = tq // blk
    width = q_ref.shape[-1] // dil
    n_pairs = width // LANES
    lo = lax.broadcasted_iota(jnp.int32, (blk, LANES), 1) < HEAD_DIM
    lane8 = lax.broadcasted_iota(jnp.int32, (blk, max(lse_w, 1)), 1)
    band = _band_bias(blk, nk, side)
    kj = lax.broadcasted_iota(jnp.int32, (blk, nk), 1)
    band_first = jnp.where((tile == 0) & (kj < side), NEG_INF, band)
    band_last = jnp.where((tile == n_tiles - 1) & (kj >= side + blk), NEG_INF, band)
    biases = []
    for j in range(n_blk):
        bias = band
        if j == 0:
            bias = band_first
        if j == n_blk - 1:
            bias = band_last if j > 0 else jnp.minimum(band_first, band_last)
        biases.append(jnp.concatenate([bias, bias], 0))

    units = [(j, c, pr) for j in range(n_blk) for c in range(dil) for pr in range(n_pairs)]

    def kv_cols(c, pr):
        if is_win:
            return slice((pr // 2) * LANES, (pr // 2 + 1) * LANES)
        return slice(c * width + pr * LANES, c * width + (pr + 1) * LANES)

    def scores(j, c, pr):
        r0 = j * blk
        q2 = q_ref[r0:r0 + blk, c * width + pr * LANES:c * width + (pr + 1) * LANES]
        zero = jnp.zeros_like(q2)
        qs = jnp.concatenate([jnp.where(lo, q2, zero), jnp.where(lo, zero, q2)], 0)
        return _dot_nt(qs, kcat[r0:r0 + nk, kv_cols(c, pr)]) + biases[j]

    def softmax(s, pr):
        ps, rdens, lses = [], [], []
        for hf in range(2):
            sh = s[hf * blk:(hf + 1) * blk]
            m = jnp.max(sh, -1, keepdims=True)
            if is_win:
                sink = sink_ref[2 * pr + hf] * LOG2E
                m = jnp.maximum(m, sink)
            ph = jnp.exp2(sh - m)
            denom = jnp.sum(ph, -1, keepdims=True)
            if is_win:
                denom = denom + jnp.exp2(sink - m)
            else:
                lses.append(m + jnp.log2(denom))
            ps.append(ph.astype(BF16))
            rdens.append(1.0 / denom)
        return jnp.concatenate(ps, 0), rdens, lses

    pairs, lse8 = [], None

    def finish(unit, o, rdens, lses):
        nonlocal pairs, lse8
        j, c, pr = unit
        pairs.append(jnp.where(lo, o[:blk] * rdens[0], o[blk:] * rdens[1]))
        if lse_w:
            if lse8 is None:
                lse8 = jnp.zeros((blk, lse_w), F32)
            lse8 = jnp.where(lane8 == 2 * pr, lses[0], lse8)
            lse8 = jnp.where(lane8 == 2 * pr + 1, lses[1], lse8)
        if pr == n_pairs - 1:
            emit(j, c, jnp.concatenate(pairs, -1), lse8)
            pairs, lse8 = [], None

    s_next = scores(*units[0])
    pending = None
    for n, (j, c, pr) in enumerate(units):
        s = s_next
        if n + 1 < len(units):
            s_next = scores(*units[n + 1])
        p, rdens, lses = softmax(s, pr)
        o = _dot(p, vcat[j * blk:j * blk + nk, kv_cols(c, pr)])
        if pending is not None:
            finish(*pending)
        pending = ((j, c, pr), o, rdens, lses)
    finish(*pending)


def _dil_kernel(dil, tq, q_ref, kp_ref, km_ref, kn_ref, vp_ref, vm_ref, vn_ref, o_ref, lse_ref, kcat, vcat):
    _fill_halo_buffer(kcat, kp_ref[0], km_ref[0], kn_ref[0], DIL_SIDE, tq)
    _fill_halo_buffer(vcat, vp_ref[0], vm_ref[0], vn_ref[0], DIL_SIDE, tq)
    width = q_ref.shape[-1] // dil

    def emit(j, c, out, lse):
        rows = slice(j * Q_BLK, (j + 1) * Q_BLK)
        o_ref[0, rows, c * width:(c + 1) * width] = out.astype(BF16)
        lse_ref[0, c, rows, :] = lse

    _band_units(DIL_SIDE, dil, tq, pl.program_id(1), pl.num_programs(1), None, q_ref.at[0], kcat, vcat, emit,
                lse_w=lse_ref.shape[-1])


def _band_specs(tq, side, n_rows, width):
    per = tq // side
    nblk = n_rows // side
    main = pl.BlockSpec((1, tq, width), lambda bi, i: (bi, i, 0))
    prev = pl.BlockSpec((1, side, width), lambda bi, i: (bi, jnp.maximum(i * per - 1, 0), 0))
    nxt = pl.BlockSpec((1, side, width), lambda bi, i: (bi, jnp.minimum((i + 1) * per, nblk - 1), 0))
    return prev, main, nxt


def _win_kernel(tq, sink_ref, q_ref, kp_ref, km_ref, kn_ref, vp_ref, vm_ref, vn_ref, g_ref, o_ref, kcat, vcat):
    _fill_halo_buffer(kcat, kp_ref[0], km_ref[0], kn_ref[0], WIN_HALF, tq)
    _fill_halo_buffer(vcat, vp_ref[0], vm_ref[0], vn_ref[0], WIN_HALF, tq)

    def emit(j, c, out, lse):
        o_ref[0, j * Q_BLK:(j + 1) * Q_BLK, :] = _rms(out, g_ref[...]).astype(BF16)

    _band_units(WIN_HALF, 1, tq, pl.program_id(1), pl.num_programs(1), sink_ref, q_ref.at[0], kcat, vcat, emit)


def _win_attn(qa, ka, va, sink, g_win, tq):
    b, s, _ = qa.shape
    side = WIN_HALF
    kw = ka.shape[-1]
    prev, main, nxt = _band_specs(tq, side, s, kw)
    qspec = pl.BlockSpec((1, tq, WIN_WIDTH), lambda bi, i: (bi, i, 0))
    return pl.pallas_call(
        functools.partial(_win_kernel, tq),
        grid=(b, s // tq),
        in_specs=[pl.BlockSpec(memory_space=pltpu.SMEM), qspec, prev, main, nxt, prev, main, nxt,
                  _const_spec((1, WIN_WIDTH))],
        out_specs=qspec,
        out_shape=jax.ShapeDtypeStruct((b, s, WIN_WIDTH), BF16),
        scratch_shapes=[pltpu.VMEM((tq + 2 * side, kw), BF16), pltpu.VMEM((tq + 2 * side, kw), BF16)],
        compiler_params=_params(("parallel", "parallel")),
        name="win_attn",
    )(sink, qa, ka, ka, ka, va, va, va, g_win)


def _dil_attn(qv, kv, vv, dil, tq):
    b, ln, wr = qv.shape
    side = DIL_SIDE
    lse_w = DIL_SLOTS if dil == 1 else LANES
    prev, main, nxt = _band_specs(tq, side, ln, wr)
    return pl.pallas_call(
        functools.partial(_dil_kernel, dil, tq),
        grid=(b, ln // tq),
        in_specs=[main, prev, main, nxt, prev, main, nxt],
        out_specs=[main, pl.BlockSpec((1, dil, tq, lse_w), lambda bi, i: (bi, 0, i, 0))],
        out_shape=[jax.ShapeDtypeStruct((b, ln, wr), BF16),
                   jax.ShapeDtypeStruct((b, dil, ln, lse_w), F32)],
        scratch_shapes=[pltpu.VMEM((tq + 2 * side, wr), BF16), pltpu.VMEM((tq + 2 * side, wr), BF16)],
        compiler_params=_params(("parallel", "parallel")),
        name=f"dil_attn_{dil}",
    )(qv, kv, kv, kv, vv, vv, vv)


def _mem_kernel(mem_ref, g_ref, b_ref, wk_ref, wv_ref, k_ref, v_ref):
    mn = _ln(mem_ref[0], g_ref[...], b_ref[...]).astype(BF16)
    k_ref[0] = _dot(mn, wk_ref[...]).astype(BF16)
    v_ref[0] = _dot(mn, wv_ref[...]).astype(BF16)


def _mem_kv(mem, g, b, wk, wv):
    bsz, m, d = mem.shape
    blk = pl.BlockSpec((1, m, d), lambda bi: (bi, 0, 0))
    return pl.pallas_call(
        _mem_kernel,
        grid=(bsz,),
        in_specs=[blk, _const_spec((1, d)), _const_spec((1, d)), _const_spec((d, d)), _const_spec((d, d))],
        out_specs=[blk, blk],
        out_shape=[jax.ShapeDtypeStruct((bsz, m, d), BF16)] * 2,
        compiler_params=_params(("parallel",)),
        name="mem_kv",
    )(mem, g, b, wk, wv)


def _mix_kernel(tm, oa_ref, o0_ref, o1_ref, o2_ref, l0_ref, l1_ref, l2_ref, h_ref, e_ref, gd_ref, wo_ref,
                g1_ref, b1_ref, wq_ref, kx_ref, vx_ref, wxo_ref, g2_ref, b2_ref, out_ref, stage, lstage):
    ts = tm // N_SUB
    n_chunks = DIL_WIDTH // LANES

    def sub_tile(k):
        rows = slice(k * ts, (k + 1) * ts)
        ls = []
        for gi, (l_ref, (_, dil)) in enumerate(zip((l0_ref, l1_ref, l2_ref), DIL_PAIRS)):
            if dil == 1:
                ls.append(l_ref[0, rows, :])
            else:
                rrows = slice(k * ts // dil, (k + 1) * ts // dil)
                for c in range(dil):
                    lstage[gi - 1, pl.ds(k * ts + c, ts // dil, stride=dil), :] = l_ref[0, c, rrows, :]
                ls.append(lstage[gi - 1, rows, 0:DIL_SLOTS])
        mx = jnp.maximum(jnp.maximum(ls[0], ls[1]), ls[2])
        es = [jnp.exp2(l - mx) for l in ls]
        tot = es[0] + es[1] + es[2]
        ob = None
        for gi, (e, o_ref, (_, dil)) in enumerate(zip(es, (o0_ref, o1_ref, o2_ref), DIL_PAIRS)):
            wexp = _dot((e / tot).astype(BF16), e_ref[...])
            if dil == 1:
                og = o_ref[0, rows, :].astype(F32)
            else:
                rrows = slice(k * ts // dil, (k + 1) * ts // dil)
                for c in range(dil):
                    for j in range(n_chunks):
                        col = c * DIL_WIDTH + j * LANES
                        stage[gi - 1, j, pl.ds(k * ts + c, ts // dil, stride=dil), :] = (
                            o_ref[0, rrows, col:col + LANES].astype(F32))
                og = jnp.concatenate([stage[gi - 1, j, rows, :] for j in range(n_chunks)], -1)
            term = wexp * og
            ob = term if ob is None else ob + term
        obn = _rms(ob, gd_ref[...]).astype(BF16)
        yield
        mix = _dot(oa_ref[0, rows, :], wo_ref[0:WIN_WIDTH, :]) + _dot(obn, wo_ref[WIN_WIDTH:, :])
        yield
        h1 = _ln(DEEPNORM_ALPHA * h_ref[0, rows, :] + mix, g1_ref[...], b1_ref[...])
        yield
        q = _dot(h1.astype(BF16), wq_ref[...]).astype(BF16)
        yield
        heads = []
        for hh in range(X_HEADS):
            cols = slice(hh * X_HEAD_DIM, (hh + 1) * X_HEAD_DIM)
            s = _dot_nt(q[:, cols], kx_ref[0, :, cols])
            yield
            m = jnp.max(s, -1, keepdims=True)
            p = jnp.exp(s - m)
            denom = jnp.sum(p, -1, keepdims=True)
            yield
            heads.append((_dot(p.astype(BF16), vx_ref[0, :, cols]) / denom).astype(BF16))
        xa = _dot(jnp.concatenate(heads, -1), wxo_ref[...])
        yield
        out_ref[0, rows, :] = _ln(DEEPNORM_ALPHA * h1 + xa, g2_ref[...], b2_ref[...])

    _round_robin([sub_tile(k) for k in range(N_SUB)])


def _mix_xattn(oa, os_, ls, h, expand, g_dil, w_out, g1, b1, w_xq, kx, vx, w_xo, g2, b2, tm):
    b, s, d = h.shape
    m = kx.shape[1]
    tok = lambda w: pl.BlockSpec((1, tm, w), lambda bi, i: (bi, i, 0))
    res = lambda dil: pl.BlockSpec((1, tm // dil, dil * DIL_WIDTH), lambda bi, i: (bi, i, 0))
    memspec = pl.BlockSpec((1, m, d), lambda bi, i: (bi, 0, 0))
    return pl.pallas_call(
        functools.partial(_mix_kernel, tm),
        grid=(b, s // tm),
        in_specs=[tok(WIN_WIDTH)] + [res(dil) for _, dil in DIL_PAIRS] +
                 [tok(DIL_SLOTS) if dil == 1 else
                  pl.BlockSpec((1, dil, tm // dil, LANES), lambda bi, i: (bi, 0, i, 0)) for _, dil in DIL_PAIRS] +
                 [tok(d),
                  _const_spec(expand.shape), _const_spec((1, DIL_WIDTH)), _const_spec((d, d)),
                  _const_spec((1, d)), _const_spec((1, d)), _const_spec((d, d)), memspec, memspec,
                  _const_spec((d, d)), _const_spec((1, d)), _const_spec((1, d))],
        out_specs=tok(d),
        out_shape=jax.ShapeDtypeStruct((b, s, d), F32),
        scratch_shapes=[pltpu.VMEM((N_DIL - 1, DIL_WIDTH // LANES, tm, LANES), F32),
                        pltpu.VMEM((N_DIL - 1, tm, LANES), F32)],
        compiler_params=_params(("parallel", "parallel")),
        name="mix_xattn",
    )(oa, *os_, *ls, h, expand, g_dil, w_out, g1, b1, w_xq, kx, vx, w_xo, g2, b2)


FF_CHUNKS = ((0, 1536), (1536, 1280))
HALO = 16
CONV_PHASES = 4


def _ffn_kernel(tm, hp_ref, hm_ref, hn_ref, wg_ref, wu_ref, cw_ref, cb_ref, wd_ref, g3_ref, b3_ref, out_ref,
                gs, us, ys):
    i = pl.program_id(1)
    n = pl.num_programs(1)
    hm = hm_ref[0]
    hb = hm.astype(BF16)
    hp = jnp.where(i > 0, hp_ref[0], 0.0).astype(BF16)
    hn = jnp.where(i < n - 1, hn_ref[0], 0.0).astype(BF16)
    hcat = jnp.concatenate([hp, hb, hn], 0)
    q = tm // CONV_PHASES
    acc = None
    for c0, width in FF_CHUNKS:
        gate = _dot(hcat, wg_ref[:, c0:c0 + width])
        up = _dot(hb, wu_ref[:, c0:c0 + width])
        n_slabs = width // LANES
        for sl in range(n_slabs):
            gs[sl] = gate[:, sl * LANES:(sl + 1) * LANES]
            us[sl] = up[:, sl * LANES:(sl + 1) * LANES]
        phases = []
        for ph in range(CONV_PHASES):
            slabs = []
            for sl in range(n_slabs):
                lanes = slice(c0 + sl * LANES, c0 + (sl + 1) * LANES)
                g = cb_ref[:, lanes]
                for j in range(CONV_WIDTH):
                    g = g + gs[sl, pl.ds(HALO - 1 + j + ph, q, stride=CONV_PHASES), :] * cw_ref[j:j + 1, lanes]
                u = us[sl, pl.ds(ph, q, stride=CONV_PHASES), :]
                slabs.append((0.5 * g * (1.0 + lax.erf(g * np.float32(np.sqrt(0.5)))) * u).astype(BF16))
            phases.append(jnp.concatenate(slabs, -1))
        part = _dot(jnp.concatenate(phases, 0), wd_ref[c0:c0 + width, :])
        acc = part if acc is None else acc + part
    for ph in range(CONV_PHASES):
        for sl in range(D_MODEL // LANES):
            ys[sl, pl.ds(ph, q, stride=CONV_PHASES), :] = acc[ph * q:(ph + 1) * q, sl * LANES:(sl + 1) * LANES]
    ff = jnp.concatenate([ys[sl] for sl in range(D_MODEL // LANES)], -1)
    out_ref[0] = _ln(DEEPNORM_ALPHA * hm + ff, g3_ref[...], b3_ref[...])


def _conv_glu(h2, wg, wu, cw, cb, wd, g3, b3, tm):
    b, s, d = h2.shape
    per = tm // HALO
    nblk = s // HALO
    main = pl.BlockSpec((1, tm, d), lambda bi, i: (bi, i, 0))
    prev = pl.BlockSpec((1, HALO, d), lambda bi, i: (bi, jnp.maximum(i * per - 1, 0), 0))
    nxt = pl.BlockSpec((1, HALO, d), lambda bi, i: (bi, jnp.minimum((i + 1) * per, nblk - 1), 0))
    n_slabs = max(w for _, w in FF_CHUNKS) // LANES
    return pl.pallas_call(
        functools.partial(_ffn_kernel, tm),
        grid=(b, s // tm),
        in_specs=[prev, main, nxt, _const_spec((d, D_FF)), _const_spec((d, D_FF)),
                  _const_spec((CONV_WIDTH, D_FF)), _const_spec((1, D_FF)), _const_spec((D_FF, d)),
                  _const_spec((1, d)), _const_spec((1, d))],
        out_specs=main,
        out_shape=jax.ShapeDtypeStruct((b, s, d), F32),
        scratch_shapes=[pltpu.VMEM((n_slabs, tm + 2 * HALO, LANES), F32), pltpu.VMEM((n_slabs, tm, LANES), F32),
                        pltpu.VMEM((d // LANES, tm, LANES), F32)],
        compiler_params=_params(("parallel", "parallel")),
        name="conv_glu",
    )(h2, h2, h2, wg, wu, cw, cb, wd, g3, b3)


def kernel(x, mem, positions, ln_in_g, ln_in_b, w_in, attn_sink, g_win, g_dil, w_mix_out, ln1_g, ln1_b, mem_ln_g, mem_ln_b, w_xq, w_xk, w_xv, w_xo, ln2_g, ln2_b, w_gate, w_up, conv_w, conv_b, w_down, ln3_g, ln3_b):
    b, s, d = x.shape
    assert DEPTH == 1
    l = 0
    row = lambda v: v.reshape(1, -1)
    cos, sin = _rope_tables(positions)

    qscale = HEAD_DIM ** -0.5 * LOG2E
    o_ka, o_va, o_qb = A_Q, A_Q + A_KV, A_Q + 2 * A_KV
    o_kb, o_vb = o_qb + B_QKV, o_qb + 2 * B_QKV
    names = ["qa", "ka", "va"]
    segs = [(0, A_Q, True, 1, qscale, False),
            (o_ka, A_KV, True, 1, 1.0, True),
            (o_va, A_KV, False, 1, 1.0, True)]
    for gi, (_, dil) in enumerate(DIL_PAIRS):
        c = gi * DIL_WIDTH
        names += [f"q{gi}", f"k{gi}", f"v{gi}"]
        segs += [(o_qb + c, DIL_WIDTH, True, dil, qscale, False),
                 (o_kb + c, DIL_WIDTH, True, dil, 1.0, False),
                 (o_vb + c, DIL_WIDTH, False, dil, 1.0, False)]

    xscale = X_HEAD_DIM ** -0.5
    to_cast = [(w_mix_out[l], 1.0), (w_xq[l], xscale), (w_xk[l], 1.0), (w_xv[l], 1.0), (w_xo[l], 1.0),
               (w_gate[l], 1.0), (w_up[l], 1.0), (w_down[l], 1.0)]
    outs = _qkv_proj(x.reshape(b * s, d), row(ln_in_g), row(ln_in_b), w_in[l].astype(BF16), tuple(segs),
                     cos, sin, to_cast, tm=512)
    n_seg = 1 + len(segs)
    wo_b, wq_b, wk_b, wv_b, wxo_b, wg_b, wu_b, wd_b = outs[n_seg:]
    hn = outs[0].reshape(b, s, d)
    proj = {name: t.reshape(b, t.shape[0] // b, t.shape[1]) for name, t in zip(names, outs[1:n_seg])}
    qa, ka, va = proj["qa"], proj["ka"], proj["va"]
    oa = _win_attn(qa, ka, va, attn_sink[l], row(g_win[l]), tq=1024)
    os_, ls = [], []
    for gi, (_, dil) in enumerate(DIL_PAIRS):
        qg, kg, vg = proj[f"q{gi}"], proj[f"k{gi}"], proj[f"v{gi}"]
        o, lse = _dil_attn(qg, kg, vg, dil, tq=max(2048 // dil, Q_BLK))
        os_.append(o)
        ls.append(lse.reshape(b, s, DIL_SLOTS) if dil == 1 else lse)

    kx, vx = _mem_kv(mem, row(mem_ln_g[l]), row(mem_ln_b[l]), wk_b, wv_b)
    expand = jnp.repeat(jnp.eye(DIL_SLOTS, dtype=BF16), HEAD_DIM, axis=1)
    h2 = _mix_xattn(oa, os_, ls, hn, expand, row(g_dil[l]), wo_b, row(ln1_g[l]), row(ln1_b[l]), wq_b, kx, vx,
                    wxo_b, row(ln2_g[l]), row(ln2_b[l]), tm=1024)
    return _conv_glu(h2, wg_b, wu_b, conv_w[l], row(conv_b[l]), wd_b, row(ln3_g[l]), row(ln3_b[l]), tm=512)
```

```python
import functools

import numpy as np
import jax
import jax.numpy as jnp
from jax import lax
from jax.experimental import pallas as pl
from jax.experimental.pallas import tpu as pltpu

D_MODEL = 1024
HEAD_DIM = 64
WIN_Q_HEADS = 8
WIN_KV_HEADS = 2
WIN_HALF = 128
DIL_SLOTS = 8
DIL_PAIRS = ((128, 1), (512, 4), (2048, 16))
N_DIL = len(DIL_PAIRS)
ROT_DIM = HEAD_DIM // 4
ROT_HALF = ROT_DIM // 2
ROPE_THETA = 500000.0
X_HEADS = 4
X_HEAD_DIM = D_MODEL // X_HEADS
D_FF = 2816
CONV_WIDTH = 3
WIN_WIDTH = WIN_Q_HEADS * HEAD_DIM
DIL_WIDTH = DIL_SLOTS * HEAD_DIM
A_Q = WIN_WIDTH
A_KV = WIN_KV_HEADS * HEAD_DIM
B_QKV = N_DIL * DIL_WIDTH
DEPTH = 1
DEEPNORM_ALPHA = (2 * DEPTH) ** 0.25
LN_EPS = 1e-5
NEG_INF = -1e30
LOG2E = float(np.log2(np.e))

LANES = 128
TOK_PER_ROW = LANES // ROT_HALF
Q_BLK = 128
N_SUB = 4
DIL_SIDE = 64
VMEM_LIMIT = 56 * 1024 * 1024

F32 = jnp.float32
BF16 = jnp.bfloat16


def _ln(x, g, b):
    mu = jnp.mean(x, -1, keepdims=True)
    xc = x - mu
    var = jnp.mean(xc * xc, -1, keepdims=True)
    return xc * lax.rsqrt(var + LN_EPS) * g + b


def _rms(x, g):
    return x * lax.rsqrt(jnp.mean(x * x, -1, keepdims=True) + LN_EPS) * g


def _dot(a, b):
    return jnp.dot(a, b, preferred_element_type=F32)


def _dot_nt(a, b):
    return lax.dot_general(a, b, (((1,), (1,)), ((), ())), preferred_element_type=F32)


def _const_spec(shape):
    nd = len(shape)
    return pl.BlockSpec(shape, lambda *_: (0,) * nd, pipeline_mode=pl.Buffered(1))


def _params(sem):
    return pltpu.CompilerParams(dimension_semantics=sem, vmem_limit_bytes=VMEM_LIMIT)


def _round_robin(gens):
    live = list(gens)
    while live:
        for g in list(live):
            try:
                next(g)
            except StopIteration:
                live.remove(g)


def _rope_tab_kernel(pos_ref, invf_ref, cos_ref, sin_ref):
    ang = pos_ref[...].astype(F32) * invf_ref[...]
    cos_ref[...] = jnp.cos(ang)
    sin_ref[...] = jnp.sin(ang)


def _rope_tables(positions):
    inv_freq = ROPE_THETA ** (-jnp.arange(0, ROT_DIM, 2, dtype=F32) / ROT_DIM)
    pos_rep = jnp.repeat(positions.reshape(-1), ROT_HALF).reshape(-1, LANES)
    invf = jnp.tile(inv_freq, TOK_PER_ROW).reshape(1, LANES)
    return pl.pallas_call(
        _rope_tab_kernel,
        out_shape=(jax.ShapeDtypeStruct(pos_rep.shape, F32),) * 2,
        name="rope_tables",
    )(pos_rep, invf)


def _expand_rope(tab, tm):
    rows = jnp.broadcast_to(tab[:, None, :], (tm // TOK_PER_ROW, TOK_PER_ROW, LANES)).reshape(tm, LANES)
    tok = lax.broadcasted_iota(jnp.int32, (tm, LANES), 0) % TOK_PER_ROW
    lane = lax.broadcasted_iota(jnp.int32, (tm, LANES), 1)
    idx = tok * ROT_HALF + lane % ROT_HALF
    return jnp.take_along_axis(rows, idx, axis=1, mode="promise_in_bounds")


def _qkv_kernel(segs, tm, scales, x_ref, g_ref, b_ref, w_ref, cos_ref, sin_ref, *rest):
    n_cast = len(scales)
    cast_in, rest = rest[:n_cast], rest[n_cast:]
    h_ref, out_refs, cast_out, stage = rest[0], rest[1:1 + len(segs)], rest[1 + len(segs):-1], rest[-1]
    h = _ln(x_ref[...], g_ref[...], b_ref[...])
    h_ref[...] = h
    hb = h.astype(BF16)
    for scale, wi_ref, wo_ref in zip(scales, cast_in, cast_out):
        wo_ref[...] = (wi_ref[...] * scale if scale != 1.0 else wi_ref[...]).astype(BF16)
    slot = lax.broadcasted_iota(jnp.int32, (tm, LANES), 1) % HEAD_DIM
    cos = _expand_rope(cos_ref[...], tm)
    sin = _expand_rope(sin_ref[...], tm)
    c_tab = jnp.where(slot < ROT_DIM, cos, 1.0)
    sa_tab = jnp.where(slot < ROT_HALF, sin, 0.0)
    sb_tab = jnp.where((slot >= ROT_HALF) & (slot < ROT_DIM), sin, 0.0)
    lo = slot == lax.broadcasted_iota(jnp.int32, (tm, LANES), 1)
    for (c0, width, rope, dil, scale, dup), o_ref in zip(segs, out_refs):
        z = _dot(hb, w_ref[:, c0:c0 + width])
        for j in range(width // LANES):
            zc = z[:, j * LANES:(j + 1) * LANES]
            if rope:
                up = pltpu.roll(zc, LANES - ROT_HALF, 1)
                dn = pltpu.roll(zc, ROT_HALF, 1)
                zc = zc * c_tab - up * sa_tab + dn * sb_tab
            if scale != 1.0:
                zc = zc * scale
            if dup:
                sw = pltpu.roll(zc, HEAD_DIM, 1)
                o_ref[:, 2 * j * LANES:(2 * j + 1) * LANES] = jnp.where(lo, zc, sw).astype(BF16)
                o_ref[:, (2 * j + 1) * LANES:(2 * j + 2) * LANES] = jnp.where(lo, sw, zc).astype(BF16)
            elif dil == 1:
                o_ref[:, j * LANES:(j + 1) * LANES] = zc.astype(BF16)
            else:
                stage[...] = zc
                for c in range(dil):
                    col = c * width + j * LANES
                    o_ref[:, col:col + LANES] = stage[pl.ds(c, tm // dil, stride=dil), :].astype(BF16)


BF16_ROWS = 16


def _qkv_proj(x2, ln_g, ln_b, w_all, segs, cos, sin, to_cast, tm):
    t = x2.shape[0]
    steps = t // tm
    wtot = w_all.shape[1]
    row = lambda r, w: pl.BlockSpec((r, w), lambda i: (i, 0))
    out_shape = [jax.ShapeDtypeStruct((t, D_MODEL), F32)]
    out_specs = [row(tm, D_MODEL)]
    for _, width, _, dil, _, dup in segs:
        wout = 2 * width if dup else dil * width
        out_shape.append(jax.ShapeDtypeStruct((t // dil, wout), BF16))
        out_specs.append(row(tm // dil, wout))
    cast_specs = []
    for w, _ in to_cast:
        rows, cols = w.shape
        nblk = max(n for n in range(1, steps + 1) if steps % n == 0 and rows % (n * BF16_ROWS) == 0)
        spec = pl.BlockSpec((rows // nblk, cols), lambda i, every=steps // nblk: (i // every, 0))
        cast_specs.append(spec)
        out_shape.append(jax.ShapeDtypeStruct((rows, cols), BF16))
        out_specs.append(spec)
    return pl.pallas_call(
        functools.partial(_qkv_kernel, segs, tm, tuple(s for _, s in to_cast)),
        grid=(steps,),
        in_specs=[row(tm, D_MODEL), _const_spec((1, D_MODEL)), _const_spec((1, D_MODEL)),
                  _const_spec((D_MODEL, wtot)), row(tm // TOK_PER_ROW, LANES), row(tm // TOK_PER_ROW, LANES)]
                 + cast_specs,
        out_specs=out_specs,
        out_shape=out_shape,
        scratch_shapes=[pltpu.VMEM((tm, LANES), F32)],
        compiler_params=_params(("arbitrary",)),
        name="qkv_proj",
    )(x2, ln_g, ln_b, w_all, cos, sin, *[w for w, _ in to_cast])


def _band_bias(nq, nk, side):
    qi = lax.broadcasted_iota(jnp.int32, (nq, nk), 0)
    kj = lax.broadcasted_iota(jnp.int32, (nq, nk), 1)
    rel = kj - side - qi
    return jnp.where((rel >= -side) & (rel <= side), 0.0, NEG_INF).astype(F32)


def _fill_halo_buffer(cat, prev, main, nxt, side, tq):
    cat[0:side] = prev
    cat[side:side + tq] = main
    cat[side + tq:] = nxt


def _band_units(side, dil, tq, tile, n_tiles, sink_ref, q_ref, kcat, vcat, emit, lse_w=0):
    is_win = sink_ref is not None
    blk = Q_BLK
    nk = blk + 2 * side
    n_blk = tq // blk
    width = q_ref.shape[-1] // dil
    n_pairs = width // LANES
    lo = lax.broadcasted_iota(jnp.int32, (blk, LANES), 1) < HEAD_DIM
    lane8 = lax.broadcasted_iota(jnp.int32, (blk, max(lse_w, 1)), 1)
    band = _band_bias(blk, nk, side)
    kj = lax.broadcasted_iota(jnp.int32, (blk, nk), 1)
    band_first = jnp.where((tile == 0) & (kj < side), NEG_INF, band)
    band_last = jnp.where((tile == n_tiles - 1) & (kj >= side + blk), NEG_INF, band)
    biases = []
    for j in range(n_blk):
        bias = band
        if j == 0:
            bias = band_first
        if j == n_blk - 1:
            bias = band_last if j > 0 else jnp.minimum(band_first, band_last)
        biases.append(jnp.concatenate([bias, bias], 0))

    units = [(j, c, pr) for j in range(n_blk) for c in range(dil) for pr in range(n_pairs)]

    def kv_cols(c, pr):
        if is_win:
            return slice((pr // 2) * LANES, (pr // 2 + 1) * LANES)
        return slice(c * width + pr * LANES, c * width + (pr + 1) * LANES)

    def scores(j, c, pr):
        r0 = j * blk
        q2 = q_ref[r0:r0 + blk, c * width + pr * LANES:c * width + (pr + 1) * LANES]
        zero = jnp.zeros_like(q2)
        qs = jnp.concatenate([jnp.where(lo, q2, zero), jnp.where(lo, zero, q2)], 0)
        return _dot_nt(qs, kcat[r0:r0 + nk, kv_cols(c, pr)]) + biases[j]

    def softmax(s, pr):
        ps, rdens, lses = [], [], []
        for hf in range(2):
            sh = s[hf * blk:(hf + 1) * blk]
            m = jnp.max(sh, -1, keepdims=True)
            if is_win:
                sink = sink_ref[2 * pr + hf] * LOG2E
                m = jnp.maximum(m, sink)
            ph = jnp.exp2(sh - m)
            denom = jnp.sum(ph, -1, keepdims=True)
            if is_win:
                denom = denom + jnp.exp2(sink - m)
            else:
                lses.append(m + jnp.log2(denom))
            ps.append(ph.astype(BF16))
            rdens.append(1.0 / denom)
        return jnp.concatenate(ps, 0), rdens, lses

    pairs, lse8 = [], None

    def finish(unit, o, rdens, lses):
        nonlocal pairs, lse8
        j, c, pr = unit
        pairs.append(jnp.where(lo, o[:blk] * rdens[0], o[blk:] * rdens[1]))
        if lse_w:
            if lse8 is None:
                lse8 = jnp.zeros((blk, lse_w), F32)
            lse8 = jnp.where(lane8 == 2 * pr, lses[0], lse8)
            lse8 = jnp.where(lane8 == 2 * pr + 1, lses[1], lse8)
        if pr == n_pairs - 1:
            emit(j, c, jnp.concatenate(pairs, -1), lse8)
            pairs, lse8 = [], None

    s_next = scores(*units[0])
    pending = None
    for n, (j, c, pr) in enumerate(units):
        s = s_next
        if n + 1 < len(units):
            s_next = scores(*units[n + 1])
        p, rdens, lses = softmax(s, pr)
        o = _dot(p, vcat[j * blk:j * blk + nk, kv_cols(c, pr)])
        if pending is not None:
            finish(*pending)
        pending = ((j, c, pr), o, rdens, lses)
    finish(*pending)


def _dil_kernel(dil, tq, q_ref, kp_ref, km_ref, kn_ref, vp_ref, vm_ref, vn_ref, o_ref, lse_ref, kcat, vcat):
    _fill_halo_buffer(kcat, kp_ref[0], km_ref[0], kn_ref[0], DIL_SIDE, tq)
    _fill_halo_buffer(vcat, vp_ref[0], vm_ref[0], vn_ref[0], DIL_SIDE, tq)
    width = q_ref.shape[-1] // dil

    def emit(j, c, out, lse):
        rows = slice(j * Q_BLK, (j + 1) * Q_BLK)
        o_ref[0, rows, c * width:(c + 1) * width] = out.astype(BF16)
        lse_ref[0, c, rows, :] = lse

    _band_units(DIL_SIDE, dil, tq, pl.program_id(1), pl.num_programs(1), None, q_ref.at[0], kcat, vcat, emit,
                lse_w=lse_ref.shape[-1])


def _band_specs(tq, side, n_rows, width):
    per = tq // side
    nblk = n_rows // side
    main = pl.BlockSpec((1, tq, width), lambda bi, i: (bi, i, 0))
    prev = pl.BlockSpec((1, side, width), lambda bi, i: (bi, jnp.maximum(i * per - 1, 0), 0))
    nxt = pl.BlockSpec((1, side, width), lambda bi, i: (bi, jnp.minimum((i + 1) * per, nblk - 1), 0))
    return prev, main, nxt


def _win_kernel(tq, sink_ref, q_ref, kp_ref, km_ref, kn_ref, vp_ref, vm_ref, vn_ref, g_ref, o_ref, kcat, vcat):
    _fill_halo_buffer(kcat, kp_ref[0], km_ref[0], kn_ref[0], WIN_HALF, tq)
    _fill_halo_buffer(vcat, vp_ref[0], vm_ref[0], vn_ref[0], WIN_HALF, tq)

    def emit(j, c, out, lse):
        o_ref[0, j * Q_BLK:(j + 1) * Q_BLK, :] = _rms(out, g_ref[...]).astype(BF16)

    _band_units(WIN_HALF, 1, tq, pl.program_id(1), pl.num_programs(1), sink_ref, q_ref.at[0], kcat, vcat, emit)


def _win_attn(qa, ka, va, sink, g_win, tq):
    b, s, _ = qa.shape
    side = WIN_HALF
    kw = ka.shape[-1]
    prev, main, nxt = _band_specs(tq, side, s, kw)
    qspec = pl.BlockSpec((1, tq, WIN_WIDTH), lambda bi, i: (bi, i, 0))
    return pl.pallas_call(
        functools.partial(_win_kernel, tq),
        grid=(b, s // tq),
        in_specs=[pl.BlockSpec(memory_space=pltpu.SMEM), qspec, prev, main, nxt, prev, main, nxt,
                  _const_spec((1, WIN_WIDTH))],
        out_specs=qspec,
        out_shape=jax.ShapeDtypeStruct((b, s, WIN_WIDTH), BF16),
        scratch_shapes=[pltpu.VMEM((tq + 2 * side, kw), BF16), pltpu.VMEM((tq + 2 * side, kw), BF16)],
        compiler_params=_params(("parallel", "parallel")),
        name="win_attn",
    )(sink, qa, ka, ka, ka, va, va, va, g_win)


def _dil_attn(qv, kv, vv, dil, tq):
    b, ln, wr = qv.shape
    side = DIL_SIDE
    lse_w = DIL_SLOTS if dil == 1 else LANES
    prev, main, nxt = _band_specs(tq, side, ln, wr)
    return pl.pallas_call(
        functools.partial(_dil_kernel, dil, tq),
        grid=(b, ln // tq),
        in_specs=[main, prev, main, nxt, prev, main, nxt],
        out_specs=[main, pl.BlockSpec((1, dil, tq, lse_w), lambda bi, i: (bi, 0, i, 0))],
        out_shape=[jax.ShapeDtypeStruct((b, ln, wr), BF16),
                   jax.ShapeDtypeStruct((b, dil, ln, lse_w), F32)],
        scratch_shapes=[pltpu.VMEM((tq + 2 * side, wr), BF16), pltpu.VMEM((tq + 2 * side, wr), BF16)],
        compiler_params=_params(("parallel", "parallel")),
        name=f"dil_attn_{dil}",
    )(qv, kv, kv, kv, vv, vv, vv)


def _mem_kernel(mem_ref, g_ref, b_ref, wk_ref, wv_ref, k_ref, v_ref):
    mn = _ln(mem_ref[0], g_ref[...], b_ref[...]).astype(BF16)
    k_ref[0] = _dot(mn, wk_ref[...]).astype(BF16)
    v_ref[0] = _dot(mn, wv_ref[...]).astype(BF16)


def _mem_kv(mem, g, b, wk, wv):
    bsz, m, d = mem.shape
    blk = pl.BlockSpec((1, m, d), lambda bi: (bi, 0, 0))
    return pl.pallas_call(
        _mem_kernel,
        grid=(bsz,),
        in_specs=[blk, _const_spec((1, d)), _const_spec((1, d)), _const_spec((d, d)), _const_spec((d, d))],
        out_specs=[blk, blk],
        out_shape=[jax.ShapeDtypeStruct((bsz, m, d), BF16)] * 2,
        compiler_params=_params(("parallel",)),
        name="mem_kv",
    )(mem, g, b, wk, wv)


def _mix_kernel(tm, oa_ref, o0_ref, o1_ref, o2_ref, l0_ref, l1_ref, l2_ref, h_ref, e_ref, gd_ref, wo_ref,
                g1_ref, b1_ref, wq_ref, kx_ref, vx_ref, wxo_ref, g2_ref, b2_ref, out_ref, stage, lstage):
    ts = tm // N_SUB
    n_chunks = DIL_WIDTH // LANES

    def sub_tile(k):
        rows = slice(k * ts, (k + 1) * ts)
        ls = []
        for gi, (l_ref, (_, dil)) in enumerate(zip((l0_ref, l1_ref, l2_ref), DIL_PAIRS)):
            if dil == 1:
                ls.append(l_ref[0, rows, :])
            else:
                rrows = slice(k * ts // dil, (k + 1) * ts // dil)
                for c in range(dil):
                    lstage[gi - 1, pl.ds(k * ts + c, ts // dil, stride=dil), :] = l_ref[0, c, rrows, :]
                ls.append(lstage[gi - 1, rows, 0:DIL_SLOTS])
        mx = jnp.maximum(jnp.maximum(ls[0], ls[1]), ls[2])
        es = [jnp.exp2(l - mx) for l in ls]
        tot = es[0] + es[1] + es[2]
        ob = None
        for gi, (e, o_ref, (_, dil)) in enumerate(zip(es, (o0_ref, o1_ref, o2_ref), DIL_PAIRS)):
            wexp = _dot((e / tot).astype(BF16), e_ref[...])
            if dil == 1:
                og = o_ref[0, rows, :].astype(F32)
            else:
                rrows = slice(k * ts // dil, (k + 1) * ts // dil)
                for c in range(dil):
                    for j in range(n_chunks):
                        col = c * DIL_WIDTH + j * LANES
                        stage[gi - 1, j, pl.ds(k * ts + c, ts // dil, stride=dil), :] = (
                            o_ref[0, rrows, col:col + LANES].astype(F32))
                og = jnp.concatenate([stage[gi - 1, j, rows, :] for j in range(n_chunks)], -1)
            term = wexp * og
            ob = term if ob is None else ob + term
        obn = _rms(ob, gd_ref[...]).astype(BF16)
        yield
        mix = _dot(oa_ref[0, rows, :], wo_ref[0:WIN_WIDTH, :]) + _dot(obn, wo_ref[WIN_WIDTH:, :])
        yield
        h1 = _ln(DEEPNORM_ALPHA * h_ref[0, rows, :] + mix, g1_ref[...], b1_ref[...])
        yield
        q = _dot(h1.astype(BF16), wq_ref[...]).astype(BF16)
        yield
        heads = []
        for hh in range(X_HEADS):
            cols = slice(hh * X_HEAD_DIM, (hh + 1) * X_HEAD_DIM)
            s = _dot_nt(q[:, cols], kx_ref[0, :, cols])
            yield
            m = jnp.max(s, -1, keepdims=True)
            p = jnp.exp(s - m)
            denom = jnp.sum(p, -1, keepdims=True)
            yield
            heads.append((_dot(p.astype(BF16), vx_ref[0, :, cols]) / denom).astype(BF16))
        xa = _dot(jnp.concatenate(heads, -1), wxo_ref[...])
        yield
        out_ref[0, rows, :] = _ln(DEEPNORM_ALPHA * h1 + xa, g2_ref[...], b2_ref[...])

    _round_robin([sub_tile(k) for k in range(N_SUB)])


def _mix_xattn(oa, os_, ls, h, expand, g_dil, w_out, g1, b1, w_xq, kx, vx, w_xo, g2, b2, tm):
    b, s, d = h.shape
    m = kx.shape[1]
    tok = lambda w: pl.BlockSpec((1, tm, w), lambda bi, i: (bi, i, 0))
    res = lambda dil: pl.BlockSpec((1, tm // dil, dil * DIL_WIDTH), lambda bi, i: (bi, i, 0))
    memspec = pl.BlockSpec((1, m, d), lambda bi, i: (bi, 0, 0))
    return pl.pallas_call(
        functools.partial(_mix_kernel, tm),
        grid=(b, s // tm),
        in_specs=[tok(WIN_WIDTH)] + [res(dil) for _, dil in DIL_PAIRS] +
                 [tok(DIL_SLOTS) if dil == 1 else
                  pl.BlockSpec((1, dil, tm // dil, LANES), lambda bi, i: (bi, 0, i, 0)) for _, dil in DIL_PAIRS] +
                 [tok(d),
                  _const_spec(expand.shape), _const_spec((1, DIL_WIDTH)), _const_spec((d, d)),
                  _const_spec((1, d)), _const_spec((1, d)), _const_spec((d, d)), memspec, memspec,
                  _const_spec((d, d)), _const_spec((1, d)), _const_spec((1, d))],
        out_specs=tok(d),
        out_shape=jax.ShapeDtypeStruct((b, s, d), F32),
        scratch_shapes=[pltpu.VMEM((N_DIL - 1, DIL_WIDTH // LANES, tm, LANES), F32),
                        pltpu.VMEM((N_DIL - 1, tm, LANES), F32)],
        compiler_params=_params(("parallel", "parallel")),
        name="mix_xattn",
    )(oa, *os_, *ls, h, expand, g_dil, w_out, g1, b1, w_xq, kx, vx, w_xo, g2, b2)


FF_CHUNKS = ((0, 1536), (1536, 1280))
HALO = 16
CONV_PHASES = 4


def _ffn_kernel(tm, hp_ref, hm_ref, hn_ref, wg_ref, wu_ref, cw_ref, cb_ref, wd_ref, g3_ref, b3_ref, out_ref,
                gs, us, ys):
    i = pl.program_id(1)
    n = pl.num_programs(1)
    hm = hm_ref[0]
    hb = hm.astype(BF16)
    hp = jnp.where(i > 0, hp_ref[0], 0.0).astype(BF16)
    hn = jnp.where(i < n - 1, hn_ref[0], 0.0).astype(BF16)
    hcat = jnp.concatenate([hp, hb, hn], 0)
    q = tm // CONV_PHASES
    acc = None
    for c0, width in FF_CHUNKS:
        gate = _dot(hcat, wg_ref[:, c0:c0 + width])
        up = _dot(hb, wu_ref[:, c0:c0 + width])
        n_slabs = width // LANES
        for sl in range(n_slabs):
            gs[sl] = gate[:, sl * LANES:(sl + 1) * LANES]
            us[sl] = up[:, sl * LANES:(sl + 1) * LANES]
        phases = []
        for ph in range(CONV_PHASES):
            slabs = []
            for sl in range(n_slabs):
                lanes = slice(c0 + sl * LANES, c0 + (sl + 1) * LANES)
                g = cb_ref[:, lanes]
                for j in range(CONV_WIDTH):
                    g = g + gs[sl, pl.ds(HALO - 1 + j + ph, q, stride=CONV_PHASES), :] * cw_ref[j:j + 1, lanes]
                u = us[sl, pl.ds(ph, q, stride=CONV_PHASES), :]
                slabs.append((0.5 * g * (1.0 + lax.erf(g * np.float32(np.sqrt(0.5)))) * u).astype(BF16))
            phases.append(jnp.concatenate(slabs, -1))
        part = _dot(jnp.concatenate(phases, 0), wd_ref[c0:c0 + width, :])
        acc = part if acc is None else acc + part
    for ph in range(CONV_PHASES):
        for sl in range(D_MODEL // LANES):
            ys[sl, pl.ds(ph, q, stride=CONV_PHASES), :] = acc[ph * q:(ph + 1) * q, sl * LANES:(sl + 1) * LANES]
    ff = jnp.concatenate([ys[sl] for sl in range(D_MODEL // LANES)], -1)
    out_ref[0] = _ln(DEEPNORM_ALPHA * hm + ff, g3_ref[...], b3_ref[...])


def _conv_glu(h2, wg, wu, cw, cb, wd, g3, b3, tm):
    b, s, d = h2.shape
    per = tm // HALO
    nblk = s // HALO
    main = pl.BlockSpec((1, tm, d), lambda bi, i: (bi, i, 0))
    prev = pl.BlockSpec((1, HALO, d), lambda bi, i: (bi, jnp.maximum(i * per - 1, 0), 0))
    nxt = pl.BlockSpec((1, HALO, d), lambda bi, i: (bi, jnp.minimum((i + 1) * per, nblk - 1), 0))
    n_slabs = max(w for _, w in FF_CHUNKS) // LANES
    return pl.pallas_call(
        functools.partial(_ffn_kernel, tm),
        grid=(b, s // tm),
        in_specs=[prev, main, nxt, _const_spec((d, D_FF)), _const_spec((d, D_FF)),
                  _const_spec((CONV_WIDTH, D_FF)), _const_spec((1, D_FF)), _const_spec((D_FF, d)),
                  _const_spec((1, d)), _const_spec((1, d))],
        out_specs=main,
        out_shape=jax.ShapeDtypeStruct((b, s, d), F32),
        scratch_shapes=[pltpu.VMEM((n_slabs, tm + 2 * HALO, LANES), F32), pltpu.VMEM((n_slabs, tm, LANES), F32),
                        pltpu.VMEM((d // LANES, tm, LANES), F32)],
        compiler_params=_params(("parallel", "parallel")),
        name="conv_glu",
    )(h2, h2, h2, wg, wu, cw, cb, wd, g3, b3)


def kernel(x, mem, positions, ln_in_g, ln_in_b, w_in, attn_sink, g_win, g_dil, w_mix_out, ln1_g, ln1_b, mem_ln_g, mem_ln_b, w_xq, w_xk, w_xv, w_xo, ln2_g, ln2_b, w_gate, w_up, conv_w, conv_b, w_down, ln3_g, ln3_b):
    b, s, d = x.shape
    assert DEPTH == 1
    l = 0
    row = lambda v: v.reshape(1, -1)
    cos, sin = _rope_tables(positions)

    qscale = HEAD_DIM ** -0.5 * LOG2E
    o_ka, o_va, o_qb = A_Q, A_Q + A_KV, A_Q + 2 * A_KV
    o_kb, o_vb = o_qb + B_QKV, o_qb + 2 * B_QKV
    names = ["qa", "ka", "va"]
    segs = [(0, A_Q, True, 1, qscale, False),
            (o_ka, A_KV, True, 1, 1.0, True),
            (o_va, A_KV, False, 1, 1.0, True)]
    for gi, (_, dil) in enumerate(DIL_PAIRS):
        c = gi * DIL_WIDTH
        names += [f"q{gi}", f"k{gi}", f"v{gi}"]
        segs += [(o_qb + c, DIL_WIDTH, True, dil, qscale, False),
                 (o_kb + c, DIL_WIDTH, True, dil, 1.0, False),
                 (o_vb + c, DIL_WIDTH, False, dil, 1.0, False)]

    xscale = X_HEAD_DIM ** -0.5
    to_cast = [(w_mix_out[l], 1.0), (w_xq[l], xscale), (w_xk[l], 1.0), (w_xv[l], 1.0), (w_xo[l], 1.0),
               (w_gate[l], 1.0), (w_up[l], 1.0), (w_down[l], 1.0)]
    outs = _qkv_proj(x.reshape(b * s, d), row(ln_in_g), row(ln_in_b), w_in[l].astype(BF16), tuple(segs),
                     cos, sin, to_cast, tm=512)
    n_seg = 1 + len(segs)
    wo_b, wq_b, wk_b, wv_b, wxo_b, wg_b, wu_b, wd_b = outs[n_seg:]
    hn = outs[0].reshape(b, s, d)
    proj = {name: t.reshape(b, t.shape[0] // b, t.shape[1]) for name, t in zip(names, outs[1:n_seg])}
    qa, ka, va = proj["qa"], proj["ka"], proj["va"]
    oa = _win_attn(qa, ka, va, attn_sink[l], row(g_win[l]), tq=1024)
    os_, ls = [], []
    for gi, (_, dil) in enumerate(DIL_PAIRS):
        qg, kg, vg = proj[f"q{gi}"], proj[f"k{gi}"], proj[f"v{gi}"]
        o, lse = _dil_attn(qg, kg, vg, dil, tq=max(2048 // dil, Q_BLK))
        os_.append(o)
        ls.append(lse.reshape(b, s, DIL_SLOTS) if dil == 1 else lse)

    kx, vx = _mem_kv(mem, row(mem_ln_g[l]), row(mem_ln_b[l]), wk_b, wv_b)
    expand = jnp.repeat(jnp.eye(DIL_SLOTS, dtype=BF16), HEAD_DIM, axis=1)
    h2 = _mix_xattn(oa, os_, ls, hn, expand, row(g_dil[l]), wo_b, row(ln1_g[l]), row(ln1_b[l]), wq_b, kx, vx,
                    wxo_b, row(ln2_g[l]), row(ln2_b[l]), tm=1024)
    return _conv_glu(h2, wg_b, wu_b, conv_w[l], row(conv_b[l]), wd_b, row(ln3_g[l]), row(ln3_b[l]), tm=512)
```

```python
import functools

import numpy as np
import jax
import jax.numpy as jnp
from jax import lax
from jax.experimental import pallas as pl
from jax.experimental.pallas import tpu as pltpu

D_MODEL = 1024
HEAD_DIM = 64
WIN_Q_HEADS = 8
WIN_KV_HEADS = 2
WIN_HALF = 128
DIL_SLOTS = 8
DIL_PAIRS = ((128, 1), (512, 4), (2048, 16))
N_DIL = len(DIL_PAIRS)
ROT_DIM = HEAD_DIM // 4
ROT_HALF = ROT_DIM // 2
ROPE_THETA = 500000.0
X_HEADS = 4
X_HEAD_DIM = D_MODEL // X_HEADS
D_FF = 2816
CONV_WIDTH = 3
WIN_WIDTH = WIN_Q_HEADS * HEAD_DIM
DIL_WIDTH = DIL_SLOTS * HEAD_DIM
A_Q = WIN_WIDTH
A_KV = WIN_KV_HEADS * HEAD_DIM
B_QKV = N_DIL * DIL_WIDTH
DEPTH = 1
DEEPNORM_ALPHA = (2 * DEPTH) ** 0.25
LN_EPS = 1e-5
NEG_INF = -1e30
LOG2E = float(np.log2(np.e))

LANES = 128
TOK_PER_ROW = LANES // ROT_HALF
Q_BLK = 128
N_SUB = 4
DIL_SIDE = 64
VMEM_LIMIT = 56 * 1024 * 1024
QKV_TILE = 512
WIN_TILE = 2048
DIL_TOKENS = 2048
MIX_TILE = 1024
FFN_TILE = 512

F32 = jnp.float32
BF16 = jnp.bfloat16


def _ln(x, g, b):
    mu = jnp.mean(x, -1, keepdims=True)
    xc = x - mu
    var = jnp.mean(xc * xc, -1, keepdims=True)
    return xc * lax.rsqrt(var + LN_EPS) * g + b


def _rms(x, g):
    return x * lax.rsqrt(jnp.mean(x * x, -1, keepdims=True) + LN_EPS) * g


def _dot(a, b):
    return jnp.dot(a, b, preferred_element_type=F32)


def _dot_nt(a, b):
    return lax.dot_general(a, b, (((1,), (1,)), ((), ())), preferred_element_type=F32)


def _const_spec(shape):
    nd = len(shape)
    return pl.BlockSpec(shape, lambda *_: (0,) * nd, pipeline_mode=pl.Buffered(1))


def _params(sem):
    return pltpu.CompilerParams(dimension_semantics=sem, vmem_limit_bytes=VMEM_LIMIT)


def _round_robin(gens):
    live = list(gens)
    while live:
        for g in list(live):
            try:
                next(g)
            except StopIteration:
                live.remove(g)


def _rope_tab_kernel(pos_ref, invf_ref, cos_ref, sin_ref):
    ang = pos_ref[...].astype(F32) * invf_ref[...]
    cos_ref[...] = jnp.cos(ang)
    sin_ref[...] = jnp.sin(ang)


def _rope_tables(positions):
    inv_freq = ROPE_THETA ** (-jnp.arange(0, ROT_DIM, 2, dtype=F32) / ROT_DIM)
    pos_rep = jnp.repeat(positions.reshape(-1), ROT_HALF).reshape(-1, LANES)
    invf = jnp.tile(inv_freq, TOK_PER_ROW).reshape(1, LANES)
    return pl.pallas_call(
        _rope_tab_kernel,
        out_shape=(jax.ShapeDtypeStruct(pos_rep.shape, F32),) * 2,
        name="rope_tables",
    )(pos_rep, invf)


def _expand_rope(tab, tm):
    rows = jnp.broadcast_to(tab[:, None, :], (tm // TOK_PER_ROW, TOK_PER_ROW, LANES)).reshape(tm, LANES)
    tok = lax.broadcasted_iota(jnp.int32, (tm, LANES), 0) % TOK_PER_ROW
    lane = lax.broadcasted_iota(jnp.int32, (tm, LANES), 1)
    idx = tok * ROT_HALF + lane % ROT_HALF
    return jnp.take_along_axis(rows, idx, axis=1, mode="promise_in_bounds")


QKV_DOT_WIDTH = 1536


def _run_width(segs, c0):
    starts = {s[0]: s[1] for s in segs}
    width = 0
    while c0 + width in starts and width + starts[c0 + width] <= QKV_DOT_WIDTH:
        width += starts[c0 + width]
    return width


def _qkv_kernel(segs, tm, scales, x_ref, g_ref, b_ref, w_ref, cos_ref, sin_ref, *rest):
    n_cast = len(scales)
    cast_in, rest = rest[:n_cast], rest[n_cast:]
    h_ref, out_refs, cast_out, stage = rest[0], rest[1:1 + len(segs)], rest[1 + len(segs):-1], rest[-1]
    h = _ln(x_ref[...], g_ref[...], b_ref[...])
    h_ref[...] = h
    hb = h.astype(BF16)
    for scale, wi_ref, wo_ref in zip(scales, cast_in, cast_out):
        wo_ref[...] = (wi_ref[...] * scale if scale != 1.0 else wi_ref[...]).astype(BF16)
    slot = lax.broadcasted_iota(jnp.int32, (tm, LANES), 1) % HEAD_DIM
    cos = _expand_rope(cos_ref[...], tm)
    sin = _expand_rope(sin_ref[...], tm)
    c_tab = jnp.where(slot < ROT_DIM, cos, 1.0)
    sa_tab = jnp.where(slot < ROT_HALF, sin, 0.0)
    sb_tab = jnp.where((slot >= ROT_HALF) & (slot < ROT_DIM), sin, 0.0)
    lo = slot == lax.broadcasted_iota(jnp.int32, (tm, LANES), 1)
    z, z0 = None, 0
    for (c0, width, rope, dil, scale, dup), o_ref in zip(segs, out_refs):
        if z is None or c0 + width > z0 + z.shape[1]:
            z0 = c0
            z = _dot(hb, w_ref[:, z0:z0 + _run_width(segs, c0)])
        for j in range(width // LANES):
            zc = z[:, c0 - z0 + j * LANES:c0 - z0 + (j + 1) * LANES]
            if rope:
                up = pltpu.roll(zc, LANES - ROT_HALF, 1)
                dn = pltpu.roll(zc, ROT_HALF, 1)
                zc = zc * c_tab - up * sa_tab + dn * sb_tab
            if scale != 1.0:
                zc = zc * scale
            if dup:
                sw = pltpu.roll(zc, HEAD_DIM, 1)
                o_ref[:, 2 * j * LANES:(2 * j + 1) * LANES] = jnp.where(lo, zc, sw).astype(BF16)
                o_ref[:, (2 * j + 1) * LANES:(2 * j + 2) * LANES] = jnp.where(lo, sw, zc).astype(BF16)
            elif dil == 1:
                o_ref[:, j * LANES:(j + 1) * LANES] = zc.astype(BF16)
            else:
                stage[...] = zc
                for c in range(dil):
                    col = c * width + j * LANES
                    o_ref[:, col:col + LANES] = stage[pl.ds(c, tm // dil, stride=dil), :].astype(BF16)


BF16_ROWS = 16


def _qkv_proj(x2, ln_g, ln_b, w_all, segs, cos, sin, to_cast, tm):
    t = x2.shape[0]
    steps = t // tm
    wtot = w_all.shape[1]
    row = lambda r, w: pl.BlockSpec((r, w), lambda i: (i, 0))
    out_shape = [jax.ShapeDtypeStruct((t, D_MODEL), F32)]
    out_specs = [row(tm, D_MODEL)]
    for _, width, _, dil, _, dup in segs:
        wout = 2 * width if dup else dil * width
        out_shape.append(jax.ShapeDtypeStruct((t // dil, wout), BF16))
        out_specs.append(row(tm // dil, wout))
    cast_specs = []
    for w, _ in to_cast:
        rows, cols = w.shape
        nblk = max(n for n in range(1, steps + 1) if steps % n == 0 and rows % (n * BF16_ROWS) == 0)
        spec = pl.BlockSpec((rows // nblk, cols), lambda i, every=steps // nblk: (i // every, 0))
        cast_specs.append(spec)
        out_shape.append(jax.ShapeDtypeStruct((rows, cols), BF16))
        out_specs.append(spec)
    return pl.pallas_call(
        functools.partial(_qkv_kernel, segs, tm, tuple(s for _, s in to_cast)),
        grid=(steps,),
        in_specs=[row(tm, D_MODEL), _const_spec((1, D_MODEL)), _const_spec((1, D_MODEL)),
                  _const_spec((D_MODEL, wtot)), row(tm // TOK_PER_ROW, LANES), row(tm // TOK_PER_ROW, LANES)]
                 + cast_specs,
        out_specs=out_specs,
        out_shape=out_shape,
        scratch_shapes=[pltpu.VMEM((tm, LANES), F32)],
        compiler_params=_params(("arbitrary",)),
        name="qkv_proj",
    )(x2, ln_g, ln_b, w_all, cos, sin, *[w for w, _ in to_cast])


def _band_bias(nq, nk, side):
    qi = lax.broadcasted_iota(jnp.int32, (nq, nk), 0)
    kj = lax.broadcasted_iota(jnp.int32, (nq, nk), 1)
    rel = kj - side - qi
    return jnp.where((rel >= -side) & (rel <= side), 0.0, NEG_INF).astype(F32)


def _halo_rows(parts, lo, hi, cols):
    prev, main, nxt = parts
    side, tq = prev.shape[0], main.shape[0]
    pieces = []
    if lo < 0:
        pieces.append(prev[side + lo:side, cols])
    pieces.append(main[max(lo, 0):min(hi, tq), cols])
    if hi > tq:
        pieces.append(nxt[0:hi - tq, cols])
    return pieces[0] if len(pieces) == 1 else jnp.concatenate(pieces, 0)


def _band_units(side, dil, tq, tile, n_tiles, sink_ref, q_ref, k_parts, v_parts, emit, lse_w=0):
    is_win = sink_ref is not None
    blk = Q_BLK
    nk = blk + 2 * side
    n_blk = tq // blk
    width = q_ref.shape[-1] // dil
    n_pairs = width // LANES
    lo = lax.broadcasted_iota(jnp.int32, (blk, LANES), 1) < HEAD_DIM
    lane8 = lax.broadcasted_iota(jnp.int32, (blk, max(lse_w, 1)), 1)
    band = _band_bias(blk, nk, side)
    kj = lax.broadcasted_iota(jnp.int32, (blk, nk), 1)
    band_first = jnp.where((tile == 0) & (kj < side), NEG_INF, band)
    band_last = jnp.where((tile == n_tiles - 1) & (kj >= side + blk), NEG_INF, band)
    biases = []
    for j in range(n_blk):
        bias = band
        if j == 0:
            bias = band_first
        if j == n_blk - 1:
            bias = band_last if j > 0 else jnp.minimum(band_first, band_last)
        biases.append(jnp.concatenate([bias, bias], 0))

    units = [(j, c, pr) for j in range(n_blk) for c in range(dil) for pr in range(n_pairs)]

    def kv_cols(c, pr):
        if is_win:
            return slice((pr // 2) * LANES, (pr // 2 + 1) * LANES)
        return slice(c * width + pr * LANES, c * width + (pr + 1) * LANES)

    def scores(j, c, pr):
        r0 = j * blk
        q2 = q_ref[r0:r0 + blk, c * width + pr * LANES:c * width + (pr + 1) * LANES]
        zero = jnp.zeros_like(q2)
        qs = jnp.concatenate([jnp.where(lo, q2, zero), jnp.where(lo, zero, q2)], 0)
        keys = _halo_rows(k_parts, r0 - side, r0 + blk + side, kv_cols(c, pr))
        return _dot_nt(qs, keys) + biases[j]

    def softmax(s, pr):
        ps, rdens, lses = [], [], []
        for hf in range(2):
            sh = s[hf * blk:(hf + 1) * blk]
            m = jnp.max(sh, -1, keepdims=True)
            if is_win:
                sink = sink_ref[2 * pr + hf] * LOG2E
                m = jnp.maximum(m, sink)
            ph = jnp.exp2(sh - m)
            denom = jnp.sum(ph, -1, keepdims=True)
            if is_win:
                denom = denom + jnp.exp2(sink - m)
            else:
                lses.append(m + jnp.log2(denom))
            ps.append(ph.astype(BF16))
            rdens.append(1.0 / denom)
        return jnp.concatenate(ps, 0), rdens, lses

    pairs, lse8 = [], None

    def finish(unit, o, rdens, lses):
        nonlocal pairs, lse8
        j, c, pr = unit
        pairs.append(jnp.where(lo, o[:blk] * rdens[0], o[blk:] * rdens[1]))
        if lse_w:
            if lse8 is None:
                lse8 = jnp.zeros((blk, lse_w), F32)
            lse8 = jnp.where(lane8 == 2 * pr, lses[0], lse8)
            lse8 = jnp.where(lane8 == 2 * pr + 1, lses[1], lse8)
        if pr == n_pairs - 1:
            emit(j, c, jnp.concatenate(pairs, -1), lse8)
            pairs, lse8 = [], None

    s_next = scores(*units[0])
    pending = None
    for n, (j, c, pr) in enumerate(units):
        s = s_next
        if n + 1 < len(units):
            s_next = scores(*units[n + 1])
        p, rdens, lses = softmax(s, pr)
        o = _dot(p, _halo_rows(v_parts, j * blk - side, (j + 1) * blk + side, kv_cols(c, pr)))
        if pending is not None:
            finish(*pending)
        pending = ((j, c, pr), o, rdens, lses)
    finish(*pending)


def _dil_kernel(dil, tq, q_ref, kp_ref, km_ref, kn_ref, vp_ref, vm_ref, vn_ref, o_ref, lse_ref):
    width = q_ref.shape[-1] // dil

    def emit(j, c, out, lse):
        rows = slice(j * Q_BLK, (j + 1) * Q_BLK)
        o_ref[0, rows, c * width:(c + 1) * width] = out.astype(BF16)
        lse_ref[0, c, rows, :] = lse

    _band_units(DIL_SIDE, dil, tq, pl.program_id(1), pl.num_programs(1), None, q_ref.at[0],
                (kp_ref.at[0], km_ref.at[0], kn_ref.at[0]), (vp_ref.at[0], vm_ref.at[0], vn_ref.at[0]), emit,
                lse_w=lse_ref.shape[-1])


def _band_specs(tq, side, n_rows, width):
    per = tq // side
    nblk = n_rows // side
    main = pl.BlockSpec((1, tq, width), lambda bi, i: (bi, i, 0))
    prev = pl.BlockSpec((1, side, width), lambda bi, i: (bi, jnp.maximum(i * per - 1, 0), 0))
    nxt = pl.BlockSpec((1, side, width), lambda bi, i: (bi, jnp.minimum((i + 1) * per, nblk - 1), 0))
    return prev, main, nxt


def _win_kernel(tq, sink_ref, q_ref, kp_ref, km_ref, kn_ref, vp_ref, vm_ref, vn_ref, g_ref, o_ref):

    def emit(j, c, out, lse):
        o_ref[0, j * Q_BLK:(j + 1) * Q_BLK, :] = _rms(out, g_ref[...]).astype(BF16)

    _band_units(WIN_HALF, 1, tq, pl.program_id(1), pl.num_programs(1), sink_ref, q_ref.at[0],
                (kp_ref.at[0], km_ref.at[0], kn_ref.at[0]), (vp_ref.at[0], vm_ref.at[0], vn_ref.at[0]), emit)


def _win_attn(qa, ka, va, sink, g_win, tq):
    b, s, _ = qa.shape
    side = WIN_HALF
    kw = ka.shape[-1]
    prev, main, nxt = _band_specs(tq, side, s, kw)
    qspec = pl.BlockSpec((1, tq, WIN_WIDTH), lambda bi, i: (bi, i, 0))
    return pl.pallas_call(
        functools.partial(_win_kernel, tq),
        grid=(b, s // tq),
        in_specs=[pl.BlockSpec(memory_space=pltpu.SMEM), qspec, prev, main, nxt, prev, main, nxt,
                  _const_spec((1, WIN_WIDTH))],
        out_specs=qspec,
        out_shape=jax.ShapeDtypeStruct((b, s, WIN_WIDTH), BF16),
        compiler_params=_params(("parallel", "parallel")),
        name="win_attn",
    )(sink, qa, ka, ka, ka, va, va, va, g_win)


def _dil_attn(qv, kv, vv, dil, tq):
    b, ln, wr = qv.shape
    side = DIL_SIDE
    lse_w = DIL_SLOTS if dil == 1 else LANES
    prev, main, nxt = _band_specs(tq, side, ln, wr)
    return pl.pallas_call(
        functools.partial(_dil_kernel, dil, tq),
        grid=(b, ln // tq),
        in_specs=[main, prev, main, nxt, prev, main, nxt],
        out_specs=[main, pl.BlockSpec((1, dil, tq, lse_w), lambda bi, i: (bi, 0, i, 0))],
        out_shape=[jax.ShapeDtypeStruct((b, ln, wr), BF16),
                   jax.ShapeDtypeStruct((b, dil, ln, lse_w), F32)],
        compiler_params=_params(("parallel", "parallel")),
        name=f"dil_attn_{dil}",
    )(qv, kv, kv, kv, vv, vv, vv)


def _mem_kernel(mem_ref, g_ref, b_ref, wk_ref, wv_ref, k_ref, v_ref):
    mn = _ln(mem_ref[0], g_ref[...], b_ref[...]).astype(BF16)
    k_ref[0] = _dot(mn, wk_ref[...]).astype(BF16)
    v_ref[0] = _dot(mn, wv_ref[...]).astype(BF16)


def _mem_kv(mem, g, b, wk, wv):
    bsz, m, d = mem.shape
    blk = pl.BlockSpec((1, m, d), lambda bi: (bi, 0, 0))
    return pl.pallas_call(
        _mem_kernel,
        grid=(bsz,),
        in_specs=[blk, _const_spec((1, d)), _const_spec((1, d)), _const_spec((d, d)), _const_spec((d, d))],
        out_specs=[blk, blk],
        out_shape=[jax.ShapeDtypeStruct((bsz, m, d), BF16)] * 2,
        compiler_params=_params(("parallel",)),
        name="mem_kv",
    )(mem, g, b, wk, wv)


def _mix_kernel(tm, oa_ref, o0_ref, o1_ref, o2_ref, l0_ref, l1_ref, l2_ref, h_ref, e_ref, gd_ref, wo_ref,
                g1_ref, b1_ref, wq_ref, kx_ref, vx_ref, wxo_ref, g2_ref, b2_ref, out_ref, stage, lstage):
    ts = tm // N_SUB
    n_chunks = DIL_WIDTH // LANES

    def sub_tile(k):
        rows = slice(k * ts, (k + 1) * ts)
        ls = []
        for gi, (l_ref, (_, dil)) in enumerate(zip((l0_ref, l1_ref, l2_ref), DIL_PAIRS)):
            if dil == 1:
                ls.append(l_ref[0, rows, :])
            else:
                rrows = slice(k * ts // dil, (k + 1) * ts // dil)
                for c in range(dil):
                    lstage[gi - 1, pl.ds(k * ts + c, ts // dil, stride=dil), :] = l_ref[0, c, rrows, :]
                ls.append(lstage[gi - 1, rows, 0:DIL_SLOTS])
        mx = jnp.maximum(jnp.maximum(ls[0], ls[1]), ls[2])
        es = [jnp.exp2(l - mx) for l in ls]
        tot = es[0] + es[1] + es[2]
        ob = None
        for gi, (e, o_ref, (_, dil)) in enumerate(zip(es, (o0_ref, o1_ref, o2_ref), DIL_PAIRS)):
            wexp = _dot((e / tot).astype(BF16), e_ref[...])
            if dil == 1:
                og = o_ref[0, rows, :].astype(F32)
            else:
                rrows = slice(k * ts // dil, (k + 1) * ts // dil)
                for c in range(dil):
                    for j in range(n_chunks):
                        col = c * DIL_WIDTH + j * LANES
                        stage[gi - 1, j, pl.ds(k * ts + c, ts // dil, stride=dil), :] = (
                            o_ref[0, rrows, col:col + LANES].astype(F32))
                og = jnp.concatenate([stage[gi - 1, j, rows, :] for j in range(n_chunks)], -1)
            term = wexp * og
            ob = term if ob is None else ob + term
        obn = _rms(ob, gd_ref[...]).astype(BF16)
        yield
        mix = _dot(oa_ref[0, rows, :], wo_ref[0:WIN_WIDTH, :]) + _dot(obn, wo_ref[WIN_WIDTH:, :])
        yield
        h1 = _ln(DEEPNORM_ALPHA * h_ref[0, rows, :] + mix, g1_ref[...], b1_ref[...])
        yield
        q = _dot(h1.astype(BF16), wq_ref[...]).astype(BF16)
        yield
        heads = []
        for hh in range(X_HEADS):
            cols = slice(hh * X_HEAD_DIM, (hh + 1) * X_HEAD_DIM)
            s = _dot_nt(q[:, cols], kx_ref[0, :, cols])
            yield
            m = jnp.max(s, -1, keepdims=True)
            p = jnp.exp(s - m)
            denom = jnp.sum(p, -1, keepdims=True)
            yield
            heads.append((_dot(p.astype(BF16), vx_ref[0, :, cols]) / denom).astype(BF16))
        xa = _dot(jnp.concatenate(heads, -1), wxo_ref[...])
        yield
        out_ref[0, rows, :] = _ln(DEEPNORM_ALPHA * h1 + xa, g2_ref[...], b2_ref[...])

    _round_robin([sub_tile(k) for k in range(N_SUB)])


def _mix_xattn(oa, os_, ls, h, expand, g_dil, w_out, g1, b1, w_xq, kx, vx, w_xo, g2, b2, tm):
    b, s, d = h.shape
    m = kx.shape[1]
    tok = lambda w: pl.BlockSpec((1, tm, w), lambda bi, i: (bi, i, 0))
    res = lambda dil: pl.BlockSpec((1, tm // dil, dil * DIL_WIDTH), lambda bi, i: (bi, i, 0))
    memspec = pl.BlockSpec((1, m, d), lambda bi, i: (bi, 0, 0))
    return pl.pallas_call(
        functools.partial(_mix_kernel, tm),
        grid=(b, s // tm),
        in_specs=[tok(WIN_WIDTH)] + [res(dil) for _, dil in DIL_PAIRS] +
                 [tok(DIL_SLOTS) if dil == 1 else
                  pl.BlockSpec((1, dil, tm // dil, LANES), lambda bi, i: (bi, 0, i, 0)) for _, dil in DIL_PAIRS] +
                 [tok(d),
                  _const_spec(expand.shape), _const_spec((1, DIL_WIDTH)), _const_spec((d, d)),
                  _const_spec((1, d)), _const_spec((1, d)), _const_spec((d, d)), memspec, memspec,
                  _const_spec((d, d)), _const_spec((1, d)), _const_spec((1, d))],
        out_specs=tok(d),
        out_shape=jax.ShapeDtypeStruct((b, s, d), F32),
        scratch_shapes=[pltpu.VMEM((N_DIL - 1, DIL_WIDTH // LANES, tm, LANES), F32),
                        pltpu.VMEM((N_DIL - 1, tm, LANES), F32)],
        compiler_params=_params(("parallel", "parallel")),
        name="mix_xattn",
    )(oa, *os_, *ls, h, expand, g_dil, w_out, g1, b1, w_xq, kx, vx, w_xo, g2, b2)


FF_CHUNKS = ((0, 1536), (1536, 1280))
HALO = 16
CONV_PHASES = 4


def _ffn_kernel(tm, hp_ref, hm_ref, hn_ref, wg_ref, wu_ref, cw_ref, cb_ref, wd_ref, g3_ref, b3_ref, out_ref,
                gs, us, ys):
    i = pl.program_id(1)
    n = pl.num_programs(1)
    hm = hm_ref[0]
    hb = hm.astype(BF16)
    hp = jnp.where(i > 0, hp_ref[0], 0.0).astype(BF16)
    hn = jnp.where(i < n - 1, hn_ref[0], 0.0).astype(BF16)
    hcat = jnp.concatenate([hp, hb, hn], 0)
    q = tm // CONV_PHASES
    acc = None
    for c0, width in FF_CHUNKS:
        gate = _dot(hcat, wg_ref[:, c0:c0 + width])
        up = _dot(hb, wu_ref[:, c0:c0 + width])
        n_slabs = width // LANES
        for sl in range(n_slabs):
            gs[sl] = gate[:, sl * LANES:(sl + 1) * LANES]
            us[sl] = up[:, sl * LANES:(sl + 1) * LANES]
        phases = []
        for ph in range(CONV_PHASES):
            slabs = []
            for sl in range(n_slabs):
                lanes = slice(c0 + sl * LANES, c0 + (sl + 1) * LANES)
                g = cb_ref[:, lanes]
                for j in range(CONV_WIDTH):
                    g = g + gs[sl, pl.ds(HALO - 1 + j + ph, q, stride=CONV_PHASES), :] * cw_ref[j:j + 1, lanes]
                u = us[sl, pl.ds(ph, q, stride=CONV_PHASES), :]
                slabs.append((0.5 * g * (1.0 + lax.erf(g * np.float32(np.sqrt(0.5)))) * u).astype(BF16))
            phases.append(jnp.concatenate(slabs, -1))
        part = _dot(jnp.concatenate(phases, 0), wd_ref[c0:c0 + width, :])
        acc = part if acc is None else acc + part
    for ph in range(CONV_PHASES):
        for sl in range(D_MODEL // LANES):
            ys[sl, pl.ds(ph, q, stride=CONV_PHASES), :] = acc[ph * q:(ph + 1) * q, sl * LANES:(sl + 1) * LANES]
    ff = jnp.concatenate([ys[sl] for sl in range(D_MODEL // LANES)], -1)
    out_ref[0] = _ln(DEEPNORM_ALPHA * hm + ff, g3_ref[...], b3_ref[...])


def _conv_glu(h2, wg, wu, cw, cb, wd, g3, b3, tm):
    b, s, d = h2.shape
    per = tm // HALO
    nblk = s // HALO
    main = pl.BlockSpec((1, tm, d), lambda bi, i: (bi, i, 0))
    prev = pl.BlockSpec((1, HALO, d), lambda bi, i: (bi, jnp.maximum(i * per - 1, 0), 0))
    nxt = pl.BlockSpec((1, HALO, d), lambda bi, i: (bi, jnp.minimum((i + 1) * per, nblk - 1), 0))
    n_slabs = max(w for _, w in FF_CHUNKS) // LANES
    return pl.pallas_call(
        functools.partial(_ffn_kernel, tm),
        grid=(b, s // tm),
        in_specs=[prev, main, nxt, _const_spec((d, D_FF)), _const_spec((d, D_FF)),
                  _const_spec((CONV_WIDTH, D_FF)), _const_spec((1, D_FF)), _const_spec((D_FF, d)),
                  _const_spec((1, d)), _const_spec((1, d))],
        out_specs=main,
        out_shape=jax.ShapeDtypeStruct((b, s, d), F32),
        scratch_shapes=[pltpu.VMEM((n_slabs, tm + 2 * HALO, LANES), F32), pltpu.VMEM((n_slabs, tm, LANES), F32),
                        pltpu.VMEM((d // LANES, tm, LANES), F32)],
        compiler_params=_params(("parallel", "parallel")),
        name="conv_glu",
    )(h2, h2, h2, wg, wu, cw, cb, wd, g3, b3)


def kernel(x, mem, positions, ln_in_g, ln_in_b, w_in, attn_sink, g_win, g_dil, w_mix_out, ln1_g, ln1_b, mem_ln_g, mem_ln_b, w_xq, w_xk, w_xv, w_xo, ln2_g, ln2_b, w_gate, w_up, conv_w, conv_b, w_down, ln3_g, ln3_b):
    b, s, d = x.shape
    assert DEPTH == 1
    l = 0
    row = lambda v: v.reshape(1, -1)
    cos, sin = _rope_tables(positions)

    qscale = HEAD_DIM ** -0.5 * LOG2E
    o_ka, o_va, o_qb = A_Q, A_Q + A_KV, A_Q + 2 * A_KV
    o_kb, o_vb = o_qb + B_QKV, o_qb + 2 * B_QKV
    names = ["qa", "ka", "va"]
    segs = [(0, A_Q, True, 1, qscale, False),
            (o_ka, A_KV, True, 1, 1.0, True),
            (o_va, A_KV, False, 1, 1.0, True)]
    for kind, base, rope, scale in (("q", o_qb, True, qscale), ("k", o_kb, True, 1.0), ("v", o_vb, False, 1.0)):
        for gi, (_, dil) in enumerate(DIL_PAIRS):
            names.append(f"{kind}{gi}")
            segs.append((base + gi * DIL_WIDTH, DIL_WIDTH, rope, dil, scale, False))

    xscale = X_HEAD_DIM ** -0.5
    to_cast = [(w_mix_out[l], 1.0), (w_xq[l], xscale), (w_xk[l], 1.0), (w_xv[l], 1.0), (w_xo[l], 1.0),
               (w_gate[l], 1.0), (w_up[l], 1.0), (w_down[l], 1.0)]
    outs = _qkv_proj(x.reshape(b * s, d), row(ln_in_g), row(ln_in_b), w_in[l].astype(BF16), tuple(segs),
                     cos, sin, to_cast, tm=QKV_TILE)
    n_seg = 1 + len(segs)
    wo_b, wq_b, wk_b, wv_b, wxo_b, wg_b, wu_b, wd_b = outs[n_seg:]
    hn = outs[0].reshape(b, s, d)
    proj = {name: t.reshape(b, t.shape[0] // b, t.shape[1]) for name, t in zip(names, outs[1:n_seg])}
    qa, ka, va = proj["qa"], proj["ka"], proj["va"]
    oa = _win_attn(qa, ka, va, attn_sink[l], row(g_win[l]), tq=WIN_TILE)
    os_, ls = [], []
    for gi, (_, dil) in enumerate(DIL_PAIRS):
        qg, kg, vg = proj[f"q{gi}"], proj[f"k{gi}"], proj[f"v{gi}"]
        o, lse = _dil_attn(qg, kg, vg, dil, tq=DIL_TOKENS // dil)
        os_.append(o)
        ls.append(lse.reshape(b, s, DIL_SLOTS) if dil == 1 else lse)

    kx, vx = _mem_kv(mem, row(mem_ln_g[l]), row(mem_ln_b[l]), wk_b, wv_b)
    expand = jnp.repeat(jnp.eye(DIL_SLOTS, dtype=BF16), HEAD_DIM, axis=1)
    h2 = _mix_xattn(oa, os_, ls, hn, expand, row(g_dil[l]), wo_b, row(ln1_g[l]), row(ln1_b[l]), wq_b, kx, vx,
                    wxo_b, row(ln2_g[l]), row(ln2_b[l]), tm=MIX_TILE)
    return _conv_glu(h2, wg_b, wu_b, conv_w[l], row(conv_b[l]), wd_b, row(ln3_g[l]), row(ln3_b[l]), tm=FFN_TILE)
```

```python
import functools

import numpy as np
import jax
import jax.numpy as jnp
from jax import lax
from jax.experimental import pallas as pl
from jax.experimental.pallas import tpu as pltpu

D_MODEL = 1024
HEAD_DIM = 64
WIN_Q_HEADS = 8
WIN_KV_HEADS = 2
WIN_HALF = 128
DIL_SLOTS = 8
DIL_PAIRS = ((128, 1), (512, 4), (2048, 16))
N_DIL = len(DIL_PAIRS)
ROT_DIM = HEAD_DIM // 4
ROT_HALF = ROT_DIM // 2
ROPE_THETA = 500000.0
X_HEADS = 4
X_HEAD_DIM = D_MODEL // X_HEADS
D_FF = 2816
CONV_WIDTH = 3
WIN_WIDTH = WIN_Q_HEADS * HEAD_DIM
DIL_WIDTH = DIL_SLOTS * HEAD_DIM
A_Q = WIN_WIDTH
A_KV = WIN_KV_HEADS * HEAD_DIM
B_QKV = N_DIL * DIL_WIDTH
DEPTH = 1
DEEPNORM_ALPHA = (2 * DEPTH) ** 0.25
LN_EPS = 1e-5
NEG_INF = -1e30
LOG2E = float(np.log2(np.e))

LANES = 128
TOK_PER_ROW = LANES // ROT_HALF
Q_BLK = 128
N_SUB = 4
DIL_SIDE = 64
VMEM_LIMIT = 56 * 1024 * 1024
QKV_TILE = 512
WIN_TILE = 2048
DIL_TOKENS = 2048
MIX_TILE = 1024
FFN_TILE = 512

F32 = jnp.float32
BF16 = jnp.bfloat16


def _ln(x, g, b):
    mu = jnp.mean(x, -1, keepdims=True)
    xc = x - mu
    var = jnp.mean(xc * xc, -1, keepdims=True)
    return xc * lax.rsqrt(var + LN_EPS) * g + b


def _rms(x, g):
    return x * lax.rsqrt(jnp.mean(x * x, -1, keepdims=True) + LN_EPS) * g


def _dot(a, b):
    return jnp.dot(a, b, preferred_element_type=F32)


def _dot_nt(a, b):
    return lax.dot_general(a, b, (((1,), (1,)), ((), ())), preferred_element_type=F32)


def _const_spec(shape):
    nd = len(shape)
    return pl.BlockSpec(shape, lambda *_: (0,) * nd, pipeline_mode=pl.Buffered(1))


def _params(sem):
    return pltpu.CompilerParams(dimension_semantics=sem, vmem_limit_bytes=VMEM_LIMIT)


def _round_robin(gens):
    live = list(gens)
    while live:
        for g in list(live):
            try:
                next(g)
            except StopIteration:
                live.remove(g)


def _rope_tab_kernel(pos_ref, invf_ref, cos_ref, sin_ref):
    ang = pos_ref[...].astype(F32) * invf_ref[...]
    cos_ref[...] = jnp.cos(ang)
    sin_ref[...] = jnp.sin(ang)


def _rope_tables(positions):
    inv_freq = ROPE_THETA ** (-jnp.arange(0, ROT_DIM, 2, dtype=F32) / ROT_DIM)
    pos_rep = jnp.repeat(positions.reshape(-1), ROT_HALF).reshape(-1, LANES)
    invf = jnp.tile(inv_freq, TOK_PER_ROW).reshape(1, LANES)
    return pl.pallas_call(
        _rope_tab_kernel,
        out_shape=(jax.ShapeDtypeStruct(pos_rep.shape, F32),) * 2,
        name="rope_tables",
    )(pos_rep, invf)


def _expand_rope(tab, tm):
    rows = jnp.broadcast_to(tab[:, None, :], (tm // TOK_PER_ROW, TOK_PER_ROW, LANES)).reshape(tm, LANES)
    tok = lax.broadcasted_iota(jnp.int32, (tm, LANES), 0) % TOK_PER_ROW
    lane = lax.broadcasted_iota(jnp.int32, (tm, LANES), 1)
    idx = tok * ROT_HALF + lane % ROT_HALF
    return jnp.take_along_axis(rows, idx, axis=1, mode="promise_in_bounds")


QKV_DOT_WIDTH = 1536


def _run_width(segs, c0):
    starts = {s[0]: s[1] for s in segs}
    width = 0
    while c0 + width in starts and width + starts[c0 + width] <= QKV_DOT_WIDTH:
        width += starts[c0 + width]
    return width


def _qkv_kernel(segs, tm, scales, x_ref, g_ref, b_ref, w_ref, cos_ref, sin_ref, *rest):
    n_cast = len(scales)
    cast_in, rest = rest[:n_cast], rest[n_cast:]
    h_ref, out_refs, cast_out, stage = rest[0], rest[1:1 + len(segs)], rest[1 + len(segs):-1], rest[-1]
    h = _ln(x_ref[...], g_ref[...], b_ref[...])
    h_ref[...] = h
    hb = h.astype(BF16)
    for scale, wi_ref, wo_ref in zip(scales, cast_in, cast_out):
        wo_ref[...] = (wi_ref[...] * scale if scale != 1.0 else wi_ref[...]).astype(BF16)
    slot = lax.broadcasted_iota(jnp.int32, (tm, LANES), 1) % HEAD_DIM
    cos = _expand_rope(cos_ref[...], tm)
    sin = _expand_rope(sin_ref[...], tm)
    is_t1 = slot < ROT_HALF
    c_tab = jnp.where(slot < ROT_DIM, cos, 1.0)
    s_tab = jnp.where(is_t1, -sin, jnp.where(slot < ROT_DIM, sin, 0.0))
    tables = {scale: (c_tab, s_tab) if scale == 1.0 else (c_tab * scale, s_tab * scale)
              for scale in sorted({seg[4] for seg in segs if seg[2]})}
    lo = slot == lax.broadcasted_iota(jnp.int32, (tm, LANES), 1)
    z, z0 = None, 0
    for (c0, width, rope, dil, scale, dup), o_ref in zip(segs, out_refs):
        if z is None or c0 + width > z0 + z.shape[1]:
            z0 = c0
            z = _dot(hb, w_ref[:, z0:z0 + _run_width(segs, c0)])
        for j in range(width // LANES):
            zc = z[:, c0 - z0 + j * LANES:c0 - z0 + (j + 1) * LANES]
            if rope:
                up = pltpu.roll(zc, LANES - ROT_HALF, 1)
                dn = pltpu.roll(zc, ROT_HALF, 1)
                c_s, s_s = tables[scale]
                zc = zc * c_s + jnp.where(is_t1, up, dn) * s_s
            elif scale != 1.0:
                zc = zc * scale
            if dup:
                sw = pltpu.roll(zc, HEAD_DIM, 1)
                o_ref[:, 2 * j * LANES:(2 * j + 1) * LANES] = jnp.where(lo, zc, sw).astype(BF16)
                o_ref[:, (2 * j + 1) * LANES:(2 * j + 2) * LANES] = jnp.where(lo, sw, zc).astype(BF16)
            elif dil == 1:
                o_ref[:, j * LANES:(j + 1) * LANES] = zc.astype(BF16)
            else:
                stage[...] = zc
                for c in range(dil):
                    col = c * width + j * LANES
                    o_ref[:, col:col + LANES] = stage[pl.ds(c, tm // dil, stride=dil), :].astype(BF16)


BF16_ROWS = 16


def _qkv_proj(x2, ln_g, ln_b, w_all, segs, cos, sin, to_cast, tm):
    t = x2.shape[0]
    steps = t // tm
    wtot = w_all.shape[1]
    row = lambda r, w: pl.BlockSpec((r, w), lambda i: (i, 0))
    out_shape = [jax.ShapeDtypeStruct((t, D_MODEL), F32)]
    out_specs = [row(tm, D_MODEL)]
    for _, width, _, dil, _, dup in segs:
        wout = 2 * width if dup else dil * width
        out_shape.append(jax.ShapeDtypeStruct((t // dil, wout), BF16))
        out_specs.append(row(tm // dil, wout))
    cast_specs = []
    for w, _ in to_cast:
        rows, cols = w.shape
        nblk = max(n for n in range(1, steps + 1) if steps % n == 0 and rows % (n * BF16_ROWS) == 0)
        spec = pl.BlockSpec((rows // nblk, cols), lambda i, every=steps // nblk: (i // every, 0))
        cast_specs.append(spec)
        out_shape.append(jax.ShapeDtypeStruct((rows, cols), BF16))
        out_specs.append(spec)
    return pl.pallas_call(
        functools.partial(_qkv_kernel, segs, tm, tuple(s for _, s in to_cast)),
        grid=(steps,),
        in_specs=[row(tm, D_MODEL), _const_spec((1, D_MODEL)), _const_spec((1, D_MODEL)),
                  _const_spec((D_MODEL, wtot)), row(tm // TOK_PER_ROW, LANES), row(tm // TOK_PER_ROW, LANES)]
                 + cast_specs,
        out_specs=out_specs,
        out_shape=out_shape,
        scratch_shapes=[pltpu.VMEM((tm, LANES), F32)],
        compiler_params=_params(("arbitrary",)),
        name="qkv_proj",
    )(x2, ln_g, ln_b, w_all, cos, sin, *[w for w, _ in to_cast])


def _band_bias(nq, nk, side):
    qi = lax.broadcasted_iota(jnp.int32, (nq, nk), 0)
    kj = lax.broadcasted_iota(jnp.int32, (nq, nk), 1)
    rel = kj - side - qi
    return jnp.where((rel >= -side) & (rel <= side), 0.0, NEG_INF).astype(F32)


def _halo_rows(parts, lo, hi, cols):
    prev, main, nxt = parts
    side, tq = prev.shape[0], main.shape[0]
    pieces = []
    if lo < 0:
        pieces.append(prev[side + lo:side, cols])
    pieces.append(main[max(lo, 0):min(hi, tq), cols])
    if hi > tq:
        pieces.append(nxt[0:hi - tq, cols])
    return pieces[0] if len(pieces) == 1 else jnp.concatenate(pieces, 0)


def _band_units(side, dil, tq, tile, n_tiles, sink_ref, q_ref, k_parts, v_parts, emit, lse_w=0):
    is_win = sink_ref is not None
    blk = Q_BLK
    nk = blk + 2 * side
    n_blk = tq // blk
    width = q_ref.shape[-1] // dil
    n_pairs = width // LANES
    lo = lax.broadcasted_iota(jnp.int32, (blk, LANES), 1) < HEAD_DIM
    lane8 = lax.broadcasted_iota(jnp.int32, (blk, max(lse_w, 1)), 1)
    band = _band_bias(blk, nk, side)
    kj = lax.broadcasted_iota(jnp.int32, (blk, nk), 1)
    band_first = jnp.where((tile == 0) & (kj < side), NEG_INF, band)
    band_last = jnp.where((tile == n_tiles - 1) & (kj >= side + blk), NEG_INF, band)
    biases = []
    for j in range(n_blk):
        bias = band
        if j == 0:
            bias = band_first
        if j == n_blk - 1:
            bias = band_last if j > 0 else jnp.minimum(band_first, band_last)
        biases.append(jnp.concatenate([bias, bias], 0))

    units = [(j, c, pr) for j in range(n_blk) for c in range(dil) for pr in range(n_pairs)]

    def kv_cols(c, pr):
        if is_win:
            return slice((pr // 2) * LANES, (pr // 2 + 1) * LANES)
        return slice(c * width + pr * LANES, c * width + (pr + 1) * LANES)

    def scores(j, c, pr):
        r0 = j * blk
        q2 = q_ref[r0:r0 + blk, c * width + pr * LANES:c * width + (pr + 1) * LANES]
        zero = jnp.zeros_like(q2)
        qs = jnp.concatenate([jnp.where(lo, q2, zero), jnp.where(lo, zero, q2)], 0)
        keys = _halo_rows(k_parts, r0 - side, r0 + blk + side, kv_cols(c, pr))
        return _dot_nt(qs, keys) + biases[j]

    def softmax(s, pr):
        ps, rdens, lses = [], [], []
        for hf in range(2):
            sh = s[hf * blk:(hf + 1) * blk]
            m = jnp.max(sh, -1, keepdims=True)
            if is_win:
                sink = sink_ref[2 * pr + hf] * LOG2E
                m = jnp.maximum(m, sink)
            ph = jnp.exp2(sh - m)
            denom = jnp.sum(ph, -1, keepdims=True)
            if is_win:
                denom = denom + jnp.exp2(sink - m)
            else:
                lses.append(m + jnp.log2(denom))
            ps.append(ph.astype(BF16))
            rdens.append(1.0 / denom)
        return jnp.concatenate(ps, 0), rdens, lses

    pairs, lse8 = [], None

    def finish(unit, o, rdens, lses):
        nonlocal pairs, lse8
        j, c, pr = unit
        pairs.append(jnp.where(lo, o[:blk] * rdens[0], o[blk:] * rdens[1]))
        if lse_w:
            if lse8 is None:
                lse8 = jnp.zeros((blk, lse_w), F32)
            lse8 = jnp.where(lane8 == 2 * pr, lses[0], lse8)
            lse8 = jnp.where(lane8 == 2 * pr + 1, lses[1], lse8)
        if pr == n_pairs - 1:
            emit(j, c, jnp.concatenate(pairs, -1), lse8)
            pairs, lse8 = [], None

    s_next = scores(*units[0])
    pending = None
    for n, (j, c, pr) in enumerate(units):
        s = s_next
        if n + 1 < len(units):
            s_next = scores(*units[n + 1])
        p, rdens, lses = softmax(s, pr)
        o = _dot(p, _halo_rows(v_parts, j * blk - side, (j + 1) * blk + side, kv_cols(c, pr)))
        if pending is not None:
            finish(*pending)
        pending = ((j, c, pr), o, rdens, lses)
    finish(*pending)


def _dil_kernel(dil, tq, q_ref, kp_ref, km_ref, kn_ref, vp_ref, vm_ref, vn_ref, o_ref, lse_ref):
    width = q_ref.shape[-1] // dil

    def emit(j, c, out, lse):
        rows = slice(j * Q_BLK, (j + 1) * Q_BLK)
        o_ref[0, rows, c * width:(c + 1) * width] = out.astype(BF16)
        lse_ref[0, c, rows, :] = lse

    _band_units(DIL_SIDE, dil, tq, pl.program_id(1), pl.num_programs(1), None, q_ref.at[0],
                (kp_ref.at[0], km_ref.at[0], kn_ref.at[0]), (vp_ref.at[0], vm_ref.at[0], vn_ref.at[0]), emit,
                lse_w=lse_ref.shape[-1])


def _band_specs(tq, side, n_rows, width):
    per = tq // side
    nblk = n_rows // side
    main = pl.BlockSpec((1, tq, width), lambda bi, i: (bi, i, 0))
    prev = pl.BlockSpec((1, side, width), lambda bi, i: (bi, jnp.maximum(i * per - 1, 0), 0))
    nxt = pl.BlockSpec((1, side, width), lambda bi, i: (bi, jnp.minimum((i + 1) * per, nblk - 1), 0))
    return prev, main, nxt


def _win_kernel(tq, sink_ref, q_ref, kp_ref, km_ref, kn_ref, vp_ref, vm_ref, vn_ref, g_ref, o_ref):

    def emit(j, c, out, lse):
        o_ref[0, j * Q_BLK:(j + 1) * Q_BLK, :] = _rms(out, g_ref[...]).astype(BF16)

    _band_units(WIN_HALF, 1, tq, pl.program_id(1), pl.num_programs(1), sink_ref, q_ref.at[0],
                (kp_ref.at[0], km_ref.at[0], kn_ref.at[0]), (vp_ref.at[0], vm_ref.at[0], vn_ref.at[0]), emit)


def _win_attn(qa, ka, va, sink, g_win, tq):
    b, s, _ = qa.shape
    side = WIN_HALF
    kw = ka.shape[-1]
    prev, main, nxt = _band_specs(tq, side, s, kw)
    qspec = pl.BlockSpec((1, tq, WIN_WIDTH), lambda bi, i: (bi, i, 0))
    return pl.pallas_call(
        functools.partial(_win_kernel, tq),
        grid=(b, s // tq),
        in_specs=[pl.BlockSpec(memory_space=pltpu.SMEM), qspec, prev, main, nxt, prev, main, nxt,
                  _const_spec((1, WIN_WIDTH))],
        out_specs=qspec,
        out_shape=jax.ShapeDtypeStruct((b, s, WIN_WIDTH), BF16),
        compiler_params=_params(("parallel", "parallel")),
        name="win_attn",
    )(sink, qa, ka, ka, ka, va, va, va, g_win)


def _dil_attn(qv, kv, vv, dil, tq):
    b, ln, wr = qv.shape
    side = DIL_SIDE
    lse_w = DIL_SLOTS if dil == 1 else LANES
    prev, main, nxt = _band_specs(tq, side, ln, wr)
    return pl.pallas_call(
        functools.partial(_dil_kernel, dil, tq),
        grid=(b, ln // tq),
        in_specs=[main, prev, main, nxt, prev, main, nxt],
        out_specs=[main, pl.BlockSpec((1, dil, tq, lse_w), lambda bi, i: (bi, 0, i, 0))],
        out_shape=[jax.ShapeDtypeStruct((b, ln, wr), BF16),
                   jax.ShapeDtypeStruct((b, dil, ln, lse_w), F32)],
        compiler_params=_params(("parallel", "parallel")),
        name=f"dil_attn_{dil}",
    )(qv, kv, kv, kv, vv, vv, vv)


def _mem_kernel(mem_ref, g_ref, b_ref, wk_ref, wv_ref, k_ref, v_ref):
    mn = _ln(mem_ref[0], g_ref[...], b_ref[...]).astype(BF16)
    k_ref[0] = _dot(mn, wk_ref[...]).astype(BF16)
    v_ref[0] = _dot(mn, wv_ref[...]).astype(BF16)


def _mem_kv(mem, g, b, wk, wv):
    bsz, m, d = mem.shape
    blk = pl.BlockSpec((1, m, d), lambda bi: (bi, 0, 0))
    return pl.pallas_call(
        _mem_kernel,
        grid=(bsz,),
        in_specs=[blk, _const_spec((1, d)), _const_spec((1, d)), _const_spec((d, d)), _const_spec((d, d))],
        out_specs=[blk, blk],
        out_shape=[jax.ShapeDtypeStruct((bsz, m, d), BF16)] * 2,
        compiler_params=_params(("parallel",)),
        name="mem_kv",
    )(mem, g, b, wk, wv)


def _mix_kernel(tm, oa_ref, o0_ref, o1_ref, o2_ref, l0_ref, l1_ref, l2_ref, h_ref, e_ref, gd_ref, wo_ref,
                g1_ref, b1_ref, wq_ref, kx_ref, vx_ref, wxo_ref, g2_ref, b2_ref, out_ref, stage, lstage):
    ts = tm // N_SUB
    n_chunks = DIL_WIDTH // LANES

    def sub_tile(k):
        rows = slice(k * ts, (k + 1) * ts)
        ls = []
        for gi, (l_ref, (_, dil)) in enumerate(zip((l0_ref, l1_ref, l2_ref), DIL_PAIRS)):
            if dil == 1:
                ls.append(l_ref[0, rows, :])
            else:
                rrows = slice(k * ts // dil, (k + 1) * ts // dil)
                for c in range(dil):
                    lstage[gi - 1, pl.ds(k * ts + c, ts // dil, stride=dil), :] = l_ref[0, c, rrows, :]
                ls.append(lstage[gi - 1, rows, 0:DIL_SLOTS])
        mx = jnp.maximum(jnp.maximum(ls[0], ls[1]), ls[2])
        es = [jnp.exp2(l - mx) for l in ls]
        tot = es[0] + es[1] + es[2]
        ob = None
        for gi, (e, o_ref, (_, dil)) in enumerate(zip(es, (o0_ref, o1_ref, o2_ref), DIL_PAIRS)):
            wexp = _dot((e / tot).astype(BF16), e_ref[...])
            if dil == 1:
                og = o_ref[0, rows, :].astype(F32)
            else:
                rrows = slice(k * ts // dil, (k + 1) * ts // dil)
                for c in range(dil):
                    for j in range(n_chunks):
                        col = c * DIL_WIDTH + j * LANES
                        stage[gi - 1, j, pl.ds(k * ts + c, ts // dil, stride=dil), :] = (
                            o_ref[0, rrows, col:col + LANES].astype(F32))
                og = jnp.concatenate([stage[gi - 1, j, rows, :] for j in range(n_chunks)], -1)
            term = wexp * og
            ob = term if ob is None else ob + term
        obn = _rms(ob, gd_ref[...]).astype(BF16)
        yield
        mix = _dot(oa_ref[0, rows, :], wo_ref[0:WIN_WIDTH, :]) + _dot(obn, wo_ref[WIN_WIDTH:, :])
        yield
        h1 = _ln(DEEPNORM_ALPHA * h_ref[0, rows, :] + mix, g1_ref[...], b1_ref[...])
        yield
        q = _dot(h1.astype(BF16), wq_ref[...]).astype(BF16)
        yield
        heads = []
        for hh in range(X_HEADS):
            cols = slice(hh * X_HEAD_DIM, (hh + 1) * X_HEAD_DIM)
            s = _dot_nt(q[:, cols], kx_ref[0, :, cols])
            yield
            m = jnp.max(s, -1, keepdims=True)
            p = jnp.exp(s - m)
            denom = jnp.sum(p, -1, keepdims=True)
            yield
            heads.append((_dot(p.astype(BF16), vx_ref[0, :, cols]) / denom).astype(BF16))
        xa = _dot(jnp.concatenate(heads, -1), wxo_ref[...])
        yield
        out_ref[0, rows, :] = _ln(DEEPNORM_ALPHA * h1 + xa, g2_ref[...], b2_ref[...])

    _round_robin([sub_tile(k) for k in range(N_SUB)])


def _mix_xattn(oa, os_, ls, h, expand, g_dil, w_out, g1, b1, w_xq, kx, vx, w_xo, g2, b2, tm):
    b, s, d = h.shape
    m = kx.shape[1]
    tok = lambda w: pl.BlockSpec((1, tm, w), lambda bi, i: (bi, i, 0))
    res = lambda dil: pl.BlockSpec((1, tm // dil, dil * DIL_WIDTH), lambda bi, i: (bi, i, 0))
    memspec = pl.BlockSpec((1, m, d), lambda bi, i: (bi, 0, 0))
    return pl.pallas_call(
        functools.partial(_mix_kernel, tm),
        grid=(b, s // tm),
        in_specs=[tok(WIN_WIDTH)] + [res(dil) for _, dil in DIL_PAIRS] +
                 [tok(DIL_SLOTS) if dil == 1 else
                  pl.BlockSpec((1, dil, tm // dil, LANES), lambda bi, i: (bi, 0, i, 0)) for _, dil in DIL_PAIRS] +
                 [tok(d),
                  _const_spec(expand.shape), _const_spec((1, DIL_WIDTH)), _const_spec((d, d)),
                  _const_spec((1, d)), _const_spec((1, d)), _const_spec((d, d)), memspec, memspec,
                  _const_spec((d, d)), _const_spec((1, d)), _const_spec((1, d))],
        out_specs=tok(d),
        out_shape=jax.ShapeDtypeStruct((b, s, d), F32),
        scratch_shapes=[pltpu.VMEM((N_DIL - 1, DIL_WIDTH // LANES, tm, LANES), F32),
                        pltpu.VMEM((N_DIL - 1, tm, LANES), F32)],
        compiler_params=_params(("parallel", "parallel")),
        name="mix_xattn",
    )(oa, *os_, *ls, h, expand, g_dil, w_out, g1, b1, w_xq, kx, vx, w_xo, g2, b2)


FF_CHUNKS = ((0, 1536), (1536, 1280))
HALO = 16
CONV_PHASES = 4


def _ffn_kernel(tm, hp_ref, hm_ref, hn_ref, wg_ref, wu_ref, cw_ref, cb_ref, wd_ref, g3_ref, b3_ref, out_ref,
                gs, us, ys):
    i = pl.program_id(1)
    n = pl.num_programs(1)
    hm = hm_ref[0]
    hb = hm.astype(BF16)
    hp = jnp.where(i > 0, hp_ref[0], 0.0).astype(BF16)
    hn = jnp.where(i < n - 1, hn_ref[0], 0.0).astype(BF16)
    hcat = jnp.concatenate([hp, hb, hn], 0)
    q = tm // CONV_PHASES
    acc = None
    for c0, width in FF_CHUNKS:
        gate = _dot(hcat, wg_ref[:, c0:c0 + width])
        up = _dot(hb, wu_ref[:, c0:c0 + width])
        n_slabs = width // LANES
        for sl in range(n_slabs):
            gs[sl] = gate[:, sl * LANES:(sl + 1) * LANES]
            us[sl] = up[:, sl * LANES:(sl + 1) * LANES]
        phases = []
        for ph in range(CONV_PHASES):
            slabs = []
            for sl in range(n_slabs):
                lanes = slice(c0 + sl * LANES, c0 + (sl + 1) * LANES)
                g = cb_ref[:, lanes]
                for j in range(CONV_WIDTH):
                    g = g + gs[sl, pl.ds(HALO - 1 + j + ph, q, stride=CONV_PHASES), :] * cw_ref[j:j + 1, lanes]
                u = us[sl, pl.ds(ph, q, stride=CONV_PHASES), :]
                slabs.append((0.5 * g * (1.0 + lax.erf(g * np.float32(np.sqrt(0.5)))) * u).astype(BF16))
            phases.append(jnp.concatenate(slabs, -1))
        part = _dot(jnp.concatenate(phases, 0), wd_ref[c0:c0 + width, :])
        acc = part if acc is None else acc + part
    for ph in range(CONV_PHASES):
        for sl in range(D_MODEL // LANES):
            ys[sl, pl.ds(ph, q, stride=CONV_PHASES), :] = acc[ph * q:(ph + 1) * q, sl * LANES:(sl + 1) * LANES]
    ff = jnp.concatenate([ys[sl] for sl in range(D_MODEL // LANES)], -1)
    out_ref[0] = _ln(DEEPNORM_ALPHA * hm + ff, g3_ref[...], b3_ref[...])


def _conv_glu(h2, wg, wu, cw, cb, wd, g3, b3, tm):
    b, s, d = h2.shape
    per = tm // HALO
    nblk = s // HALO
    main = pl.BlockSpec((1, tm, d), lambda bi, i: (bi, i, 0))
    prev = pl.BlockSpec((1, HALO, d), lambda bi, i: (bi, jnp.maximum(i * per - 1, 0), 0))
    nxt = pl.BlockSpec((1, HALO, d), lambda bi, i: (bi, jnp.minimum((i + 1) * per, nblk - 1), 0))
    n_slabs = max(w for _, w in FF_CHUNKS) // LANES
    return pl.pallas_call(
        functools.partial(_ffn_kernel, tm),
        grid=(b, s // tm),
        in_specs=[prev, main, nxt, _const_spec((d, D_FF)), _const_spec((d, D_FF)),
                  _const_spec((CONV_WIDTH, D_FF)), _const_spec((1, D_FF)), _const_spec((D_FF, d)),
                  _const_spec((1, d)), _const_spec((1, d))],
        out_specs=main,
        out_shape=jax.ShapeDtypeStruct((b, s, d), F32),
        scratch_shapes=[pltpu.VMEM((n_slabs, tm + 2 * HALO, LANES), F32), pltpu.VMEM((n_slabs, tm, LANES), F32),
                        pltpu.VMEM((d // LANES, tm, LANES), F32)],
        compiler_params=_params(("parallel", "parallel")),
        name="conv_glu",
    )(h2, h2, h2, wg, wu, cw, cb, wd, g3, b3)


def kernel(x, mem, positions, ln_in_g, ln_in_b, w_in, attn_sink, g_win, g_dil, w_mix_out, ln1_g, ln1_b, mem_ln_g, mem_ln_b, w_xq, w_xk, w_xv, w_xo, ln2_g, ln2_b, w_gate, w_up, conv_w, conv_b, w_down, ln3_g, ln3_b):
    b, s, d = x.shape
    assert DEPTH == 1
    l = 0
    row = lambda v: v.reshape(1, -1)
    cos, sin = _rope_tables(positions)

    qscale = HEAD_DIM ** -0.5 * LOG2E
    o_ka, o_va, o_qb = A_Q, A_Q + A_KV, A_Q + 2 * A_KV
    o_kb, o_vb = o_qb + B_QKV, o_qb + 2 * B_QKV
    names = ["qa", "ka", "va"]
    segs = [(0, A_Q, True, 1, qscale, False),
            (o_ka, A_KV, True, 1, 1.0, True),
            (o_va, A_KV, False, 1, 1.0, True)]
    for kind, base, rope, scale in (("q", o_qb, True, qscale), ("k", o_kb, True, 1.0), ("v", o_vb, False, 1.0)):
        for gi, (_, dil) in enumerate(DIL_PAIRS):
            names.append(f"{kind}{gi}")
            segs.append((base + gi * DIL_WIDTH, DIL_WIDTH, rope, dil, scale, False))

    xscale = X_HEAD_DIM ** -0.5
    to_cast = [(w_mix_out[l], 1.0), (w_xq[l], xscale), (w_xk[l], 1.0), (w_xv[l], 1.0), (w_xo[l], 1.0),
               (w_gate[l], 1.0), (w_up[l], 1.0), (w_down[l], 1.0)]
    outs = _qkv_proj(x.reshape(b * s, d), row(ln_in_g), row(ln_in_b), w_in[l].astype(BF16), tuple(segs),
                     cos, sin, to_cast, tm=QKV_TILE)
    n_seg = 1 + len(segs)
    wo_b, wq_b, wk_b, wv_b, wxo_b, wg_b, wu_b, wd_b = outs[n_seg:]
    hn = outs[0].reshape(b, s, d)
    proj = {name: t.reshape(b, t.shape[0] // b, t.shape[1]) for name, t in zip(names, outs[1:n_seg])}
    qa, ka, va = proj["qa"], proj["ka"], proj["va"]
    oa = _win_attn(qa, ka, va, attn_sink[l], row(g_win[l]), tq=WIN_TILE)
    os_, ls = [], []
    for gi, (_, dil) in enumerate(DIL_PAIRS):
        qg, kg, vg = proj[f"q{gi}"], proj[f"k{gi}"], proj[f"v{gi}"]
        o, lse = _dil_attn(qg, kg, vg, dil, tq=DIL_TOKENS // dil)
        os_.append(o)
        ls.append(lse.reshape(b, s, DIL_SLOTS) if dil == 1 else lse)

    kx, vx = _mem_kv(mem, row(mem_ln_g[l]), row(mem_ln_b[l]), wk_b, wv_b)
    expand = jnp.repeat(jnp.eye(DIL_SLOTS, dtype=BF16), HEAD_DIM, axis=1)
    h2 = _mix_xattn(oa, os_, ls, hn, expand, row(g_dil[l]), wo_b, row(ln1_g[l]), row(ln1_b[l]), wq_b, kx, vx,
                    wxo_b, row(ln2_g[l]), row(ln2_b[l]), tm=MIX_TILE)
    return _conv_glu(h2, wg_b, wu_b, conv_w[l], row(conv_b[l]), wd_b, row(ln3_g[l]), row(ln3_b[l]), tm=FFN_TILE)
```

```python
import functools

import numpy as np
import jax
import jax.numpy as jnp
from jax import lax
from jax.experimental import pallas as pl
from jax.experimental.pallas import tpu as pltpu

D_MODEL = 1024
HEAD_DIM = 64
WIN_Q_HEADS = 8
WIN_KV_HEADS = 2
WIN_HALF = 128
DIL_SLOTS = 8
DIL_PAIRS = ((128, 1), (512, 4), (2048, 16))
N_DIL = len(DIL_PAIRS)
ROT_DIM = HEAD_DIM // 4
ROT_HALF = ROT_DIM // 2
ROPE_THETA = 500000.0
X_HEADS = 4
X_HEAD_DIM = D_MODEL // X_HEADS
D_FF = 2816
CONV_WIDTH = 3
WIN_WIDTH = WIN_Q_HEADS * HEAD_DIM
DIL_WIDTH = DIL_SLOTS * HEAD_DIM
A_Q = WIN_WIDTH
A_KV = WIN_KV_HEADS * HEAD_DIM
B_QKV = N_DIL * DIL_WIDTH
DEPTH = 1
DEEPNORM_ALPHA = (2 * DEPTH) ** 0.25
LN_EPS = 1e-5
NEG_INF = -1e30
LOG2E = float(np.log2(np.e))

LANES = 128
TOK_PER_ROW = LANES // ROT_HALF
Q_BLK = 128
N_SUB = 4
DIL_SIDE = 64
assert all(window // (2 * dil) == DIL_SIDE for window, dil in DIL_PAIRS)
VMEM_LIMIT = 56 * 1024 * 1024
QKV_TILE = 512
WIN_TILE = 2048
DIL_TOKENS = 2048
MIX_TILE = 1024
FFN_TILE = 512

F32 = jnp.float32
BF16 = jnp.bfloat16


def _ln(x, g, b):
    mu = jnp.mean(x, -1, keepdims=True)
    xc = x - mu
    var = jnp.mean(xc * xc, -1, keepdims=True)
    return xc * lax.rsqrt(var + LN_EPS) * g + b


def _rms(x, g):
    return x * lax.rsqrt(jnp.mean(x * x, -1, keepdims=True) + LN_EPS) * g


def _dot(a, b):
    return jnp.dot(a, b, preferred_element_type=F32)


def _dot_nt(a, b):
    return lax.dot_general(a, b, (((1,), (1,)), ((), ())), preferred_element_type=F32)


def _const_spec(shape):
    nd = len(shape)
    return pl.BlockSpec(shape, lambda *_: (0,) * nd, pipeline_mode=pl.Buffered(1))


def _params(sem):
    return pltpu.CompilerParams(dimension_semantics=sem, vmem_limit_bytes=VMEM_LIMIT)


def _round_robin(gens):
    live = list(gens)
    while live:
        for g in list(live):
            try:
                next(g)
            except StopIteration:
                live.remove(g)


def _rope_tab_kernel(pos_ref, invf_ref, cos_ref, sin_ref):
    ang = pos_ref[...].astype(F32) * invf_ref[...]
    cos_ref[...] = jnp.cos(ang)
    sin_ref[...] = jnp.sin(ang)


def _rope_tables(positions):
    inv_freq = ROPE_THETA ** (-jnp.arange(0, ROT_DIM, 2, dtype=F32) / ROT_DIM)
    pos_rep = jnp.repeat(positions.reshape(-1), ROT_HALF).reshape(-1, LANES)
    invf = jnp.tile(inv_freq, TOK_PER_ROW).reshape(1, LANES)
    return pl.pallas_call(
        _rope_tab_kernel,
        out_shape=(jax.ShapeDtypeStruct(pos_rep.shape, F32),) * 2,
        name="rope_tables",
    )(pos_rep, invf)


def _expand_rope(tab, tm):
    rows = jnp.broadcast_to(tab[:, None, :], (tm // TOK_PER_ROW, TOK_PER_ROW, LANES)).reshape(tm, LANES)
    tok = lax.broadcasted_iota(jnp.int32, (tm, LANES), 0) % TOK_PER_ROW
    lane = lax.broadcasted_iota(jnp.int32, (tm, LANES), 1)
    idx = tok * ROT_HALF + lane % ROT_HALF
    return jnp.take_along_axis(rows, idx, axis=1, mode="promise_in_bounds")


QKV_DOT_WIDTH = 1536


def _run_width(segs, c0):
    starts = {s[0]: s[1] for s in segs}
    width = 0
    while c0 + width in starts and width + starts[c0 + width] <= QKV_DOT_WIDTH:
        width += starts[c0 + width]
    return width


def _qkv_kernel(segs, tm, scales, x_ref, g_ref, b_ref, w_ref, cos_ref, sin_ref, *rest):
    n_cast = len(scales)
    cast_in, rest = rest[:n_cast], rest[n_cast:]
    h_ref, out_refs, cast_out, stage = rest[0], rest[1:1 + len(segs)], rest[1 + len(segs):-1], rest[-1]
    h = _ln(x_ref[...], g_ref[...], b_ref[...])
    h_ref[...] = h
    hb = h.astype(BF16)
    for scale, wi_ref, wo_ref in zip(scales, cast_in, cast_out):
        wo_ref[...] = (wi_ref[...] * scale if scale != 1.0 else wi_ref[...]).astype(BF16)
    slot = lax.broadcasted_iota(jnp.int32, (tm, LANES), 1) % HEAD_DIM
    cos = _expand_rope(cos_ref[...], tm)
    sin = _expand_rope(sin_ref[...], tm)
    is_t1 = slot < ROT_HALF
    c_tab = jnp.where(slot < ROT_DIM, cos, 1.0)
    s_tab = jnp.where(is_t1, -sin, jnp.where(slot < ROT_DIM, sin, 0.0))
    tables = {scale: (c_tab, s_tab) if scale == 1.0 else (c_tab * scale, s_tab * scale)
              for scale in sorted({seg[4] for seg in segs if seg[2]})}
    lo = slot == lax.broadcasted_iota(jnp.int32, (tm, LANES), 1)
    z, z0 = None, 0
    for (c0, width, rope, dil, scale, dup), o_ref in zip(segs, out_refs):
        if z is None or c0 + width > z0 + z.shape[1]:
            z0 = c0
            z = _dot(hb, w_ref[:, z0:z0 + _run_width(segs, c0)])
        for j in range(width // LANES):
            zc = z[:, c0 - z0 + j * LANES:c0 - z0 + (j + 1) * LANES]
            if rope:
                up = pltpu.roll(zc, LANES - ROT_HALF, 1)
                dn = pltpu.roll(zc, ROT_HALF, 1)
                c_s, s_s = tables[scale]
                zc = zc * c_s + jnp.where(is_t1, up, dn) * s_s
            elif scale != 1.0:
                zc = zc * scale
            if dup:
                sw = pltpu.roll(zc, HEAD_DIM, 1)
                o_ref[:, 2 * j * LANES:(2 * j + 1) * LANES] = jnp.where(lo, zc, sw).astype(BF16)
                o_ref[:, (2 * j + 1) * LANES:(2 * j + 2) * LANES] = jnp.where(lo, sw, zc).astype(BF16)
            elif dil == 1:
                o_ref[:, j * LANES:(j + 1) * LANES] = zc.astype(BF16)
            else:
                stage[...] = zc
                for c in range(dil):
                    col = c * width + j * LANES
                    o_ref[:, col:col + LANES] = stage[pl.ds(c, tm // dil, stride=dil), :].astype(BF16)


BF16_ROWS = 16


def _qkv_proj(x2, ln_g, ln_b, w_all, segs, cos, sin, to_cast, tm):
    t = x2.shape[0]
    steps = t // tm
    wtot = w_all.shape[1]
    row = lambda r, w: pl.BlockSpec((r, w), lambda i: (i, 0))
    out_shape = [jax.ShapeDtypeStruct((t, D_MODEL), F32)]
    out_specs = [row(tm, D_MODEL)]
    for _, width, _, dil, _, dup in segs:
        wout = 2 * width if dup else dil * width
        out_shape.append(jax.ShapeDtypeStruct((t // dil, wout), BF16))
        out_specs.append(row(tm // dil, wout))
    cast_specs = []
    for w, _ in to_cast:
        rows, cols = w.shape
        nblk = max(n for n in range(1, steps + 1) if steps % n == 0 and rows % (n * BF16_ROWS) == 0)
        spec = pl.BlockSpec((rows // nblk, cols), lambda i, every=steps // nblk: (i // every, 0))
        cast_specs.append(spec)
        out_shape.append(jax.ShapeDtypeStruct((rows, cols), BF16))
        out_specs.append(spec)
    return pl.pallas_call(
        functools.partial(_qkv_kernel, segs, tm, tuple(s for _, s in to_cast)),
        grid=(steps,),
        in_specs=[row(tm, D_MODEL), _const_spec((1, D_MODEL)), _const_spec((1, D_MODEL)),
                  _const_spec((D_MODEL, wtot)), row(tm // TOK_PER_ROW, LANES), row(tm // TOK_PER_ROW, LANES)]
                 + cast_specs,
        out_specs=out_specs,
        out_shape=out_shape,
        scratch_shapes=[pltpu.VMEM((tm, LANES), F32)],
        compiler_params=_params(("arbitrary",)),
        name="qkv_proj",
    )(x2, ln_g, ln_b, w_all, cos, sin, *[w for w, _ in to_cast])


def _band_bias(nq, nk, side):
    qi = lax.broadcasted_iota(jnp.int32, (nq, nk), 0)
    kj = lax.broadcasted_iota(jnp.int32, (nq, nk), 1)
    rel = kj - side - qi
    return jnp.where((rel >= -side) & (rel <= side), 0.0, NEG_INF).astype(F32)


def _halo_rows(parts, lo, hi, cols):
    prev, main, nxt = parts
    side, tq = prev.shape[0], main.shape[0]
    pieces = []
    if lo < 0:
        pieces.append(prev[side + lo:side, cols])
    pieces.append(main[max(lo, 0):min(hi, tq), cols])
    if hi > tq:
        pieces.append(nxt[0:hi - tq, cols])
    return pieces[0] if len(pieces) == 1 else jnp.concatenate(pieces, 0)


def _band_units(side, dil, tq, tile, n_tiles, sink_ref, q_ref, k_parts, v_parts, emit, lse_w=0):
    is_win = sink_ref is not None
    blk = Q_BLK
    nk = blk + 2 * side
    n_blk = tq // blk
    width = q_ref.shape[-1] // dil
    n_pairs = width // LANES
    lo = lax.broadcasted_iota(jnp.int32, (blk, LANES), 1) < HEAD_DIM
    lane8 = lax.broadcasted_iota(jnp.int32, (blk, max(lse_w, 1)), 1)
    band = _band_bias(blk, nk, side)
    kj = lax.broadcasted_iota(jnp.int32, (blk, nk), 1)
    band_first = jnp.where((tile == 0) & (kj < side), NEG_INF, band)
    band_last = jnp.where((tile == n_tiles - 1) & (kj >= side + blk), NEG_INF, band)
    biases = []
    for j in range(n_blk):
        bias = band
        if j == 0:
            bias = band_first
        if j == n_blk - 1:
            bias = band_last if j > 0 else jnp.minimum(band_first, band_last)
        biases.append(jnp.concatenate([bias, bias], 0))

    units = [(j, c, pr) for j in range(n_blk) for c in range(dil) for pr in range(n_pairs)]

    def kv_cols(c, pr):
        if is_win:
            return slice((pr // 2) * LANES, (pr // 2 + 1) * LANES)
        return slice(c * width + pr * LANES, c * width + (pr + 1) * LANES)

    def scores(j, c, pr):
        r0 = j * blk
        q2 = q_ref[r0:r0 + blk, c * width + pr * LANES:c * width + (pr + 1) * LANES]
        zero = jnp.zeros_like(q2)
        qs = jnp.concatenate([jnp.where(lo, q2, zero), jnp.where(lo, zero, q2)], 0)
        keys = _halo_rows(k_parts, r0 - side, r0 + blk + side, kv_cols(c, pr))
        return _dot_nt(qs, keys) + biases[j]

    def softmax(s, pr):
        ps, rdens, lses = [], [], []
        for hf in range(2):
            sh = s[hf * blk:(hf + 1) * blk]
            m = jnp.max(sh, -1, keepdims=True)
            if is_win:
                sink = sink_ref[2 * pr + hf] * LOG2E
                m = jnp.maximum(m, sink)
            ph = jnp.exp2(sh - m)
            denom = jnp.sum(ph, -1, keepdims=True)
            if is_win:
                denom = denom + jnp.exp2(sink - m)
            else:
                lses.append(m + jnp.log2(denom))
            ps.append(ph.astype(BF16))
            rdens.append(1.0 / denom)
        return jnp.concatenate(ps, 0), rdens, lses

    pairs, lse8 = [], None

    def finish(unit, o, rdens, lses):
        nonlocal pairs, lse8
        j, c, pr = unit
        pairs.append(jnp.where(lo, o[:blk] * rdens[0], o[blk:] * rdens[1]))
        if lse_w:
            if lse8 is None:
                lse8 = jnp.zeros((blk, lse_w), F32)
            lse8 = jnp.where(lane8 == 2 * pr, lses[0], lse8)
            lse8 = jnp.where(lane8 == 2 * pr + 1, lses[1], lse8)
        if pr == n_pairs - 1:
            emit(j, c, jnp.concatenate(pairs, -1), lse8)
            pairs, lse8 = [], None

    s_next = scores(*units[0])
    pending = None
    for n, (j, c, pr) in enumerate(units):
        s = s_next
        if n + 1 < len(units):
            s_next = scores(*units[n + 1])
        p, rdens, lses = softmax(s, pr)
        o = _dot(p, _halo_rows(v_parts, j * blk - side, (j + 1) * blk + side, kv_cols(c, pr)))
        if pending is not None:
            finish(*pending)
        pending = ((j, c, pr), o, rdens, lses)
    finish(*pending)


def _dil_kernel(dil, tq, q_ref, kp_ref, km_ref, kn_ref, vp_ref, vm_ref, vn_ref, o_ref, lse_ref):
    width = q_ref.shape[-1] // dil

    def emit(j, c, out, lse):
        rows = slice(j * Q_BLK, (j + 1) * Q_BLK)
        o_ref[0, rows, c * width:(c + 1) * width] = out.astype(BF16)
        lse_ref[0, c, rows, :] = lse

    _band_units(DIL_SIDE, dil, tq, pl.program_id(1), pl.num_programs(1), None, q_ref.at[0],
                (kp_ref.at[0], km_ref.at[0], kn_ref.at[0]), (vp_ref.at[0], vm_ref.at[0], vn_ref.at[0]), emit,
                lse_w=lse_ref.shape[-1])


def _band_specs(tq, side, n_rows, width):
    per = tq // side
    nblk = n_rows // side
    main = pl.BlockSpec((1, tq, width), lambda bi, i: (bi, i, 0))
    prev = pl.BlockSpec((1, side, width), lambda bi, i: (bi, jnp.maximum(i * per - 1, 0), 0))
    nxt = pl.BlockSpec((1, side, width), lambda bi, i: (bi, jnp.minimum((i + 1) * per, nblk - 1), 0))
    return prev, main, nxt


def _win_kernel(tq, sink_ref, q_ref, kp_ref, km_ref, kn_ref, vp_ref, vm_ref, vn_ref, g_ref, o_ref):

    def emit(j, c, out, lse):
        o_ref[0, j * Q_BLK:(j + 1) * Q_BLK, :] = _rms(out, g_ref[...]).astype(BF16)

    _band_units(WIN_HALF, 1, tq, pl.program_id(1), pl.num_programs(1), sink_ref, q_ref.at[0],
                (kp_ref.at[0], km_ref.at[0], kn_ref.at[0]), (vp_ref.at[0], vm_ref.at[0], vn_ref.at[0]), emit)


def _win_attn(qa, ka, va, sink, g_win, tq):
    b, s, _ = qa.shape
    side = WIN_HALF
    kw = ka.shape[-1]
    prev, main, nxt = _band_specs(tq, side, s, kw)
    qspec = pl.BlockSpec((1, tq, WIN_WIDTH), lambda bi, i: (bi, i, 0))
    return pl.pallas_call(
        functools.partial(_win_kernel, tq),
        grid=(b, s // tq),
        in_specs=[pl.BlockSpec(memory_space=pltpu.SMEM), qspec, prev, main, nxt, prev, main, nxt,
                  _const_spec((1, WIN_WIDTH))],
        out_specs=qspec,
        out_shape=jax.ShapeDtypeStruct((b, s, WIN_WIDTH), BF16),
        compiler_params=_params(("parallel", "parallel")),
        name="win_attn",
    )(sink, qa, ka, ka, ka, va, va, va, g_win)


def _dil_attn(qv, kv, vv, dil, tq):
    b, ln, wr = qv.shape
    side = DIL_SIDE
    lse_w = DIL_SLOTS if dil == 1 else LANES
    prev, main, nxt = _band_specs(tq, side, ln, wr)
    return pl.pallas_call(
        functools.partial(_dil_kernel, dil, tq),
        grid=(b, ln // tq),
        in_specs=[main, prev, main, nxt, prev, main, nxt],
        out_specs=[main, pl.BlockSpec((1, dil, tq, lse_w), lambda bi, i: (bi, 0, i, 0))],
        out_shape=[jax.ShapeDtypeStruct((b, ln, wr), BF16),
                   jax.ShapeDtypeStruct((b, dil, ln, lse_w), F32)],
        compiler_params=_params(("parallel", "parallel")),
        name=f"dil_attn_{dil}",
    )(qv, kv, kv, kv, vv, vv, vv)


def _mem_kernel(mem_ref, g_ref, b_ref, wk_ref, wv_ref, k_ref, v_ref):
    mn = _ln(mem_ref[0], g_ref[...], b_ref[...]).astype(BF16)
    k_ref[0] = _dot(mn, wk_ref[...]).astype(BF16)
    v_ref[0] = _dot(mn, wv_ref[...]).astype(BF16)


def _mem_kv(mem, g, b, wk, wv):
    bsz, m, d = mem.shape
    blk = pl.BlockSpec((1, m, d), lambda bi: (bi, 0, 0))
    return pl.pallas_call(
        _mem_kernel,
        grid=(bsz,),
        in_specs=[blk, _const_spec((1, d)), _const_spec((1, d)), _const_spec((d, d)), _const_spec((d, d))],
        out_specs=[blk, blk],
        out_shape=[jax.ShapeDtypeStruct((bsz, m, d), BF16)] * 2,
        compiler_params=_params(("parallel",)),
        name="mem_kv",
    )(mem, g, b, wk, wv)


def _mix_kernel(tm, oa_ref, o0_ref, o1_ref, o2_ref, l0_ref, l1_ref, l2_ref, h_ref, e_ref, gd_ref, wo_ref,
                g1_ref, b1_ref, wq_ref, kx_ref, vx_ref, wxo_ref, g2_ref, b2_ref, out_ref, stage, lstage):
    ts = tm // N_SUB
    n_chunks = DIL_WIDTH // LANES

    def sub_tile(k):
        rows = slice(k * ts, (k + 1) * ts)
        ls = []
        for gi, (l_ref, (_, dil)) in enumerate(zip((l0_ref, l1_ref, l2_ref), DIL_PAIRS)):
            if dil == 1:
                ls.append(l_ref[0, rows, :])
            else:
                rrows = slice(k * ts // dil, (k + 1) * ts // dil)
                for c in range(dil):
                    lstage[gi - 1, pl.ds(k * ts + c, ts // dil, stride=dil), :] = l_ref[0, c, rrows, :]
                ls.append(lstage[gi - 1, rows, 0:DIL_SLOTS])
        mx = jnp.maximum(jnp.maximum(ls[0], ls[1]), ls[2])
        es = [jnp.exp2(l - mx) for l in ls]
        tot = es[0] + es[1] + es[2]
        ob = None
        for gi, (e, o_ref, (_, dil)) in enumerate(zip(es, (o0_ref, o1_ref, o2_ref), DIL_PAIRS)):
            wexp = _dot((e / tot).astype(BF16), e_ref[...])
            if dil == 1:
                og = o_ref[0, rows, :].astype(F32)
            else:
                rrows = slice(k * ts // dil, (k + 1) * ts // dil)
                for c in range(dil):
                    for j in range(n_chunks):
                        col = c * DIL_WIDTH + j * LANES
                        stage[gi - 1, j, pl.ds(k * ts + c, ts // dil, stride=dil), :] = (
                            o_ref[0, rrows, col:col + LANES].astype(F32))
                og = jnp.concatenate([stage[gi - 1, j, rows, :] for j in range(n_chunks)], -1)
            term = wexp * og
            ob = term if ob is None else ob + term
        obn = _rms(ob, gd_ref[...]).astype(BF16)
        yield
        mix = _dot(oa_ref[0, rows, :], wo_ref[0:WIN_WIDTH, :]) + _dot(obn, wo_ref[WIN_WIDTH:, :])
        yield
        h1 = _ln(DEEPNORM_ALPHA * h_ref[0, rows, :] + mix, g1_ref[...], b1_ref[...])
        yield
        q = _dot(h1.astype(BF16), wq_ref[...]).astype(BF16)
        yield
        heads = []
        for hh in range(X_HEADS):
            cols = slice(hh * X_HEAD_DIM, (hh + 1) * X_HEAD_DIM)
            s = _dot_nt(q[:, cols], kx_ref[0, :, cols])
            yield
            m = jnp.max(s, -1, keepdims=True)
            p = jnp.exp(s - m)
            denom = jnp.sum(p, -1, keepdims=True)
            yield
            heads.append((_dot(p.astype(BF16), vx_ref[0, :, cols]) / denom).astype(BF16))
        xa = _dot(jnp.concatenate(heads, -1), wxo_ref[...])
        yield
        out_ref[0, rows, :] = _ln(DEEPNORM_ALPHA * h1 + xa, g2_ref[...], b2_ref[...])

    _round_robin([sub_tile(k) for k in range(N_SUB)])


def _mix_xattn(oa, os_, ls, h, expand, g_dil, w_out, g1, b1, w_xq, kx, vx, w_xo, g2, b2, tm):
    b, s, d = h.shape
    m = kx.shape[1]
    tok = lambda w: pl.BlockSpec((1, tm, w), lambda bi, i: (bi, i, 0))
    res = lambda dil: pl.BlockSpec((1, tm // dil, dil * DIL_WIDTH), lambda bi, i: (bi, i, 0))
    memspec = pl.BlockSpec((1, m, d), lambda bi, i: (bi, 0, 0))
    return pl.pallas_call(
        functools.partial(_mix_kernel, tm),
        grid=(b, s // tm),
        in_specs=[tok(WIN_WIDTH)] + [res(dil) for _, dil in DIL_PAIRS] +
                 [tok(DIL_SLOTS) if dil == 1 else
                  pl.BlockSpec((1, dil, tm // dil, LANES), lambda bi, i: (bi, 0, i, 0)) for _, dil in DIL_PAIRS] +
                 [tok(d),
                  _const_spec(expand.shape), _const_spec((1, DIL_WIDTH)), _const_spec((d, d)),
                  _const_spec((1, d)), _const_spec((1, d)), _const_spec((d, d)), memspec, memspec,
                  _const_spec((d, d)), _const_spec((1, d)), _const_spec((1, d))],
        out_specs=tok(d),
        out_shape=jax.ShapeDtypeStruct((b, s, d), F32),
        scratch_shapes=[pltpu.VMEM((N_DIL - 1, DIL_WIDTH // LANES, tm, LANES), F32),
                        pltpu.VMEM((N_DIL - 1, tm, LANES), F32)],
        compiler_params=_params(("parallel", "parallel")),
        name="mix_xattn",
    )(oa, *os_, *ls, h, expand, g_dil, w_out, g1, b1, w_xq, kx, vx, w_xo, g2, b2)


FF_CHUNKS = ((0, D_FF),)
HALO = 16
CONV_PHASES = 4


def _ffn_kernel(tm, hp_ref, hm_ref, hn_ref, wg_ref, wu_ref, cw_ref, cb_ref, wd_ref, g3_ref, b3_ref, out_ref,
                gs, us, ys):
    i = pl.program_id(1)
    n = pl.num_programs(1)
    hm = hm_ref[0]
    hb = hm.astype(BF16)
    hp = jnp.where(i > 0, hp_ref[0], 0.0).astype(BF16)
    hn = jnp.where(i < n - 1, hn_ref[0], 0.0).astype(BF16)
    hcat = jnp.concatenate([hp, hb, hn], 0)
    q = tm // CONV_PHASES
    acc = None
    for c0, width in FF_CHUNKS:
        gate = _dot(hcat, wg_ref[:, c0:c0 + width])
        up = _dot(hb, wu_ref[:, c0:c0 + width])
        n_slabs = width // LANES
        for sl in range(n_slabs):
            gs[sl] = gate[:, sl * LANES:(sl + 1) * LANES]
            us[sl] = up[:, sl * LANES:(sl + 1) * LANES]
        phases = []
        for ph in range(CONV_PHASES):
            slabs = []
            for sl in range(n_slabs):
                lanes = slice(c0 + sl * LANES, c0 + (sl + 1) * LANES)
                g = cb_ref[:, lanes]
                for j in range(CONV_WIDTH):
                    g = g + gs[sl, pl.ds(HALO - 1 + j + ph, q, stride=CONV_PHASES), :] * cw_ref[j:j + 1, lanes]
                u = us[sl, pl.ds(ph, q, stride=CONV_PHASES), :]
                slabs.append((0.5 * g * (1.0 + lax.erf(g * np.float32(np.sqrt(0.5)))) * u).astype(BF16))
            phases.append(jnp.concatenate(slabs, -1))
        part = _dot(jnp.concatenate(phases, 0), wd_ref[c0:c0 + width, :])
        acc = part if acc is None else acc + part
    for ph in range(CONV_PHASES):
        for sl in range(D_MODEL // LANES):
            ys[sl, pl.ds(ph, q, stride=CONV_PHASES), :] = acc[ph * q:(ph + 1) * q, sl * LANES:(sl + 1) * LANES]
    ff = jnp.concatenate([ys[sl] for sl in range(D_MODEL // LANES)], -1)
    out_ref[0] = _ln(DEEPNORM_ALPHA * hm + ff, g3_ref[...], b3_ref[...])


def _conv_glu(h2, wg, wu, cw, cb, wd, g3, b3, tm):
    b, s, d = h2.shape
    per = tm // HALO
    nblk = s // HALO
    main = pl.BlockSpec((1, tm, d), lambda bi, i: (bi, i, 0))
    prev = pl.BlockSpec((1, HALO, d), lambda bi, i: (bi, jnp.maximum(i * per - 1, 0), 0))
    nxt = pl.BlockSpec((1, HALO, d), lambda bi, i: (bi, jnp.minimum((i + 1) * per, nblk - 1), 0))
    n_slabs = max(w for _, w in FF_CHUNKS) // LANES
    return pl.pallas_call(
        functools.partial(_ffn_kernel, tm),
        grid=(b, s // tm),
        in_specs=[prev, main, nxt, _const_spec((d, D_FF)), _const_spec((d, D_FF)),
                  _const_spec((CONV_WIDTH, D_FF)), _const_spec((1, D_FF)), _const_spec((D_FF, d)),
                  _const_spec((1, d)), _const_spec((1, d))],
        out_specs=main,
        out_shape=jax.ShapeDtypeStruct((b, s, d), F32),
        scratch_shapes=[pltpu.VMEM((n_slabs, tm + 2 * HALO, LANES), F32), pltpu.VMEM((n_slabs, tm, LANES), F32),
                        pltpu.VMEM((d // LANES, tm, LANES), F32)],
        compiler_params=_params(("parallel", "parallel")),
        name="conv_glu",
    )(h2, h2, h2, wg, wu, cw, cb, wd, g3, b3)


def kernel(x, mem, positions, ln_in_g, ln_in_b, w_in, attn_sink, g_win, g_dil, w_mix_out, ln1_g, ln1_b, mem_ln_g, mem_ln_b, w_xq, w_xk, w_xv, w_xo, ln2_g, ln2_b, w_gate, w_up, conv_w, conv_b, w_down, ln3_g, ln3_b):
    b, s, d = x.shape
    assert DEPTH == 1
    l = 0
    row = lambda v: v.reshape(1, -1)
    cos, sin = _rope_tables(positions)

    qscale = HEAD_DIM ** -0.5 * LOG2E
    o_ka, o_va, o_qb = A_Q, A_Q + A_KV, A_Q + 2 * A_KV
    o_kb, o_vb = o_qb + B_QKV, o_qb + 2 * B_QKV
    names = ["qa", "ka", "va"]
    segs = [(0, A_Q, True, 1, qscale, False),
            (o_ka, A_KV, True, 1, 1.0, True),
            (o_va, A_KV, False, 1, 1.0, True)]
    for kind, base, rope, scale in (("q", o_qb, True, qscale), ("k", o_kb, True, 1.0), ("v", o_vb, False, 1.0)):
        for gi, (_, dil) in enumerate(DIL_PAIRS):
            names.append(f"{kind}{gi}")
            segs.append((base + gi * DIL_WIDTH, DIL_WIDTH, rope, dil, scale, False))

    xscale = X_HEAD_DIM ** -0.5
    to_cast = [(w_mix_out[l], 1.0), (w_xq[l], xscale), (w_xk[l], 1.0), (w_xv[l], 1.0), (w_xo[l], 1.0),
               (w_gate[l], 1.0), (w_up[l], 1.0), (w_down[l], 1.0)]
    outs = _qkv_proj(x.reshape(b * s, d), row(ln_in_g), row(ln_in_b), w_in[l].astype(BF16), tuple(segs),
                     cos, sin, to_cast, tm=QKV_TILE)
    n_seg = 1 + len(segs)
    wo_b, wq_b, wk_b, wv_b, wxo_b, wg_b, wu_b, wd_b = outs[n_seg:]
    hn = outs[0].reshape(b, s, d)
    proj = {name: t.reshape(b, t.shape[0] // b, t.shape[1]) for name, t in zip(names, outs[1:n_seg])}
    qa, ka, va = proj["qa"], proj["ka"], proj["va"]
    oa = _win_attn(qa, ka, va, attn_sink[l], row(g_win[l]), tq=WIN_TILE)
    os_, ls = [], []
    for gi, (_, dil) in enumerate(DIL_PAIRS):
        qg, kg, vg = proj[f"q{gi}"], proj[f"k{gi}"], proj[f"v{gi}"]
        o, lse = _dil_attn(qg, kg, vg, dil, tq=DIL_TOKENS // dil)
        os_.append(o)
        ls.append(lse.reshape(b, s, DIL_SLOTS) if dil == 1 else lse)

    kx, vx = _mem_kv(mem, row(mem_ln_g[l]), row(mem_ln_b[l]), wk_b, wv_b)
    expand = jnp.repeat(jnp.eye(DIL_SLOTS, dtype=BF16), HEAD_DIM, axis=1)
    h2 = _mix_xattn(oa, os_, ls, hn, expand, row(g_dil[l]), wo_b, row(ln1_g[l]), row(ln1_b[l]), wq_b, kx, vx,
                    wxo_b, row(ln2_g[l]), row(ln2_b[l]), tm=MIX_TILE)
    return _conv_glu(h2, wg_b, wu_b, conv_w[l], row(conv_b[l]), wd_b, row(ln3_g[l]), row(ln3_b[l]), tm=FFN_TILE)
```

```python
import functools

import numpy as np
import jax
import jax.numpy as jnp
from jax import lax
from jax.experimental import pallas as pl
from jax.experimental.pallas import tpu as pltpu

D_MODEL = 1024
HEAD_DIM = 64
WIN_Q_HEADS = 8
WIN_KV_HEADS = 2
WIN_HALF = 128
DIL_SLOTS = 8
DIL_PAIRS = ((128, 1), (512, 4), (2048, 16))
N_DIL = len(DIL_PAIRS)
ROT_DIM = HEAD_DIM // 4
ROT_HALF = ROT_DIM // 2
ROPE_THETA = 500000.0
X_HEADS = 4
X_HEAD_DIM = D_MODEL // X_HEADS
D_FF = 2816
CONV_WIDTH = 3
WIN_WIDTH = WIN_Q_HEADS * HEAD_DIM
DIL_WIDTH = DIL_SLOTS * HEAD_DIM
A_Q = WIN_WIDTH
A_KV = WIN_KV_HEADS * HEAD_DIM
B_QKV = N_DIL * DIL_WIDTH
DEPTH = 1
DEEPNORM_ALPHA = (2 * DEPTH) ** 0.25
LN_EPS = 1e-5
NEG_INF = -1e30
LOG2E = float(np.log2(np.e))

LANES = 128
TOK_PER_ROW = LANES // ROT_HALF
Q_BLK = 128
N_SUB = 4
DIL_SIDE = 64
assert all(window // (2 * dil) == DIL_SIDE for window, dil in DIL_PAIRS)
VMEM_LIMIT = 56 * 1024 * 1024
QKV_TILE = 512
WIN_TILE = 2048
DIL_TOKENS = 2048
MIX_TILE = 1024
FFN_TILE = 512

F32 = jnp.float32
BF16 = jnp.bfloat16


def _ln(x, g, b):
    mu = jnp.mean(x, -1, keepdims=True)
    xc = x - mu
    var = jnp.mean(xc * xc, -1, keepdims=True)
    return xc * lax.rsqrt(var + LN_EPS) * g + b


def _rms(x, g):
    return x * lax.rsqrt(jnp.mean(x * x, -1, keepdims=True) + LN_EPS) * g


def _dot(a, b):
    return jnp.dot(a, b, preferred_element_type=F32)


def _dot_nt(a, b):
    return lax.dot_general(a, b, (((1,), (1,)), ((), ())), preferred_element_type=F32)


def _const_spec(shape):
    nd = len(shape)
    return pl.BlockSpec(shape, lambda *_: (0,) * nd, pipeline_mode=pl.Buffered(1))


def _params(sem):
    return pltpu.CompilerParams(dimension_semantics=sem, vmem_limit_bytes=VMEM_LIMIT)


def _round_robin(gens):
    live = list(gens)
    while live:
        for g in list(live):
            try:
                next(g)
            except StopIteration:
                live.remove(g)


def _rope_tab_kernel(pos_ref, invf_ref, cos_ref, sin_ref):
    ang = pos_ref[...].astype(F32) * invf_ref[...]
    cos_ref[...] = jnp.cos(ang)
    sin_ref[...] = jnp.sin(ang)


def _rope_tables(positions):
    inv_freq = ROPE_THETA ** (-jnp.arange(0, ROT_DIM, 2, dtype=F32) / ROT_DIM)
    pos_rep = jnp.repeat(positions.reshape(-1), ROT_HALF).reshape(-1, LANES)
    invf = jnp.tile(inv_freq, TOK_PER_ROW).reshape(1, LANES)
    return pl.pallas_call(
        _rope_tab_kernel,
        out_shape=(jax.ShapeDtypeStruct(pos_rep.shape, F32),) * 2,
        name="rope_tables",
    )(pos_rep, invf)


def _expand_rope(tab, tm):
    rows = jnp.broadcast_to(tab[:, None, :], (tm // TOK_PER_ROW, TOK_PER_ROW, LANES)).reshape(tm, LANES)
    tok = lax.broadcasted_iota(jnp.int32, (tm, LANES), 0) % TOK_PER_ROW
    lane = lax.broadcasted_iota(jnp.int32, (tm, LANES), 1)
    idx = tok * ROT_HALF + lane % ROT_HALF
    return jnp.take_along_axis(rows, idx, axis=1, mode="promise_in_bounds")


QKV_DOT_WIDTH = 1536


def _run_width(segs, c0):
    starts = {s[0]: s[1] for s in segs}
    width = 0
    while c0 + width in starts and width + starts[c0 + width] <= QKV_DOT_WIDTH:
        width += starts[c0 + width]
    return width


def _qkv_kernel(segs, tm, scales, x_ref, g_ref, b_ref, w_ref, cos_ref, sin_ref, *rest):
    n_cast = len(scales)
    cast_in, rest = rest[:n_cast], rest[n_cast:]
    h_ref, out_refs, cast_out, stage = rest[0], rest[1:1 + len(segs)], rest[1 + len(segs):-1], rest[-1]
    h = _ln(x_ref[...], g_ref[...], b_ref[...])
    h_ref[...] = h
    hb = h.astype(BF16)
    for scale, wi_ref, wo_ref in zip(scales, cast_in, cast_out):
        wo_ref[...] = (wi_ref[...] * scale if scale != 1.0 else wi_ref[...]).astype(BF16)
    slot = lax.broadcasted_iota(jnp.int32, (tm, LANES), 1) % HEAD_DIM
    cos = _expand_rope(cos_ref[...], tm)
    sin = _expand_rope(sin_ref[...], tm)
    is_t1 = slot < ROT_HALF
    c_tab = jnp.where(slot < ROT_DIM, cos, 1.0)
    s_tab = jnp.where(is_t1, -sin, jnp.where(slot < ROT_DIM, sin, 0.0))
    tables = {scale: (c_tab, s_tab) if scale == 1.0 else (c_tab * scale, s_tab * scale)
              for scale in sorted({seg[4] for seg in segs if seg[2]})}
    lo = slot == lax.broadcasted_iota(jnp.int32, (tm, LANES), 1)
    z, z0 = None, 0
    for (c0, width, rope, dil, scale, dup), o_ref in zip(segs, out_refs):
        if z is None or c0 + width > z0 + z.shape[1]:
            z0 = c0
            z = _dot(hb, w_ref[:, z0:z0 + _run_width(segs, c0)])
        for j in range(width // LANES):
            zc = z[:, c0 - z0 + j * LANES:c0 - z0 + (j + 1) * LANES]
            if rope:
                up = pltpu.roll(zc, LANES - ROT_HALF, 1)
                dn = pltpu.roll(zc, ROT_HALF, 1)
                c_s, s_s = tables[scale]
                zc = zc * c_s + jnp.where(is_t1, up, dn) * s_s
            elif scale != 1.0:
                zc = zc * scale
            if dup:
                sw = pltpu.roll(zc, HEAD_DIM, 1)
                o_ref[:, 2 * j * LANES:(2 * j + 1) * LANES] = jnp.where(lo, zc, sw).astype(BF16)
                o_ref[:, (2 * j + 1) * LANES:(2 * j + 2) * LANES] = jnp.where(lo, sw, zc).astype(BF16)
            elif dil == 1:
                o_ref[:, j * LANES:(j + 1) * LANES] = zc.astype(BF16)
            else:
                stage[...] = zc
                for c in range(dil):
                    col = c * width + j * LANES
                    o_ref[:, col:col + LANES] = stage[pl.ds(c, tm // dil, stride=dil), :].astype(BF16)


BF16_ROWS = 16


def _qkv_proj(x2, ln_g, ln_b, w_all, segs, cos, sin, to_cast, tm):
    t = x2.shape[0]
    steps = t // tm
    wtot = w_all.shape[1]
    row = lambda r, w: pl.BlockSpec((r, w), lambda i: (i, 0))
    out_shape = [jax.ShapeDtypeStruct((t, D_MODEL), F32)]
    out_specs = [row(tm, D_MODEL)]
    for _, width, _, dil, _, dup in segs:
        wout = 2 * width if dup else dil * width
        out_shape.append(jax.ShapeDtypeStruct((t // dil, wout), BF16))
        out_specs.append(row(tm // dil, wout))
    cast_specs = []
    for w, _ in to_cast:
        rows, cols = w.shape
        nblk = max(n for n in range(1, steps + 1) if steps % n == 0 and rows % (n * BF16_ROWS) == 0)
        spec = pl.BlockSpec((rows // nblk, cols), lambda i, every=steps // nblk: (i // every, 0))
        cast_specs.append(spec)
        out_shape.append(jax.ShapeDtypeStruct((rows, cols), BF16))
        out_specs.append(spec)
    return pl.pallas_call(
        functools.partial(_qkv_kernel, segs, tm, tuple(s for _, s in to_cast)),
        grid=(steps,),
        in_specs=[row(tm, D_MODEL), _const_spec((1, D_MODEL)), _const_spec((1, D_MODEL)),
                  _const_spec((D_MODEL, wtot)), row(tm // TOK_PER_ROW, LANES), row(tm // TOK_PER_ROW, LANES)]
                 + cast_specs,
        out_specs=out_specs,
        out_shape=out_shape,
        scratch_shapes=[pltpu.VMEM((tm, LANES), F32)],
        compiler_params=_params(("arbitrary",)),
        name="qkv_proj",
    )(x2, ln_g, ln_b, w_all, cos, sin, *[w for w, _ in to_cast])


def _band_bias(nq, nk, side):
    qi = lax.broadcasted_iota(jnp.int32, (nq, nk), 0)
    kj = lax.broadcasted_iota(jnp.int32, (nq, nk), 1)
    rel = kj - side - qi
    return jnp.where((rel >= -side) & (rel <= side), 0.0, NEG_INF).astype(F32)


def _halo_rows(parts, lo, hi, cols):
    prev, main, nxt = parts
    side, tq = prev.shape[0], main.shape[0]
    pieces = []
    if lo < 0:
        pieces.append(prev[side + lo:side, cols])
    pieces.append(main[max(lo, 0):min(hi, tq), cols])
    if hi > tq:
        pieces.append(nxt[0:hi - tq, cols])
    return pieces[0] if len(pieces) == 1 else jnp.concatenate(pieces, 0)


def _band_units(side, dil, tq, tile, n_tiles, sink_ref, q_ref, k_parts, v_parts, emit, lse_w=0):
    is_win = sink_ref is not None
    blk = Q_BLK
    nk = blk + 2 * side
    n_blk = tq // blk
    width = q_ref.shape[-1] // dil
    n_pairs = width // LANES
    lo = lax.broadcasted_iota(jnp.int32, (blk, LANES), 1) < HEAD_DIM
    lane8 = lax.broadcasted_iota(jnp.int32, (blk, max(lse_w, 1)), 1)
    band = _band_bias(blk, nk, side)
    kj = lax.broadcasted_iota(jnp.int32, (blk, nk), 1)
    band_first = jnp.where((tile == 0) & (kj < side), NEG_INF, band)
    band_last = jnp.where((tile == n_tiles - 1) & (kj >= side + blk), NEG_INF, band)
    biases = []
    for j in range(n_blk):
        bias = band
        if j == 0:
            bias = band_first
        if j == n_blk - 1:
            bias = band_last if j > 0 else jnp.minimum(band_first, band_last)
        biases.append(jnp.concatenate([bias, bias], 0))

    units = [(j, c, pr) for j in range(n_blk) for c in range(dil) for pr in range(n_pairs)]

    def kv_cols(c, pr):
        if is_win:
            return slice((pr // 2) * LANES, (pr // 2 + 1) * LANES)
        return slice(c * width + pr * LANES, c * width + (pr + 1) * LANES)

    def scores(j, c, pr):
        r0 = j * blk
        q2 = q_ref[r0:r0 + blk, c * width + pr * LANES:c * width + (pr + 1) * LANES]
        zero = jnp.zeros_like(q2)
        qs = jnp.concatenate([jnp.where(lo, q2, zero), jnp.where(lo, zero, q2)], 0)
        keys = _halo_rows(k_parts, r0 - side, r0 + blk + side, kv_cols(c, pr))
        return _dot_nt(qs, keys) + biases[j]

    def softmax(s, pr):
        ps, rdens, lses = [], [], []
        for hf in range(2):
            sh = s[hf * blk:(hf + 1) * blk]
            m = jnp.max(sh, -1, keepdims=True)
            if is_win:
                sink = sink_ref[2 * pr + hf] * LOG2E
                m = jnp.maximum(m, sink)
            ph = jnp.exp2(sh - m)
            denom = jnp.sum(ph, -1, keepdims=True)
            if is_win:
                denom = denom + jnp.exp2(sink - m)
            else:
                lses.append(m + jnp.log2(denom))
            ps.append(ph.astype(BF16))
            rdens.append(1.0 / denom)
        return jnp.concatenate(ps, 0), rdens, lses

    pairs, lse8 = [], None

    def finish(unit, o, rdens, lses):
        nonlocal pairs, lse8
        j, c, pr = unit
        pairs.append(jnp.where(lo, o[:blk] * rdens[0], o[blk:] * rdens[1]))
        if lse_w:
            if lse8 is None:
                lse8 = jnp.zeros((blk, lse_w), F32)
            lse8 = jnp.where(lane8 == 2 * pr, lses[0], lse8)
            lse8 = jnp.where(lane8 == 2 * pr + 1, lses[1], lse8)
        if pr == n_pairs - 1:
            emit(j, c, jnp.concatenate(pairs, -1), lse8)
            pairs, lse8 = [], None

    s_next = scores(*units[0])
    pending = None
    for n, (j, c, pr) in enumerate(units):
        s = s_next
        if n + 1 < len(units):
            s_next = scores(*units[n + 1])
        p, rdens, lses = softmax(s, pr)
        o = _dot(p, _halo_rows(v_parts, j * blk - side, (j + 1) * blk + side, kv_cols(c, pr)))
        if pending is not None:
            finish(*pending)
        pending = ((j, c, pr), o, rdens, lses)
    finish(*pending)


def _dil_kernel(dil, tq, q_ref, kp_ref, km_ref, kn_ref, vp_ref, vm_ref, vn_ref, o_ref, lse_ref):
    width = q_ref.shape[-1] // dil

    def emit(j, c, out, lse):
        rows = slice(j * Q_BLK, (j + 1) * Q_BLK)
        o_ref[0, rows, c * width:(c + 1) * width] = out.astype(BF16)
        lse_ref[0, c, rows, :] = lse

    _band_units(DIL_SIDE, dil, tq, pl.program_id(1), pl.num_programs(1), None, q_ref.at[0],
                (kp_ref.at[0], km_ref.at[0], kn_ref.at[0]), (vp_ref.at[0], vm_ref.at[0], vn_ref.at[0]), emit,
                lse_w=lse_ref.shape[-1])


def _band_specs(tq, side, n_rows, width):
    per = tq // side
    nblk = n_rows // side
    main = pl.BlockSpec((1, tq, width), lambda bi, i: (bi, i, 0))
    prev = pl.BlockSpec((1, side, width), lambda bi, i: (bi, jnp.maximum(i * per - 1, 0), 0))
    nxt = pl.BlockSpec((1, side, width), lambda bi, i: (bi, jnp.minimum((i + 1) * per, nblk - 1), 0))
    return prev, main, nxt


def _win_kernel(tq, sink_ref, q_ref, kp_ref, km_ref, kn_ref, vp_ref, vm_ref, vn_ref, g_ref, o_ref):

    def emit(j, c, out, lse):
        o_ref[0, j * Q_BLK:(j + 1) * Q_BLK, :] = _rms(out, g_ref[...]).astype(BF16)

    _band_units(WIN_HALF, 1, tq, pl.program_id(1), pl.num_programs(1), sink_ref, q_ref.at[0],
                (kp_ref.at[0], km_ref.at[0], kn_ref.at[0]), (vp_ref.at[0], vm_ref.at[0], vn_ref.at[0]), emit)


def _win_attn(qa, ka, va, sink, g_win, tq):
    b, s, _ = qa.shape
    side = WIN_HALF
    kw = ka.shape[-1]
    prev, main, nxt = _band_specs(tq, side, s, kw)
    qspec = pl.BlockSpec((1, tq, WIN_WIDTH), lambda bi, i: (bi, i, 0))
    return pl.pallas_call(
        functools.partial(_win_kernel, tq),
        grid=(b, s // tq),
        in_specs=[pl.BlockSpec(memory_space=pltpu.SMEM), qspec, prev, main, nxt, prev, main, nxt,
                  _const_spec((1, WIN_WIDTH))],
        out_specs=qspec,
        out_shape=jax.ShapeDtypeStruct((b, s, WIN_WIDTH), BF16),
        compiler_params=_params(("parallel", "parallel")),
        name="win_attn",
    )(sink, qa, ka, ka, ka, va, va, va, g_win)


def _dil_attn(qv, kv, vv, dil, tq):
    b, ln, wr = qv.shape
    side = DIL_SIDE
    lse_w = DIL_SLOTS if dil == 1 else LANES
    prev, main, nxt = _band_specs(tq, side, ln, wr)
    return pl.pallas_call(
        functools.partial(_dil_kernel, dil, tq),
        grid=(b, ln // tq),
        in_specs=[main, prev, main, nxt, prev, main, nxt],
        out_specs=[main, pl.BlockSpec((1, dil, tq, lse_w), lambda bi, i: (bi, 0, i, 0))],
        out_shape=[jax.ShapeDtypeStruct((b, ln, wr), BF16),
                   jax.ShapeDtypeStruct((b, dil, ln, lse_w), F32)],
        compiler_params=_params(("parallel", "parallel")),
        name=f"dil_attn_{dil}",
    )(qv, kv, kv, kv, vv, vv, vv)


def _mem_kernel(mem_ref, g_ref, b_ref, wk_ref, wv_ref, k_ref, v_ref):
    mn = _ln(mem_ref[0], g_ref[...], b_ref[...]).astype(BF16)
    k_ref[0] = _dot(mn, wk_ref[...]).astype(BF16)
    v_ref[0] = _dot(mn, wv_ref[...]).astype(BF16)


def _mem_kv(mem, g, b, wk, wv):
    bsz, m, d = mem.shape
    blk = pl.BlockSpec((1, m, d), lambda bi: (bi, 0, 0))
    return pl.pallas_call(
        _mem_kernel,
        grid=(bsz,),
        in_specs=[blk, _const_spec((1, d)), _const_spec((1, d)), _const_spec((d, d)), _const_spec((d, d))],
        out_specs=[blk, blk],
        out_shape=[jax.ShapeDtypeStruct((bsz, m, d), BF16)] * 2,
        compiler_params=_params(("parallel",)),
        name="mem_kv",
    )(mem, g, b, wk, wv)


def _mix_kernel(tm, oa_ref, o0_ref, o1_ref, o2_ref, l0_ref, l1_ref, l2_ref, h_ref, e_ref, gd_ref, wo_ref,
                g1_ref, b1_ref, wq_ref, kx_ref, vx_ref, wxo_ref, g2_ref, b2_ref, out_ref, stage, lstage):
    ts = tm // N_SUB
    n_chunks = DIL_WIDTH // LANES

    def sub_tile(k):
        rows = slice(k * ts, (k + 1) * ts)
        ls = []
        for gi, (l_ref, (_, dil)) in enumerate(zip((l0_ref, l1_ref, l2_ref), DIL_PAIRS)):
            if dil == 1:
                ls.append(l_ref[0, rows, :])
            else:
                rrows = slice(k * ts // dil, (k + 1) * ts // dil)
                for c in range(dil):
                    lstage[gi - 1, pl.ds(k * ts + c, ts // dil, stride=dil), :] = l_ref[0, c, rrows, :]
                ls.append(lstage[gi - 1, rows, 0:DIL_SLOTS])
        mx = jnp.maximum(jnp.maximum(ls[0], ls[1]), ls[2])
        es = [jnp.exp2(l - mx) for l in ls]
        tot = es[0] + es[1] + es[2]
        ob = None
        for gi, (e, o_ref, (_, dil)) in enumerate(zip(es, (o0_ref, o1_ref, o2_ref), DIL_PAIRS)):
            wexp = _dot((e / tot).astype(BF16), e_ref[...])
            if dil == 1:
                og = o_ref[0, rows, :].astype(F32)
            else:
                rrows = slice(k * ts // dil, (k + 1) * ts // dil)
                for c in range(dil):
                    for j in range(n_chunks):
                        col = c * DIL_WIDTH + j * LANES
                        stage[gi - 1, j, pl.ds(k * ts + c, ts // dil, stride=dil), :] = (
                            o_ref[0, rrows, col:col + LANES].astype(F32))
                og = jnp.concatenate([stage[gi - 1, j, rows, :] for j in range(n_chunks)], -1)
            term = wexp * og
            ob = term if ob is None else ob + term
        obn = _rms(ob, gd_ref[...]).astype(BF16)
        yield
        mix = _dot(oa_ref[0, rows, :], wo_ref[0:WIN_WIDTH, :]) + _dot(obn, wo_ref[WIN_WIDTH:, :])
        yield
        h1 = _ln(DEEPNORM_ALPHA * h_ref[0, rows, :] + mix, g1_ref[...], b1_ref[...])
        yield
        q = _dot(h1.astype(BF16), wq_ref[...]).astype(BF16)
        yield
        heads = []
        for hh in range(X_HEADS):
            cols = slice(hh * X_HEAD_DIM, (hh + 1) * X_HEAD_DIM)
            s = _dot_nt(q[:, cols], kx_ref[0, :, cols])
            yield
            m = jnp.max(s, -1, keepdims=True)
            p = jnp.exp(s - m)
            denom = jnp.sum(p, -1, keepdims=True)
            yield
            heads.append((_dot(p.astype(BF16), vx_ref[0, :, cols]) / denom).astype(BF16))
        xa = _dot(jnp.concatenate(heads, -1), wxo_ref[...])
        yield
        out_ref[0, rows, :] = _ln(DEEPNORM_ALPHA * h1 + xa, g2_ref[...], b2_ref[...])

    _round_robin([sub_tile(k) for k in range(N_SUB)])


def _mix_xattn(oa, os_, ls, h, expand, g_dil, w_out, g1, b1, w_xq, kx, vx, w_xo, g2, b2, tm):
    b, s, d = h.shape
    m = kx.shape[1]
    tok = lambda w: pl.BlockSpec((1, tm, w), lambda bi, i: (bi, i, 0))
    res = lambda dil: pl.BlockSpec((1, tm // dil, dil * DIL_WIDTH), lambda bi, i: (bi, i, 0))
    memspec = pl.BlockSpec((1, m, d), lambda bi, i: (bi, 0, 0))
    return pl.pallas_call(
        functools.partial(_mix_kernel, tm),
        grid=(b, s // tm),
        in_specs=[tok(WIN_WIDTH)] + [res(dil) for _, dil in DIL_PAIRS] +
                 [tok(DIL_SLOTS) if dil == 1 else
                  pl.BlockSpec((1, dil, tm // dil, LANES), lambda bi, i: (bi, 0, i, 0)) for _, dil in DIL_PAIRS] +
                 [tok(d),
                  _const_spec(expand.shape), _const_spec((1, DIL_WIDTH)), _const_spec((d, d)),
                  _const_spec((1, d)), _const_spec((1, d)), _const_spec((d, d)), memspec, memspec,
                  _const_spec((d, d)), _const_spec((1, d)), _const_spec((1, d))],
        out_specs=tok(d),
        out_shape=jax.ShapeDtypeStruct((b, s, d), F32),
        scratch_shapes=[pltpu.VMEM((N_DIL - 1, DIL_WIDTH // LANES, tm, LANES), F32),
                        pltpu.VMEM((N_DIL - 1, tm, LANES), F32)],
        compiler_params=_params(("parallel", "parallel")),
        name="mix_xattn",
    )(oa, *os_, *ls, h, expand, g_dil, w_out, g1, b1, w_xq, kx, vx, w_xo, g2, b2)


FF_CHUNKS = ((0, D_FF),)
HALO = 16
CONV_PHASES = 4


def _ffn_kernel(tm, hp_ref, hm_ref, hn_ref, wg_ref, wu_ref, cw_ref, cb_ref, wd_ref, g3_ref, b3_ref, out_ref,
                gs, ys):
    i = pl.program_id(1)
    n = pl.num_programs(1)
    hm = hm_ref[0]
    hb = hm.astype(BF16)
    hp = jnp.where(i > 0, hp_ref[0], 0.0).astype(BF16)
    hn = jnp.where(i < n - 1, hn_ref[0], 0.0).astype(BF16)
    hcat = jnp.concatenate([hp, hb, hn], 0)
    q = tm // CONV_PHASES
    for sl in range(D_MODEL // LANES):
        ys[sl] = hm[:, sl * LANES:(sl + 1) * LANES]
    hb_phased = jnp.concatenate(
        [jnp.concatenate([ys[sl, pl.ds(ph, q, stride=CONV_PHASES), :] for sl in range(D_MODEL // LANES)], -1)
         for ph in range(CONV_PHASES)], 0).astype(BF16)
    acc = None
    for c0, width in FF_CHUNKS:
        gate = _dot(hcat, wg_ref[:, c0:c0 + width])
        up = _dot(hb_phased, wu_ref[:, c0:c0 + width])
        n_slabs = width // LANES
        for sl in range(n_slabs):
            gs[sl] = gate[:, sl * LANES:(sl + 1) * LANES]
        phases = []
        for ph in range(CONV_PHASES):
            slabs = []
            for sl in range(n_slabs):
                lanes = slice(c0 + sl * LANES, c0 + (sl + 1) * LANES)
                g = cb_ref[:, lanes]
                for j in range(CONV_WIDTH):
                    g = g + gs[sl, pl.ds(HALO - 1 + j + ph, q, stride=CONV_PHASES), :] * cw_ref[j:j + 1, lanes]
                u = up[ph * q:(ph + 1) * q, sl * LANES:(sl + 1) * LANES]
                slabs.append((0.5 * g * (1.0 + lax.erf(g * np.float32(np.sqrt(0.5)))) * u).astype(BF16))
            phases.append(jnp.concatenate(slabs, -1))
        part = _dot(jnp.concatenate(phases, 0), wd_ref[c0:c0 + width, :])
        acc = part if acc is None else acc + part
    for ph in range(CONV_PHASES):
        for sl in range(D_MODEL // LANES):
            ys[sl, pl.ds(ph, q, stride=CONV_PHASES), :] = acc[ph * q:(ph + 1) * q, sl * LANES:(sl + 1) * LANES]
    ff = jnp.concatenate([ys[sl] for sl in range(D_MODEL // LANES)], -1)
    out_ref[0] = _ln(DEEPNORM_ALPHA * hm + ff, g3_ref[...], b3_ref[...])


def _conv_glu(h2, wg, wu, cw, cb, wd, g3, b3, tm):
    b, s, d = h2.shape
    per = tm // HALO
    nblk = s // HALO
    main = pl.BlockSpec((1, tm, d), lambda bi, i: (bi, i, 0))
    prev = pl.BlockSpec((1, HALO, d), lambda bi, i: (bi, jnp.maximum(i * per - 1, 0), 0))
    nxt = pl.BlockSpec((1, HALO, d), lambda bi, i: (bi, jnp.minimum((i + 1) * per, nblk - 1), 0))
    n_slabs = max(w for _, w in FF_CHUNKS) // LANES
    return pl.pallas_call(
        functools.partial(_ffn_kernel, tm),
        grid=(b, s // tm),
        in_specs=[prev, main, nxt, _const_spec((d, D_FF)), _const_spec((d, D_FF)),
                  _const_spec((CONV_WIDTH, D_FF)), _const_spec((1, D_FF)), _const_spec((D_FF, d)),
                  _const_spec((1, d)), _const_spec((1, d))],
        out_specs=main,
        out_shape=jax.ShapeDtypeStruct((b, s, d), F32),
        scratch_shapes=[pltpu.VMEM((n_slabs, tm + 2 * HALO, LANES), F32), pltpu.VMEM((d // LANES, tm, LANES), F32)],
        compiler_params=_params(("parallel", "parallel")),
        name="conv_glu",
    )(h2, h2, h2, wg, wu, cw, cb, wd, g3, b3)


def kernel(x, mem, positions, ln_in_g, ln_in_b, w_in, attn_sink, g_win, g_dil, w_mix_out, ln1_g, ln1_b, mem_ln_g, mem_ln_b, w_xq, w_xk, w_xv, w_xo, ln2_g, ln2_b, w_gate, w_up, conv_w, conv_b, w_down, ln3_g, ln3_b):
    b, s, d = x.shape
    assert DEPTH == 1
    l = 0
    row = lambda v: v.reshape(1, -1)
    cos, sin = _rope_tables(positions)

    qscale = HEAD_DIM ** -0.5 * LOG2E
    o_ka, o_va, o_qb = A_Q, A_Q + A_KV, A_Q + 2 * A_KV
    o_kb, o_vb = o_qb + B_QKV, o_qb + 2 * B_QKV
    names = ["qa", "ka", "va"]
    segs = [(0, A_Q, True, 1, qscale, False),
            (o_ka, A_KV, True, 1, 1.0, True),
            (o_va, A_KV, False, 1, 1.0, True)]
    for kind, base, rope, scale in (("q", o_qb, True, qscale), ("k", o_kb, True, 1.0), ("v", o_vb, False, 1.0)):
        for gi, (_, dil) in enumerate(DIL_PAIRS):
            names.append(f"{kind}{gi}")
            segs.append((base + gi * DIL_WIDTH, DIL_WIDTH, rope, dil, scale, False))

    xscale = X_HEAD_DIM ** -0.5
    to_cast = [(w_mix_out[l], 1.0), (w_xq[l], xscale), (w_xk[l], 1.0), (w_xv[l], 1.0), (w_xo[l], 1.0),
               (w_gate[l], 1.0), (w_up[l], 1.0), (w_down[l], 1.0)]
    outs = _qkv_proj(x.reshape(b * s, d), row(ln_in_g), row(ln_in_b), w_in[l].astype(BF16), tuple(segs),
                     cos, sin, to_cast, tm=QKV_TILE)
    n_seg = 1 + len(segs)
    wo_b, wq_b, wk_b, wv_b, wxo_b, wg_b, wu_b, wd_b = outs[n_seg:]
    hn = outs[0].reshape(b, s, d)
    proj = {name: t.reshape(b, t.shape[0] // b, t.shape[1]) for name, t in zip(names, outs[1:n_seg])}
    qa, ka, va = proj["qa"], proj["ka"], proj["va"]
    oa = _win_attn(qa, ka, va, attn_sink[l], row(g_win[l]), tq=WIN_TILE)
    os_, ls = [], []
    for gi, (_, dil) in enumerate(DIL_PAIRS):
        qg, kg, vg = proj[f"q{gi}"], proj[f"k{gi}"], proj[f"v{gi}"]
        o, lse = _dil_attn(qg, kg, vg, dil, tq=DIL_TOKENS // dil)
        os_.append(o)
        ls.append(lse.reshape(b, s, DIL_SLOTS) if dil == 1 else lse)

    kx, vx = _mem_kv(mem, row(mem_ln_g[l]), row(mem_ln_b[l]), wk_b, wv_b)
    expand = jnp.repeat(jnp.eye(DIL_SLOTS, dtype=BF16), HEAD_DIM, axis=1)
    h2 = _mix_xattn(oa, os_, ls, hn, expand, row(g_dil[l]), wo_b, row(ln1_g[l]), row(ln1_b[l]), wq_b, kx, vx,
                    wxo_b, row(ln2_g[l]), row(ln2_b[l]), tm=MIX_TILE)
    return _conv_glu(h2, wg_b, wu_b, conv_w[l], row(conv_b[l]), wd_b, row(ln3_g[l]), row(ln3_b[l]), tm=FFN_TILE)
```

```python
import functools

import numpy as np
import jax
import jax.numpy as jnp
from jax import lax
from jax.experimental import pallas as pl
from jax.experimental.pallas import tpu as pltpu

D_MODEL = 1024
HEAD_DIM = 64
WIN_Q_HEADS = 8
WIN_KV_HEADS = 2
WIN_HALF = 128
DIL_SLOTS = 8
DIL_PAIRS = ((128, 1), (512, 4), (2048, 16))
N_DIL = len(DIL_PAIRS)
ROT_DIM = HEAD_DIM // 4
ROT_HALF = ROT_DIM // 2
ROPE_THETA = 500000.0
X_HEADS = 4
X_HEAD_DIM = D_MODEL // X_HEADS
D_FF = 2816
CONV_WIDTH = 3
WIN_WIDTH = WIN_Q_HEADS * HEAD_DIM
DIL_WIDTH = DIL_SLOTS * HEAD_DIM
A_Q = WIN_WIDTH
A_KV = WIN_KV_HEADS * HEAD_DIM
B_QKV = N_DIL * DIL_WIDTH
DEPTH = 1
DEEPNORM_ALPHA = (2 * DEPTH) ** 0.25
LN_EPS = 1e-5
NEG_INF = -1e30
LOG2E = float(np.log2(np.e))

LANES = 128
TOK_PER_ROW = LANES // ROT_HALF
Q_BLK = 128
N_SUB = 4
DIL_SIDE = 64
assert all(window // (2 * dil) == DIL_SIDE for window, dil in DIL_PAIRS)
VMEM_LIMIT = 56 * 1024 * 1024
QKV_TILE = 512
WIN_TILE = 2048
DIL_TOKENS = 2048
MIX_TILE = 1024
FFN_TILE = 512

F32 = jnp.float32
BF16 = jnp.bfloat16


def _ln(x, g, b):
    mu = jnp.mean(x, -1, keepdims=True)
    xc = x - mu
    var = jnp.mean(xc * xc, -1, keepdims=True)
    return xc * lax.rsqrt(var + LN_EPS) * g + b


def _rms(x, g):
    return x * lax.rsqrt(jnp.mean(x * x, -1, keepdims=True) + LN_EPS) * g


def _dot(a, b):
    return jnp.dot(a, b, preferred_element_type=F32)


def _dot_nt(a, b):
    return lax.dot_general(a, b, (((1,), (1,)), ((), ())), preferred_element_type=F32)


def _const_spec(shape):
    nd = len(shape)
    return pl.BlockSpec(shape, lambda *_: (0,) * nd, pipeline_mode=pl.Buffered(1))


def _params(sem):
    return pltpu.CompilerParams(dimension_semantics=sem, vmem_limit_bytes=VMEM_LIMIT)


def _round_robin(gens):
    live = list(gens)
    while live:
        for g in list(live):
            try:
                next(g)
            except StopIteration:
                live.remove(g)


def _rope_tab_kernel(pos_ref, invf_ref, cos_ref, sin_ref):
    ang = pos_ref[...].astype(F32) * invf_ref[...]
    cos_ref[...] = jnp.cos(ang)
    sin_ref[...] = jnp.sin(ang)


def _rope_tables(positions):
    inv_freq = ROPE_THETA ** (-jnp.arange(0, ROT_DIM, 2, dtype=F32) / ROT_DIM)
    pos_rep = jnp.repeat(positions.reshape(-1), ROT_HALF).reshape(-1, LANES)
    invf = jnp.tile(inv_freq, TOK_PER_ROW).reshape(1, LANES)
    return pl.pallas_call(
        _rope_tab_kernel,
        out_shape=(jax.ShapeDtypeStruct(pos_rep.shape, F32),) * 2,
        name="rope_tables",
    )(pos_rep, invf)


def _expand_rope(tab, tm):
    rows = jnp.broadcast_to(tab[:, None, :], (tm // TOK_PER_ROW, TOK_PER_ROW, LANES)).reshape(tm, LANES)
    tok = lax.broadcasted_iota(jnp.int32, (tm, LANES), 0) % TOK_PER_ROW
    lane = lax.broadcasted_iota(jnp.int32, (tm, LANES), 1)
    idx = tok * ROT_HALF + lane % ROT_HALF
    return jnp.take_along_axis(rows, idx, axis=1, mode="promise_in_bounds")


QKV_DOT_WIDTH = 1536
X_SLOTS = 3


def _run_width(segs, c0):
    starts = {s[0]: s[1] for s in segs}
    width = 0
    while c0 + width in starts and width + starts[c0 + width] <= QKV_DOT_WIDTH:
        width += starts[c0 + width]
    return width


def _qkv_kernel(segs, tm, scales, x_ref, g_ref, b_ref, w_ref, cos_ref, sin_ref, *rest):
    n_cast = len(scales)
    cast_in, rest = rest[:n_cast], rest[n_cast:]
    h_ref, out_refs, cast_out = rest[0], rest[1:1 + len(segs)], rest[1 + len(segs):-3]
    stage, xbuf, xsem = rest[-3:]
    i = pl.program_id(0)
    n = pl.num_programs(0)

    def x_copy(t):
        slot_t = t % X_SLOTS
        return pltpu.make_async_copy(x_ref.at[pl.ds(t * tm, tm), :], xbuf.at[slot_t], xsem.at[slot_t])

    @pl.when(i == 0)
    def _():
        for t in range(X_SLOTS - 1):
            x_copy(t).start()

    @pl.when(i + X_SLOTS - 1 < n)
    def _():
        x_copy(i + X_SLOTS - 1).start()

    x_copy(i).wait()
    h = _ln(xbuf[i % X_SLOTS], g_ref[...], b_ref[...])
    h_ref[...] = h
    hb = h.astype(BF16)
    for scale, wi_ref, wo_ref in zip(scales, cast_in, cast_out):
        wo_ref[...] = (wi_ref[...] * scale if scale != 1.0 else wi_ref[...]).astype(BF16)
    slot = lax.broadcasted_iota(jnp.int32, (tm, LANES), 1) % HEAD_DIM
    cos = _expand_rope(cos_ref[...], tm)
    sin = _expand_rope(sin_ref[...], tm)
    is_t1 = slot < ROT_HALF
    c_tab = jnp.where(slot < ROT_DIM, cos, 1.0)
    s_tab = jnp.where(is_t1, -sin, jnp.where(slot < ROT_DIM, sin, 0.0))
    tables = {scale: (c_tab, s_tab) if scale == 1.0 else (c_tab * scale, s_tab * scale)
              for scale in sorted({seg[4] for seg in segs if seg[2]})}
    lo = slot == lax.broadcasted_iota(jnp.int32, (tm, LANES), 1)
    z, z0 = None, 0
    for (c0, width, rope, dil, scale, dup), o_ref in zip(segs, out_refs):
        if z is None or c0 + width > z0 + z.shape[1]:
            z0 = c0
            z = _dot(hb, w_ref[:, z0:z0 + _run_width(segs, c0)])
        for j in range(width // LANES):
            zc = z[:, c0 - z0 + j * LANES:c0 - z0 + (j + 1) * LANES]
            if rope:
                up = pltpu.roll(zc, LANES - ROT_HALF, 1)
                dn = pltpu.roll(zc, ROT_HALF, 1)
                c_s, s_s = tables[scale]
                zc = zc * c_s + jnp.where(is_t1, up, dn) * s_s
            elif scale != 1.0:
                zc = zc * scale
            if dup:
                sw = pltpu.roll(zc, HEAD_DIM, 1)
                o_ref[:, 2 * j * LANES:(2 * j + 1) * LANES] = jnp.where(lo, zc, sw).astype(BF16)
                o_ref[:, (2 * j + 1) * LANES:(2 * j + 2) * LANES] = jnp.where(lo, sw, zc).astype(BF16)
            elif dil == 1:
                o_ref[:, j * LANES:(j + 1) * LANES] = zc.astype(BF16)
            else:
                stage[...] = zc
                for c in range(dil):
                    col = c * width + j * LANES
                    o_ref[:, col:col + LANES] = stage[pl.ds(c, tm // dil, stride=dil), :].astype(BF16)


BF16_ROWS = 16


def _qkv_proj(x2, ln_g, ln_b, w_all, segs, cos, sin, to_cast, tm):
    t = x2.shape[0]
    steps = t // tm
    wtot = w_all.shape[1]
    row = lambda r, w: pl.BlockSpec((r, w), lambda i: (i, 0))
    out_shape = [jax.ShapeDtypeStruct((t, D_MODEL), F32)]
    out_specs = [row(tm, D_MODEL)]
    for _, width, _, dil, _, dup in segs:
        wout = 2 * width if dup else dil * width
        out_shape.append(jax.ShapeDtypeStruct((t // dil, wout), BF16))
        out_specs.append(row(tm // dil, wout))
    cast_specs = []
    for w, _ in to_cast:
        rows, cols = w.shape
        nblk = max(n for n in range(1, steps + 1) if steps % n == 0 and rows % (n * BF16_ROWS) == 0)
        spec = pl.BlockSpec((rows // nblk, cols), lambda i, every=steps // nblk: (i // every, 0))
        cast_specs.append(spec)
        out_shape.append(jax.ShapeDtypeStruct((rows, cols), BF16))
        out_specs.append(spec)
    return pl.pallas_call(
        functools.partial(_qkv_kernel, segs, tm, tuple(s for _, s in to_cast)),
        grid=(steps,),
        in_specs=[pl.BlockSpec(memory_space=pl.ANY), _const_spec((1, D_MODEL)), _const_spec((1, D_MODEL)),
                  _const_spec((D_MODEL, wtot)), row(tm // TOK_PER_ROW, LANES), row(tm // TOK_PER_ROW, LANES)]
                 + cast_specs,
        out_specs=out_specs,
        out_shape=out_shape,
        scratch_shapes=[pltpu.VMEM((tm, LANES), F32), pltpu.VMEM((X_SLOTS, tm, D_MODEL), F32),
                        pltpu.SemaphoreType.DMA((X_SLOTS,))],
        compiler_params=_params(("arbitrary",)),
        name="qkv_proj",
    )(x2, ln_g, ln_b, w_all, cos, sin, *[w for w, _ in to_cast])


def _band_bias(nq, nk, side):
    qi = lax.broadcasted_iota(jnp.int32, (nq, nk), 0)
    kj = lax.broadcasted_iota(jnp.int32, (nq, nk), 1)
    rel = kj - side - qi
    return jnp.where((rel >= -side) & (rel <= side), 0.0, NEG_INF).astype(F32)


def _halo_rows(parts, lo, hi, cols):
    prev, main, nxt = parts
    side, tq = prev.shape[0], main.shape[0]
    pieces = []
    if lo < 0:
        pieces.append(prev[side + lo:side, cols])
    pieces.append(main[max(lo, 0):min(hi, tq), cols])
    if hi > tq:
        pieces.append(nxt[0:hi - tq, cols])
    return pieces[0] if len(pieces) == 1 else jnp.concatenate(pieces, 0)


def _band_units(side, dil, tq, tile, n_tiles, sink_ref, q_ref, k_parts, v_parts, emit, lse_w=0):
    is_win = sink_ref is not None
    blk = Q_BLK
    nk = blk + 2 * side
    n_blk = tq // blk
    width = q_ref.shape[-1] // dil
    n_pairs = width // LANES
    lo = lax.broadcasted_iota(jnp.int32, (blk, LANES), 1) < HEAD_DIM
    lane8 = lax.broadcasted_iota(jnp.int32, (blk, max(lse_w, 1)), 1)
    band = _band_bias(blk, nk, side)
    kj = lax.broadcasted_iota(jnp.int32, (blk, nk), 1)
    band_first = jnp.where((tile == 0) & (kj < side), NEG_INF, band)
    band_last = jnp.where((tile == n_tiles - 1) & (kj >= side + blk), NEG_INF, band)
    biases = []
    for j in range(n_blk):
        bias = band
        if j == 0:
            bias = band_first
        if j == n_blk - 1:
            bias = band_last if j > 0 else jnp.minimum(band_first, band_last)
        biases.append(jnp.concatenate([bias, bias], 0))

    units = [(j, c, pr) for j in range(n_blk) for c in range(dil) for pr in range(n_pairs)]

    def kv_cols(c, pr):
        if is_win:
            return slice((pr // 2) * LANES, (pr // 2 + 1) * LANES)
        return slice(c * width + pr * LANES, c * width + (pr + 1) * LANES)

    def scores(j, c, pr):
        r0 = j * blk
        q2 = q_ref[r0:r0 + blk, c * width + pr * LANES:c * width + (pr + 1) * LANES]
        zero = jnp.zeros_like(q2)
        qs = jnp.concatenate([jnp.where(lo, q2, zero), jnp.where(lo, zero, q2)], 0)
        keys = _halo_rows(k_parts, r0 - side, r0 + blk + side, kv_cols(c, pr))
        return _dot_nt(qs, keys) + biases[j]

    def softmax(s, pr):
        ps, rdens, lses = [], [], []
        for hf in range(2):
            sh = s[hf * blk:(hf + 1) * blk]
            m = jnp.max(sh, -1, keepdims=True)
            if is_win:
                sink = sink_ref[2 * pr + hf] * LOG2E
                m = jnp.maximum(m, sink)
            ph = jnp.exp2(sh - m)
            denom = jnp.sum(ph, -1, keepdims=True)
            if is_win:
                denom = denom + jnp.exp2(sink - m)
            else:
                lses.append(m + jnp.log2(denom))
            ps.append(ph.astype(BF16))
            rdens.append(1.0 / denom)
        return jnp.concatenate(ps, 0), rdens, lses

    pairs, lse8 = [], None

    def finish(unit, o, rdens, lses):
        nonlocal pairs, lse8
        j, c, pr = unit
        pairs.append(jnp.where(lo, o[:blk] * rdens[0], o[blk:] * rdens[1]))
        if lse_w:
            if lse8 is None:
                lse8 = jnp.zeros((blk, lse_w), F32)
            lse8 = jnp.where(lane8 == 2 * pr, lses[0], lse8)
            lse8 = jnp.where(lane8 == 2 * pr + 1, lses[1], lse8)
        if pr == n_pairs - 1:
            emit(j, c, jnp.concatenate(pairs, -1), lse8)
            pairs, lse8 = [], None

    s_next = scores(*units[0])
    pending = None
    for n, (j, c, pr) in enumerate(units):
        s = s_next
        if n + 1 < len(units):
            s_next = scores(*units[n + 1])
        p, rdens, lses = softmax(s, pr)
        o = _dot(p, _halo_rows(v_parts, j * blk - side, (j + 1) * blk + side, kv_cols(c, pr)))
        if pending is not None:
            finish(*pending)
        pending = ((j, c, pr), o, rdens, lses)
    finish(*pending)


def _dil_kernel(dil, tq, q_ref, kp_ref, km_ref, kn_ref, vp_ref, vm_ref, vn_ref, o_ref, lse_ref):
    width = q_ref.shape[-1] // dil

    def emit(j, c, out, lse):
        rows = slice(j * Q_BLK, (j + 1) * Q_BLK)
        o_ref[0, rows, c * width:(c + 1) * width] = out.astype(BF16)
        lse_ref[0, c, rows, :] = lse

    _band_units(DIL_SIDE, dil, tq, pl.program_id(1), pl.num_programs(1), None, q_ref.at[0],
                (kp_ref.at[0], km_ref.at[0], kn_ref.at[0]), (vp_ref.at[0], vm_ref.at[0], vn_ref.at[0]), emit,
                lse_w=lse_ref.shape[-1])


def _band_specs(tq, side, n_rows, width):
    per = tq // side
    nblk = n_rows // side
    main = pl.BlockSpec((1, tq, width), lambda bi, i: (bi, i, 0))
    prev = pl.BlockSpec((1, side, width), lambda bi, i: (bi, jnp.maximum(i * per - 1, 0), 0))
    nxt = pl.BlockSpec((1, side, width), lambda bi, i: (bi, jnp.minimum((i + 1) * per, nblk - 1), 0))
    return prev, main, nxt


def _win_kernel(tq, sink_ref, q_ref, kp_ref, km_ref, kn_ref, vp_ref, vm_ref, vn_ref, g_ref, o_ref):

    def emit(j, c, out, lse):
        o_ref[0, j * Q_BLK:(j + 1) * Q_BLK, :] = _rms(out, g_ref[...]).astype(BF16)

    _band_units(WIN_HALF, 1, tq, pl.program_id(1), pl.num_programs(1), sink_ref, q_ref.at[0],
                (kp_ref.at[0], km_ref.at[0], kn_ref.at[0]), (vp_ref.at[0], vm_ref.at[0], vn_ref.at[0]), emit)


def _win_attn(qa, ka, va, sink, g_win, tq):
    b, s, _ = qa.shape
    side = WIN_HALF
    kw = ka.shape[-1]
    prev, main, nxt = _band_specs(tq, side, s, kw)
    qspec = pl.BlockSpec((1, tq, WIN_WIDTH), lambda bi, i: (bi, i, 0))
    return pl.pallas_call(
        functools.partial(_win_kernel, tq),
        grid=(b, s // tq),
        in_specs=[pl.BlockSpec(memory_space=pltpu.SMEM), qspec, prev, main, nxt, prev, main, nxt,
                  _const_spec((1, WIN_WIDTH))],
        out_specs=qspec,
        out_shape=jax.ShapeDtypeStruct((b, s, WIN_WIDTH), BF16),
        compiler_params=_params(("parallel", "parallel")),
        name="win_attn",
    )(sink, qa, ka, ka, ka, va, va, va, g_win)


def _dil_attn(qv, kv, vv, dil, tq):
    b, ln, wr = qv.shape
    side = DIL_SIDE
    lse_w = DIL_SLOTS if dil == 1 else LANES
    prev, main, nxt = _band_specs(tq, side, ln, wr)
    return pl.pallas_call(
        functools.partial(_dil_kernel, dil, tq),
        grid=(b, ln // tq),
        in_specs=[main, prev, main, nxt, prev, main, nxt],
        out_specs=[main, pl.BlockSpec((1, dil, tq, lse_w), lambda bi, i: (bi, 0, i, 0))],
        out_shape=[jax.ShapeDtypeStruct((b, ln, wr), BF16),
                   jax.ShapeDtypeStruct((b, dil, ln, lse_w), F32)],
        compiler_params=_params(("parallel", "parallel")),
        name=f"dil_attn_{dil}",
    )(qv, kv, kv, kv, vv, vv, vv)


def _mem_kernel(mem_ref, g_ref, b_ref, wk_ref, wv_ref, k_ref, v_ref):
    mn = _ln(mem_ref[0], g_ref[...], b_ref[...]).astype(BF16)
    k_ref[0] = _dot(mn, wk_ref[...]).astype(BF16)
    v_ref[0] = _dot(mn, wv_ref[...]).astype(BF16)


def _mem_kv(mem, g, b, wk, wv):
    bsz, m, d = mem.shape
    blk = pl.BlockSpec((1, m, d), lambda bi: (bi, 0, 0))
    return pl.pallas_call(
        _mem_kernel,
        grid=(bsz,),
        in_specs=[blk, _const_spec((1, d)), _const_spec((1, d)), _const_spec((d, d)), _const_spec((d, d))],
        out_specs=[blk, blk],
        out_shape=[jax.ShapeDtypeStruct((bsz, m, d), BF16)] * 2,
        compiler_params=_params(("parallel",)),
        name="mem_kv",
    )(mem, g, b, wk, wv)


def _mix_kernel(tm, oa_ref, o0_ref, o1_ref, o2_ref, l0_ref, l1_ref, l2_ref, h_ref, e_ref, gd_ref, wo_ref,
                g1_ref, b1_ref, wq_ref, kx_ref, vx_ref, wxo_ref, g2_ref, b2_ref, out_ref, stage, lstage):
    ts = tm // N_SUB
    n_chunks = DIL_WIDTH // LANES

    def sub_tile(k):
        rows = slice(k * ts, (k + 1) * ts)
        ls = []
        for gi, (l_ref, (_, dil)) in enumerate(zip((l0_ref, l1_ref, l2_ref), DIL_PAIRS)):
            if dil == 1:
                ls.append(l_ref[0, rows, :])
            else:
                rrows = slice(k * ts // dil, (k + 1) * ts // dil)
                for c in range(dil):
                    lstage[gi - 1, pl.ds(k * ts + c, ts // dil, stride=dil), :] = l_ref[0, c, rrows, :]
                ls.append(lstage[gi - 1, rows, 0:DIL_SLOTS])
        mx = jnp.maximum(jnp.maximum(ls[0], ls[1]), ls[2])
        es = [jnp.exp2(l - mx) for l in ls]
        tot = es[0] + es[1] + es[2]
        ob = None
        for gi, (e, o_ref, (_, dil)) in enumerate(zip(es, (o0_ref, o1_ref, o2_ref), DIL_PAIRS)):
            wexp = _dot((e / tot).astype(BF16), e_ref[...])
            if dil == 1:
                og = o_ref[0, rows, :].astype(F32)
            else:
                rrows = slice(k * ts // dil, (k + 1) * ts // dil)
                for c in range(dil):
                    for j in range(n_chunks):
                        col = c * DIL_WIDTH + j * LANES
                        stage[gi - 1, j, pl.ds(k * ts + c, ts // dil, stride=dil), :] = (
                            o_ref[0, rrows, col:col + LANES].astype(F32))
                og = jnp.concatenate([stage[gi - 1, j, rows, :] for j in range(n_chunks)], -1)
            term = wexp * og
            ob = term if ob is None else ob + term
        obn = _rms(ob, gd_ref[...]).astype(BF16)
        yield
        mix = _dot(oa_ref[0, rows, :], wo_ref[0:WIN_WIDTH, :]) + _dot(obn, wo_ref[WIN_WIDTH:, :])
        yield
        h1 = _ln(DEEPNORM_ALPHA * h_ref[0, rows, :] + mix, g1_ref[...], b1_ref[...])
        yield
        q = _dot(h1.astype(BF16), wq_ref[...]).astype(BF16)
        yield
        heads = []
        for hh in range(X_HEADS):
            cols = slice(hh * X_HEAD_DIM, (hh + 1) * X_HEAD_DIM)
            s = _dot_nt(q[:, cols], kx_ref[0, :, cols])
            yield
            m = jnp.max(s, -1, keepdims=True)
            p = jnp.exp(s - m)
            denom = jnp.sum(p, -1, keepdims=True)
            yield
            heads.append((_dot(p.astype(BF16), vx_ref[0, :, cols]) / denom).astype(BF16))
        xa = _dot(jnp.concatenate(heads, -1), wxo_ref[...])
        yield
        out_ref[0, rows, :] = _ln(DEEPNORM_ALPHA * h1 + xa, g2_ref[...], b2_ref[...])

    _round_robin([sub_tile(k) for k in range(N_SUB)])


def _mix_xattn(oa, os_, ls, h, expand, g_dil, w_out, g1, b1, w_xq, kx, vx, w_xo, g2, b2, tm):
    b, s, d = h.shape
    m = kx.shape[1]
    tok = lambda w: pl.BlockSpec((1, tm, w), lambda bi, i: (bi, i, 0))
    res = lambda dil: pl.BlockSpec((1, tm // dil, dil * DIL_WIDTH), lambda bi, i: (bi, i, 0))
    memspec = pl.BlockSpec((1, m, d), lambda bi, i: (bi, 0, 0))
    return pl.pallas_call(
        functools.partial(_mix_kernel, tm),
        grid=(b, s // tm),
        in_specs=[tok(WIN_WIDTH)] + [res(dil) for _, dil in DIL_PAIRS] +
                 [tok(DIL_SLOTS) if dil == 1 else
                  pl.BlockSpec((1, dil, tm // dil, LANES), lambda bi, i: (bi, 0, i, 0)) for _, dil in DIL_PAIRS] +
                 [tok(d),
                  _const_spec(expand.shape), _const_spec((1, DIL_WIDTH)), _const_spec((d, d)),
                  _const_spec((1, d)), _const_spec((1, d)), _const_spec((d, d)), memspec, memspec,
                  _const_spec((d, d)), _const_spec((1, d)), _const_spec((1, d))],
        out_specs=tok(d),
        out_shape=jax.ShapeDtypeStruct((b, s, d), F32),
        scratch_shapes=[pltpu.VMEM((N_DIL - 1, DIL_WIDTH // LANES, tm, LANES), F32),
                        pltpu.VMEM((N_DIL - 1, tm, LANES), F32)],
        compiler_params=_params(("parallel", "parallel")),
        name="mix_xattn",
    )(oa, *os_, *ls, h, expand, g_dil, w_out, g1, b1, w_xq, kx, vx, w_xo, g2, b2)


FF_CHUNKS = ((0, D_FF),)
HALO = 16
CONV_PHASES = 4


def _ffn_kernel(tm, hp_ref, hm_ref, hn_ref, wg_ref, wu_ref, cw_ref, cb_ref, wd_ref, g3_ref, b3_ref, out_ref,
                gs, ys):
    i = pl.program_id(1)
    n = pl.num_programs(1)
    hm = hm_ref[0]
    hb = hm.astype(BF16)
    hp = jnp.where(i > 0, hp_ref[0], 0.0).astype(BF16)
    hn = jnp.where(i < n - 1, hn_ref[0], 0.0).astype(BF16)
    hcat = jnp.concatenate([hp, hb, hn], 0)
    q = tm // CONV_PHASES
    for sl in range(D_MODEL // LANES):
        ys[sl] = hm[:, sl * LANES:(sl + 1) * LANES]
    hb_phased = jnp.concatenate(
        [jnp.concatenate([ys[sl, pl.ds(ph, q, stride=CONV_PHASES), :] for sl in range(D_MODEL // LANES)], -1)
         for ph in range(CONV_PHASES)], 0).astype(BF16)
    acc = None
    for c0, width in FF_CHUNKS:
        gate = _dot(hcat, wg_ref[:, c0:c0 + width])
        up = _dot(hb_phased, wu_ref[:, c0:c0 + width])
        n_slabs = width // LANES
        for sl in range(n_slabs):
            gs[sl] = gate[:, sl * LANES:(sl + 1) * LANES]
        phases = []
        for ph in range(CONV_PHASES):
            slabs = []
            for sl in range(n_slabs):
                lanes = slice(c0 + sl * LANES, c0 + (sl + 1) * LANES)
                g = cb_ref[:, lanes]
                for j in range(CONV_WIDTH):
                    g = g + gs[sl, pl.ds(HALO - 1 + j + ph, q, stride=CONV_PHASES), :] * cw_ref[j:j + 1, lanes]
                u = up[ph * q:(ph + 1) * q, sl * LANES:(sl + 1) * LANES]
                slabs.append((0.5 * g * (1.0 + lax.erf(g * np.float32(np.sqrt(0.5)))) * u).astype(BF16))
            phases.append(jnp.concatenate(slabs, -1))
        part = _dot(jnp.concatenate(phases, 0), wd_ref[c0:c0 + width, :])
        acc = part if acc is None else acc + part
    for ph in range(CONV_PHASES):
        for sl in range(D_MODEL // LANES):
            ys[sl, pl.ds(ph, q, stride=CONV_PHASES), :] = acc[ph * q:(ph + 1) * q, sl * LANES:(sl + 1) * LANES]
    ff = jnp.concatenate([ys[sl] for sl in range(D_MODEL // LANES)], -1)
    out_ref[0] = _ln(DEEPNORM_ALPHA * hm + ff, g3_ref[...], b3_ref[...])


def _conv_glu(h2, wg, wu, cw, cb, wd, g3, b3, tm):
    b, s, d = h2.shape
    per = tm // HALO
    nblk = s // HALO
    main = pl.BlockSpec((1, tm, d), lambda bi, i: (bi, i, 0))
    prev = pl.BlockSpec((1, HALO, d), lambda bi, i: (bi, jnp.maximum(i * per - 1, 0), 0))
    nxt = pl.BlockSpec((1, HALO, d), lambda bi, i: (bi, jnp.minimum((i + 1) * per, nblk - 1), 0))
    n_slabs = max(w for _, w in FF_CHUNKS) // LANES
    return pl.pallas_call(
        functools.partial(_ffn_kernel, tm),
        grid=(b, s // tm),
        in_specs=[prev, main, nxt, _const_spec((d, D_FF)), _const_spec((d, D_FF)),
                  _const_spec((CONV_WIDTH, D_FF)), _const_spec((1, D_FF)), _const_spec((D_FF, d)),
                  _const_spec((1, d)), _const_spec((1, d))],
        out_specs=main,
        out_shape=jax.ShapeDtypeStruct((b, s, d), F32),
        scratch_shapes=[pltpu.VMEM((n_slabs, tm + 2 * HALO, LANES), F32), pltpu.VMEM((d // LANES, tm, LANES), F32)],
        compiler_params=_params(("parallel", "parallel")),
        name="conv_glu",
    )(h2, h2, h2, wg, wu, cw, cb, wd, g3, b3)


def kernel(x, mem, positions, ln_in_g, ln_in_b, w_in, attn_sink, g_win, g_dil, w_mix_out, ln1_g, ln1_b, mem_ln_g, mem_ln_b, w_xq, w_xk, w_xv, w_xo, ln2_g, ln2_b, w_gate, w_up, conv_w, conv_b, w_down, ln3_g, ln3_b):
    b, s, d = x.shape
    assert DEPTH == 1
    l = 0
    row = lambda v: v.reshape(1, -1)
    cos, sin = _rope_tables(positions)

    qscale = HEAD_DIM ** -0.5 * LOG2E
    o_ka, o_va, o_qb = A_Q, A_Q + A_KV, A_Q + 2 * A_KV
    o_kb, o_vb = o_qb + B_QKV, o_qb + 2 * B_QKV
    names = ["qa", "ka", "va"]
    segs = [(0, A_Q, True, 1, qscale, False),
            (o_ka, A_KV, True, 1, 1.0, True),
            (o_va, A_KV, False, 1, 1.0, True)]
    for kind, base, rope, scale in (("q", o_qb, True, qscale), ("k", o_kb, True, 1.0), ("v", o_vb, False, 1.0)):
        for gi, (_, dil) in enumerate(DIL_PAIRS):
            names.append(f"{kind}{gi}")
            segs.append((base + gi * DIL_WIDTH, DIL_WIDTH, rope, dil, scale, False))

    xscale = X_HEAD_DIM ** -0.5
    to_cast = [(w_mix_out[l], 1.0), (w_xq[l], xscale), (w_xk[l], 1.0), (w_xv[l], 1.0), (w_xo[l], 1.0),
               (w_gate[l], 1.0), (w_up[l], 1.0), (w_down[l], 1.0)]
    outs = _qkv_proj(x.reshape(b * s, d), row(ln_in_g), row(ln_in_b), w_in[l].astype(BF16), tuple(segs),
                     cos, sin, to_cast, tm=QKV_TILE)
    n_seg = 1 + len(segs)
    wo_b, wq_b, wk_b, wv_b, wxo_b, wg_b, wu_b, wd_b = outs[n_seg:]
    hn = outs[0].reshape(b, s, d)
    proj = {name: t.reshape(b, t.shape[0] // b, t.shape[1]) for name, t in zip(names, outs[1:n_seg])}
    qa, ka, va = proj["qa"], proj["ka"], proj["va"]
    oa = _win_attn(qa, ka, va, attn_sink[l], row(g_win[l]), tq=WIN_TILE)
    os_, ls = [], []
    for gi, (_, dil) in enumerate(DIL_PAIRS):
        qg, kg, vg = proj[f"q{gi}"], proj[f"k{gi}"], proj[f"v{gi}"]
        o, lse = _dil_attn(qg, kg, vg, dil, tq=DIL_TOKENS // dil)
        os_.append(o)
        ls.append(lse.reshape(b, s, DIL_SLOTS) if dil == 1 else lse)

    kx, vx = _mem_kv(mem, row(mem_ln_g[l]), row(mem_ln_b[l]), wk_b, wv_b)
    expand = jnp.repeat(jnp.eye(DIL_SLOTS, dtype=BF16), HEAD_DIM, axis=1)
    h2 = _mix_xattn(oa, os_, ls, hn, expand, row(g_dil[l]), wo_b, row(ln1_g[l]), row(ln1_b[l]), wq_b, kx, vx,
                    wxo_b, row(ln2_g[l]), row(ln2_b[l]), tm=MIX_TILE)
    return _conv_glu(h2, wg_b, wu_b, conv_w[l], row(conv_b[l]), wd_b, row(ln3_g[l]), row(ln3_b[l]), tm=FFN_TILE)
```

```python
import functools

import numpy as np
import jax
import jax.numpy as jnp
from jax import lax
from jax.experimental import pallas as pl
from jax.experimental.pallas import tpu as pltpu

D_MODEL = 1024
HEAD_DIM = 64
WIN_Q_HEADS = 8
WIN_KV_HEADS = 2
WIN_HALF = 128
DIL_SLOTS = 8
DIL_PAIRS = ((128, 1), (512, 4), (2048, 16))
N_DIL = len(DIL_PAIRS)
ROT_DIM = HEAD_DIM // 4
ROT_HALF = ROT_DIM // 2
ROPE_THETA = 500000.0
X_HEADS = 4
X_HEAD_DIM = D_MODEL // X_HEADS
D_FF = 2816
CONV_WIDTH = 3
WIN_WIDTH = WIN_Q_HEADS * HEAD_DIM
DIL_WIDTH = DIL_SLOTS * HEAD_DIM
A_Q = WIN_WIDTH
A_KV = WIN_KV_HEADS * HEAD_DIM
B_QKV = N_DIL * DIL_WIDTH
DEPTH = 1
DEEPNORM_ALPHA = (2 * DEPTH) ** 0.25
LN_EPS = 1e-5
NEG_INF = -1e30
LOG2E = float(np.log2(np.e))

LANES = 128
TOK_PER_ROW = LANES // ROT_HALF
Q_BLK = 128
N_SUB = 4
DIL_SIDE = 64
assert all(window // (2 * dil) == DIL_SIDE for window, dil in DIL_PAIRS)
VMEM_LIMIT = 56 * 1024 * 1024
QKV_TILE = 512
WIN_TILE = 2048
DIL_TOKENS = 2048
MIX_TILE = 1024
FFN_TILE = 512

F32 = jnp.float32
BF16 = jnp.bfloat16


def _ln(x, g, b):
    mu = jnp.mean(x, -1, keepdims=True)
    xc = x - mu
    var = jnp.mean(xc * xc, -1, keepdims=True)
    return xc * lax.rsqrt(var + LN_EPS) * g + b


def _rms(x, g):
    return x * lax.rsqrt(jnp.mean(x * x, -1, keepdims=True) + LN_EPS) * g


def _dot(a, b):
    return jnp.dot(a, b, preferred_element_type=F32)


def _dot_nt(a, b):
    return lax.dot_general(a, b, (((1,), (1,)), ((), ())), preferred_element_type=F32)


def _const_spec(shape):
    nd = len(shape)
    return pl.BlockSpec(shape, lambda *_: (0,) * nd, pipeline_mode=pl.Buffered(1))


def _params(sem, fuse_inputs=None):
    return pltpu.CompilerParams(dimension_semantics=sem, vmem_limit_bytes=VMEM_LIMIT, allow_input_fusion=fuse_inputs)


def _round_robin(gens):
    live = list(gens)
    while live:
        for g in list(live):
            try:
                next(g)
            except StopIteration:
                live.remove(g)


def _rope_tab_kernel(pos_ref, invf_ref, cos_ref, sin_ref):
    ang = pos_ref[...].astype(F32) * invf_ref[...]
    cos_ref[...] = jnp.cos(ang)
    sin_ref[...] = jnp.sin(ang)


def _rope_tables(positions):
    inv_freq = ROPE_THETA ** (-jnp.arange(0, ROT_DIM, 2, dtype=F32) / ROT_DIM)
    pos_rep = jnp.repeat(positions.reshape(-1), ROT_HALF).reshape(-1, LANES)
    invf = jnp.tile(inv_freq, TOK_PER_ROW).reshape(1, LANES)
    return pl.pallas_call(
        _rope_tab_kernel,
        out_shape=(jax.ShapeDtypeStruct(pos_rep.shape, F32),) * 2,
        name="rope_tables",
    )(pos_rep, invf)


def _expand_rope(tab, tm):
    rows = jnp.broadcast_to(tab[:, None, :], (tm // TOK_PER_ROW, TOK_PER_ROW, LANES)).reshape(tm, LANES)
    tok = lax.broadcasted_iota(jnp.int32, (tm, LANES), 0) % TOK_PER_ROW
    lane = lax.broadcasted_iota(jnp.int32, (tm, LANES), 1)
    idx = tok * ROT_HALF + lane % ROT_HALF
    return jnp.take_along_axis(rows, idx, axis=1, mode="promise_in_bounds")


QKV_DOT_WIDTH = 1536


def _run_width(segs, c0):
    starts = {s[0]: s[1] for s in segs}
    width = 0
    while c0 + width in starts and width + starts[c0 + width] <= QKV_DOT_WIDTH:
        width += starts[c0 + width]
    return width


def _qkv_kernel(segs, tm, scales, x_ref, g_ref, b_ref, w_ref, cos_ref, sin_ref, *rest):
    n_cast = len(scales)
    cast_in, rest = rest[:n_cast], rest[n_cast:]
    h_ref, out_refs, cast_out, stage = rest[0], rest[1:1 + len(segs)], rest[1 + len(segs):-1], rest[-1]
    h = _ln(x_ref[...], g_ref[...], b_ref[...])
    h_ref[...] = h
    hb = h.astype(BF16)
    for scale, wi_ref, wo_ref in zip(scales, cast_in, cast_out):
        wo_ref[...] = (wi_ref[...] * scale if scale != 1.0 else wi_ref[...]).astype(BF16)
    slot = lax.broadcasted_iota(jnp.int32, (tm, LANES), 1) % HEAD_DIM
    cos = _expand_rope(cos_ref[...], tm)
    sin = _expand_rope(sin_ref[...], tm)
    is_t1 = slot < ROT_HALF
    c_tab = jnp.where(slot < ROT_DIM, cos, 1.0)
    s_tab = jnp.where(is_t1, -sin, jnp.where(slot < ROT_DIM, sin, 0.0))
    tables = {scale: (c_tab, s_tab) if scale == 1.0 else (c_tab * scale, s_tab * scale)
              for scale in sorted({seg[4] for seg in segs if seg[2]})}
    lo = slot == lax.broadcasted_iota(jnp.int32, (tm, LANES), 1)
    z, z0 = None, 0
    for (c0, width, rope, dil, scale, dup), o_ref in zip(segs, out_refs):
        if z is None or c0 + width > z0 + z.shape[1]:
            z0 = c0
            z = _dot(hb, w_ref[:, z0:z0 + _run_width(segs, c0)])
        for j in range(width // LANES):
            zc = z[:, c0 - z0 + j * LANES:c0 - z0 + (j + 1) * LANES]
            if rope:
                up = pltpu.roll(zc, LANES - ROT_HALF, 1)
                dn = pltpu.roll(zc, ROT_HALF, 1)
                c_s, s_s = tables[scale]
                zc = zc * c_s + jnp.where(is_t1, up, dn) * s_s
            elif scale != 1.0:
                zc = zc * scale
            if dup:
                sw = pltpu.roll(zc, HEAD_DIM, 1)
                o_ref[:, 2 * j * LANES:(2 * j + 1) * LANES] = jnp.where(lo, zc, sw).astype(BF16)
                o_ref[:, (2 * j + 1) * LANES:(2 * j + 2) * LANES] = jnp.where(lo, sw, zc).astype(BF16)
            elif dil == 1:
                o_ref[:, j * LANES:(j + 1) * LANES] = zc.astype(BF16)
            else:
                stage[...] = zc
                for c in range(dil):
                    col = c * width + j * LANES
                    o_ref[:, col:col + LANES] = stage[pl.ds(c, tm // dil, stride=dil), :].astype(BF16)


BF16_ROWS = 16


def _qkv_proj(x2, ln_g, ln_b, w_all, segs, cos, sin, to_cast, tm):
    t = x2.shape[0]
    steps = t // tm
    wtot = w_all.shape[1]
    row = lambda r, w: pl.BlockSpec((r, w), lambda i: (i, 0))
    out_shape = [jax.ShapeDtypeStruct((t, D_MODEL), F32)]
    out_specs = [row(tm, D_MODEL)]
    for _, width, _, dil, _, dup in segs:
        wout = 2 * width if dup else dil * width
        out_shape.append(jax.ShapeDtypeStruct((t // dil, wout), BF16))
        out_specs.append(row(tm // dil, wout))
    cast_specs = []
    for w, _ in to_cast:
        rows, cols = w.shape
        nblk = max(n for n in range(1, steps + 1) if steps % n == 0 and rows % (n * BF16_ROWS) == 0)
        spec = pl.BlockSpec((rows // nblk, cols), lambda i, every=steps // nblk: (i // every, 0))
        cast_specs.append(spec)
        out_shape.append(jax.ShapeDtypeStruct((rows, cols), BF16))
        out_specs.append(spec)
    return pl.pallas_call(
        functools.partial(_qkv_kernel, segs, tm, tuple(s for _, s in to_cast)),
        grid=(steps,),
        in_specs=[row(tm, D_MODEL), _const_spec((1, D_MODEL)), _const_spec((1, D_MODEL)),
                  _const_spec((D_MODEL, wtot)), row(tm // TOK_PER_ROW, LANES), row(tm // TOK_PER_ROW, LANES)]
                 + cast_specs,
        out_specs=out_specs,
        out_shape=out_shape,
        scratch_shapes=[pltpu.VMEM((tm, LANES), F32)],
        compiler_params=_params(("arbitrary",), fuse_inputs=[k == 3 for k in range(6 + len(to_cast))]),
        name="qkv_proj",
    )(x2, ln_g, ln_b, w_all, cos, sin, *[w for w, _ in to_cast])


def _band_bias(nq, nk, side):
    qi = lax.broadcasted_iota(jnp.int32, (nq, nk), 0)
    kj = lax.broadcasted_iota(jnp.int32, (nq, nk), 1)
    rel = kj - side - qi
    return jnp.where((rel >= -side) & (rel <= side), 0.0, NEG_INF).astype(F32)


def _halo_rows(parts, lo, hi, cols):
    prev, main, nxt = parts
    side, tq = prev.shape[0], main.shape[0]
    pieces = []
    if lo < 0:
        pieces.append(prev[side + lo:side, cols])
    pieces.append(main[max(lo, 0):min(hi, tq), cols])
    if hi > tq:
        pieces.append(nxt[0:hi - tq, cols])
    return pieces[0] if len(pieces) == 1 else jnp.concatenate(pieces, 0)


def _band_units(side, dil, tq, tile, n_tiles, sink_ref, q_ref, k_parts, v_parts, emit, lse_w=0):
    is_win = sink_ref is not None
    blk = Q_BLK
    nk = blk + 2 * side
    n_blk = tq // blk
    width = q_ref.shape[-1] // dil
    n_pairs = width // LANES
    lo = lax.broadcasted_iota(jnp.int32, (blk, LANES), 1) < HEAD_DIM
    lane8 = lax.broadcasted_iota(jnp.int32, (blk, max(lse_w, 1)), 1)
    band = _band_bias(blk, nk, side)
    kj = lax.broadcasted_iota(jnp.int32, (blk, nk), 1)
    band_first = jnp.where((tile == 0) & (kj < side), NEG_INF, band)
    band_last = jnp.where((tile == n_tiles - 1) & (kj >= side + blk), NEG_INF, band)
    biases = []
    for j in range(n_blk):
        bias = band
        if j == 0:
            bias = band_first
        if j == n_blk - 1:
            bias = band_last if j > 0 else jnp.minimum(band_first, band_last)
        biases.append(jnp.concatenate([bias, bias], 0))

    units = [(j, c, pr) for j in range(n_blk) for c in range(dil) for pr in range(n_pairs)]

    def kv_cols(c, pr):
        if is_win:
            return slice((pr // 2) * LANES, (pr // 2 + 1) * LANES)
        return slice(c * width + pr * LANES, c * width + (pr + 1) * LANES)

    def scores(j, c, pr):
        r0 = j * blk
        q2 = q_ref[r0:r0 + blk, c * width + pr * LANES:c * width + (pr + 1) * LANES]
        zero = jnp.zeros_like(q2)
        qs = jnp.concatenate([jnp.where(lo, q2, zero), jnp.where(lo, zero, q2)], 0)
        keys = _halo_rows(k_parts, r0 - side, r0 + blk + side, kv_cols(c, pr))
        return _dot_nt(qs, keys) + biases[j]

    def softmax(s, pr):
        ps, rdens, lses = [], [], []
        for hf in range(2):
            sh = s[hf * blk:(hf + 1) * blk]
            m = jnp.max(sh, -1, keepdims=True)
            if is_win:
                sink = sink_ref[2 * pr + hf] * LOG2E
                m = jnp.maximum(m, sink)
            ph = jnp.exp2(sh - m)
            denom = jnp.sum(ph, -1, keepdims=True)
            if is_win:
                denom = denom + jnp.exp2(sink - m)
            else:
                lses.append(m + jnp.log2(denom))
            ps.append(ph.astype(BF16))
            rdens.append(1.0 / denom)
        return jnp.concatenate(ps, 0), rdens, lses

    pairs, lse8 = [], None

    def finish(unit, o, rdens, lses):
        nonlocal pairs, lse8
        j, c, pr = unit
        pairs.append(jnp.where(lo, o[:blk] * rdens[0], o[blk:] * rdens[1]))
        if lse_w:
            if lse8 is None:
                lse8 = jnp.zeros((blk, lse_w), F32)
            lse8 = jnp.where(lane8 == 2 * pr, lses[0], lse8)
            lse8 = jnp.where(lane8 == 2 * pr + 1, lses[1], lse8)
        if pr == n_pairs - 1:
            emit(j, c, jnp.concatenate(pairs, -1), lse8)
            pairs, lse8 = [], None

    s_next = scores(*units[0])
    pending = None
    for n, (j, c, pr) in enumerate(units):
        s = s_next
        if n + 1 < len(units):
            s_next = scores(*units[n + 1])
        p, rdens, lses = softmax(s, pr)
        o = _dot(p, _halo_rows(v_parts, j * blk - side, (j + 1) * blk + side, kv_cols(c, pr)))
        if pending is not None:
            finish(*pending)
        pending = ((j, c, pr), o, rdens, lses)
    finish(*pending)


def _dil_kernel(dil, tq, q_ref, kp_ref, km_ref, kn_ref, vp_ref, vm_ref, vn_ref, o_ref, lse_ref):
    width = q_ref.shape[-1] // dil

    def emit(j, c, out, lse):
        rows = slice(j * Q_BLK, (j + 1) * Q_BLK)
        o_ref[0, rows, c * width:(c + 1) * width] = out.astype(BF16)
        lse_ref[0, c, rows, :] = lse

    _band_units(DIL_SIDE, dil, tq, pl.program_id(1), pl.num_programs(1), None, q_ref.at[0],
                (kp_ref.at[0], km_ref.at[0], kn_ref.at[0]), (vp_ref.at[0], vm_ref.at[0], vn_ref.at[0]), emit,
                lse_w=lse_ref.shape[-1])


def _band_specs(tq, side, n_rows, width):
    per = tq // side
    nblk = n_rows // side
    main = pl.BlockSpec((1, tq, width), lambda bi, i: (bi, i, 0))
    prev = pl.BlockSpec((1, side, width), lambda bi, i: (bi, jnp.maximum(i * per - 1, 0), 0))
    nxt = pl.BlockSpec((1, side, width), lambda bi, i: (bi, jnp.minimum((i + 1) * per, nblk - 1), 0))
    return prev, main, nxt


def _win_kernel(tq, sink_ref, q_ref, kp_ref, km_ref, kn_ref, vp_ref, vm_ref, vn_ref, g_ref, o_ref):

    def emit(j, c, out, lse):
        o_ref[0, j * Q_BLK:(j + 1) * Q_BLK, :] = _rms(out, g_ref[...]).astype(BF16)

    _band_units(WIN_HALF, 1, tq, pl.program_id(1), pl.num_programs(1), sink_ref, q_ref.at[0],
                (kp_ref.at[0], km_ref.at[0], kn_ref.at[0]), (vp_ref.at[0], vm_ref.at[0], vn_ref.at[0]), emit)


def _win_attn(qa, ka, va, sink, g_win, tq):
    b, s, _ = qa.shape
    side = WIN_HALF
    kw = ka.shape[-1]
    prev, main, nxt = _band_specs(tq, side, s, kw)
    qspec = pl.BlockSpec((1, tq, WIN_WIDTH), lambda bi, i: (bi, i, 0))
    return pl.pallas_call(
        functools.partial(_win_kernel, tq),
        grid=(b, s // tq),
        in_specs=[pl.BlockSpec(memory_space=pltpu.SMEM), qspec, prev, main, nxt, prev, main, nxt,
                  _const_spec((1, WIN_WIDTH))],
        out_specs=qspec,
        out_shape=jax.ShapeDtypeStruct((b, s, WIN_WIDTH), BF16),
        compiler_params=_params(("parallel", "parallel")),
        name="win_attn",
    )(sink, qa, ka, ka, ka, va, va, va, g_win)


def _dil_attn(qv, kv, vv, dil, tq):
    b, ln, wr = qv.shape
    side = DIL_SIDE
    lse_w = DIL_SLOTS if dil == 1 else LANES
    prev, main, nxt = _band_specs(tq, side, ln, wr)
    return pl.pallas_call(
        functools.partial(_dil_kernel, dil, tq),
        grid=(b, ln // tq),
        in_specs=[main, prev, main, nxt, prev, main, nxt],
        out_specs=[main, pl.BlockSpec((1, dil, tq, lse_w), lambda bi, i: (bi, 0, i, 0))],
        out_shape=[jax.ShapeDtypeStruct((b, ln, wr), BF16),
                   jax.ShapeDtypeStruct((b, dil, ln, lse_w), F32)],
        compiler_params=_params(("parallel", "parallel")),
        name=f"dil_attn_{dil}",
    )(qv, kv, kv, kv, vv, vv, vv)


def _mem_kernel(mem_ref, g_ref, b_ref, wk_ref, wv_ref, k_ref, v_ref):
    mn = _ln(mem_ref[0], g_ref[...], b_ref[...]).astype(BF16)
    k_ref[0] = _dot(mn, wk_ref[...]).astype(BF16)
    v_ref[0] = _dot(mn, wv_ref[...]).astype(BF16)


def _mem_kv(mem, g, b, wk, wv):
    bsz, m, d = mem.shape
    blk = pl.BlockSpec((1, m, d), lambda bi: (bi, 0, 0))
    return pl.pallas_call(
        _mem_kernel,
        grid=(bsz,),
        in_specs=[blk, _const_spec((1, d)), _const_spec((1, d)), _const_spec((d, d)), _const_spec((d, d))],
        out_specs=[blk, blk],
        out_shape=[jax.ShapeDtypeStruct((bsz, m, d), BF16)] * 2,
        compiler_params=_params(("parallel",)),
        name="mem_kv",
    )(mem, g, b, wk, wv)


def _mix_kernel(tm, oa_ref, o0_ref, o1_ref, o2_ref, l0_ref, l1_ref, l2_ref, h_ref, e_ref, gd_ref, wo_ref,
                g1_ref, b1_ref, wq_ref, kx_ref, vx_ref, wxo_ref, g2_ref, b2_ref, out_ref, stage, lstage):
    ts = tm // N_SUB
    n_chunks = DIL_WIDTH // LANES

    def sub_tile(k):
        rows = slice(k * ts, (k + 1) * ts)
        ls = []
        for gi, (l_ref, (_, dil)) in enumerate(zip((l0_ref, l1_ref, l2_ref), DIL_PAIRS)):
            if dil == 1:
                ls.append(l_ref[0, rows, :])
            else:
                rrows = slice(k * ts // dil, (k + 1) * ts // dil)
                for c in range(dil):
                    lstage[gi - 1, pl.ds(k * ts + c, ts // dil, stride=dil), :] = l_ref[0, c, rrows, :]
                ls.append(lstage[gi - 1, rows, 0:DIL_SLOTS])
        mx = jnp.maximum(jnp.maximum(ls[0], ls[1]), ls[2])
        es = [jnp.exp2(l - mx) for l in ls]
        tot = es[0] + es[1] + es[2]
        ob = None
        for gi, (e, o_ref, (_, dil)) in enumerate(zip(es, (o0_ref, o1_ref, o2_ref), DIL_PAIRS)):
            wexp = _dot((e / tot).astype(BF16), e_ref[...])
            if dil == 1:
                og = o_ref[0, rows, :].astype(F32)
            else:
                rrows = slice(k * ts // dil, (k + 1) * ts // dil)
                for c in range(dil):
                    for j in range(n_chunks):
                        col = c * DIL_WIDTH + j * LANES
                        stage[gi - 1, j, pl.ds(k * ts + c, ts // dil, stride=dil), :] = (
                            o_ref[0, rrows, col:col + LANES].astype(F32))
                og = jnp.concatenate([stage[gi - 1, j, rows, :] for j in range(n_chunks)], -1)
            term = wexp * og
            ob = term if ob is None else ob + term
        obn = _rms(ob, gd_ref[...]).astype(BF16)
        yield
        mix = _dot(oa_ref[0, rows, :], wo_ref[0:WIN_WIDTH, :]) + _dot(obn, wo_ref[WIN_WIDTH:, :])
        yield
        h1 = _ln(DEEPNORM_ALPHA * h_ref[0, rows, :] + mix, g1_ref[...], b1_ref[...])
        yield
        q = _dot(h1.astype(BF16), wq_ref[...]).astype(BF16)
        yield
        heads = []
        for hh in range(X_HEADS):
            cols = slice(hh * X_HEAD_DIM, (hh + 1) * X_HEAD_DIM)
            s = _dot_nt(q[:, cols], kx_ref[0, :, cols])
            yield
            m = jnp.max(s, -1, keepdims=True)
            p = jnp.exp(s - m)
            denom = jnp.sum(p, -1, keepdims=True)
            yield
            heads.append((_dot(p.astype(BF16), vx_ref[0, :, cols]) / denom).astype(BF16))
        xa = _dot(jnp.concatenate(heads, -1), wxo_ref[...])
        yield
        out_ref[0, rows, :] = _ln(DEEPNORM_ALPHA * h1 + xa, g2_ref[...], b2_ref[...])

    _round_robin([sub_tile(k) for k in range(N_SUB)])


def _mix_xattn(oa, os_, ls, h, expand, g_dil, w_out, g1, b1, w_xq, kx, vx, w_xo, g2, b2, tm):
    b, s, d = h.shape
    m = kx.shape[1]
    tok = lambda w: pl.BlockSpec((1, tm, w), lambda bi, i: (bi, i, 0))
    res = lambda dil: pl.BlockSpec((1, tm // dil, dil * DIL_WIDTH), lambda bi, i: (bi, i, 0))
    memspec = pl.BlockSpec((1, m, d), lambda bi, i: (bi, 0, 0))
    return pl.pallas_call(
        functools.partial(_mix_kernel, tm),
        grid=(b, s // tm),
        in_specs=[tok(WIN_WIDTH)] + [res(dil) for _, dil in DIL_PAIRS] +
                 [tok(DIL_SLOTS) if dil == 1 else
                  pl.BlockSpec((1, dil, tm // dil, LANES), lambda bi, i: (bi, 0, i, 0)) for _, dil in DIL_PAIRS] +
                 [tok(d),
                  _const_spec(expand.shape), _const_spec((1, DIL_WIDTH)), _const_spec((d, d)),
                  _const_spec((1, d)), _const_spec((1, d)), _const_spec((d, d)), memspec, memspec,
                  _const_spec((d, d)), _const_spec((1, d)), _const_spec((1, d))],
        out_specs=tok(d),
        out_shape=jax.ShapeDtypeStruct((b, s, d), F32),
        scratch_shapes=[pltpu.VMEM((N_DIL - 1, DIL_WIDTH // LANES, tm, LANES), F32),
                        pltpu.VMEM((N_DIL - 1, tm, LANES), F32)],
        compiler_params=_params(("parallel", "parallel")),
        name="mix_xattn",
    )(oa, *os_, *ls, h, expand, g_dil, w_out, g1, b1, w_xq, kx, vx, w_xo, g2, b2)


FF_CHUNKS = ((0, D_FF),)
HALO = 16
CONV_PHASES = 4


def _ffn_kernel(tm, hp_ref, hm_ref, hn_ref, wg_ref, wu_ref, cw_ref, cb_ref, wd_ref, g3_ref, b3_ref, out_ref,
                gs, ys):
    i = pl.program_id(1)
    n = pl.num_programs(1)
    hm = hm_ref[0]
    hb = hm.astype(BF16)
    hp = jnp.where(i > 0, hp_ref[0], 0.0).astype(BF16)
    hn = jnp.where(i < n - 1, hn_ref[0], 0.0).astype(BF16)
    hcat = jnp.concatenate([hp, hb, hn], 0)
    q = tm // CONV_PHASES
    for sl in range(D_MODEL // LANES):
        ys[sl] = hm[:, sl * LANES:(sl + 1) * LANES]
    hb_phased = jnp.concatenate(
        [jnp.concatenate([ys[sl, pl.ds(ph, q, stride=CONV_PHASES), :] for sl in range(D_MODEL // LANES)], -1)
         for ph in range(CONV_PHASES)], 0).astype(BF16)
    acc = None
    for c0, width in FF_CHUNKS:
        gate = _dot(hcat, wg_ref[:, c0:c0 + width])
        up = _dot(hb_phased, wu_ref[:, c0:c0 + width])
        n_slabs = width // LANES
        for sl in range(n_slabs):
            gs[sl] = gate[:, sl * LANES:(sl + 1) * LANES]
        phases = []
        for ph in range(CONV_PHASES):
            slabs = []
            for sl in range(n_slabs):
                lanes = slice(c0 + sl * LANES, c0 + (sl + 1) * LANES)
                g = cb_ref[:, lanes]
                for j in range(CONV_WIDTH):
                    g = g + gs[sl, pl.ds(HALO - 1 + j + ph, q, stride=CONV_PHASES), :] * cw_ref[j:j + 1, lanes]
                u = up[ph * q:(ph + 1) * q, sl * LANES:(sl + 1) * LANES]
                slabs.append((0.5 * g * (1.0 + lax.erf(g * np.float32(np.sqrt(0.5)))) * u).astype(BF16))
            phases.append(jnp.concatenate(slabs, -1))
        part = _dot(jnp.concatenate(phases, 0), wd_ref[c0:c0 + width, :])
        acc = part if acc is None else acc + part
    for ph in range(CONV_PHASES):
        for sl in range(D_MODEL // LANES):
            ys[sl, pl.ds(ph, q, stride=CONV_PHASES), :] = acc[ph * q:(ph + 1) * q, sl * LANES:(sl + 1) * LANES]
    ff = jnp.concatenate([ys[sl] for sl in range(D_MODEL // LANES)], -1)
    out_ref[0] = _ln(DEEPNORM_ALPHA * hm + ff, g3_ref[...], b3_ref[...])


def _conv_glu(h2, wg, wu, cw, cb, wd, g3, b3, tm):
    b, s, d = h2.shape
    per = tm // HALO
    nblk = s // HALO
    main = pl.BlockSpec((1, tm, d), lambda bi, i: (bi, i, 0))
    prev = pl.BlockSpec((1, HALO, d), lambda bi, i: (bi, jnp.maximum(i * per - 1, 0), 0))
    nxt = pl.BlockSpec((1, HALO, d), lambda bi, i: (bi, jnp.minimum((i + 1) * per, nblk - 1), 0))
    n_slabs = max(w for _, w in FF_CHUNKS) // LANES
    return pl.pallas_call(
        functools.partial(_ffn_kernel, tm),
        grid=(b, s // tm),
        in_specs=[prev, main, nxt, _const_spec((d, D_FF)), _const_spec((d, D_FF)),
                  _const_spec((CONV_WIDTH, D_FF)), _const_spec((1, D_FF)), _const_spec((D_FF, d)),
                  _const_spec((1, d)), _const_spec((1, d))],
        out_specs=main,
        out_shape=jax.ShapeDtypeStruct((b, s, d), F32),
        scratch_shapes=[pltpu.VMEM((n_slabs, tm + 2 * HALO, LANES), F32), pltpu.VMEM((d // LANES, tm, LANES), F32)],
        compiler_params=_params(("parallel", "parallel")),
        name="conv_glu",
    )(h2, h2, h2, wg, wu, cw, cb, wd, g3, b3)


def kernel(x, mem, positions, ln_in_g, ln_in_b, w_in, attn_sink, g_win, g_dil, w_mix_out, ln1_g, ln1_b, mem_ln_g, mem_ln_b, w_xq, w_xk, w_xv, w_xo, ln2_g, ln2_b, w_gate, w_up, conv_w, conv_b, w_down, ln3_g, ln3_b):
    b, s, d = x.shape
    assert DEPTH == 1
    l = 0
    row = lambda v: v.reshape(1, -1)
    cos, sin = _rope_tables(positions)

    qscale = HEAD_DIM ** -0.5 * LOG2E
    o_ka, o_va, o_qb = A_Q, A_Q + A_KV, A_Q + 2 * A_KV
    o_kb, o_vb = o_qb + B_QKV, o_qb + 2 * B_QKV
    names = ["qa", "ka", "va"]
    segs = [(0, A_Q, True, 1, qscale, False),
            (o_ka, A_KV, True, 1, 1.0, True),
            (o_va, A_KV, False, 1, 1.0, True)]
    for kind, base, rope, scale in (("q", o_qb, True, qscale), ("k", o_kb, True, 1.0), ("v", o_vb, False, 1.0)):
        for gi, (_, dil) in enumerate(DIL_PAIRS):
            names.append(f"{kind}{gi}")
            segs.append((base + gi * DIL_WIDTH, DIL_WIDTH, rope, dil, scale, False))

    xscale = X_HEAD_DIM ** -0.5
    to_cast = [(w_mix_out[l], 1.0), (w_xq[l], xscale), (w_xk[l], 1.0), (w_xv[l], 1.0), (w_xo[l], 1.0),
               (w_gate[l], 1.0), (w_up[l], 1.0), (w_down[l], 1.0)]
    outs = _qkv_proj(x.reshape(b * s, d), row(ln_in_g), row(ln_in_b), w_in[l].astype(BF16), tuple(segs),
                     cos, sin, to_cast, tm=QKV_TILE)
    n_seg = 1 + len(segs)
    wo_b, wq_b, wk_b, wv_b, wxo_b, wg_b, wu_b, wd_b = outs[n_seg:]
    hn = outs[0].reshape(b, s, d)
    proj = {name: t.reshape(b, t.shape[0] // b, t.shape[1]) for name, t in zip(names, outs[1:n_seg])}
    qa, ka, va = proj["qa"], proj["ka"], proj["va"]
    oa = _win_attn(qa, ka, va, attn_sink[l], row(g_win[l]), tq=WIN_TILE)
    os_, ls = [], []
    for gi, (_, dil) in enumerate(DIL_PAIRS):
        qg, kg, vg = proj[f"q{gi}"], proj[f"k{gi}"], proj[f"v{gi}"]
        o, lse = _dil_attn(qg, kg, vg, dil, tq=DIL_TOKENS // dil)
        os_.append(o)
        ls.append(lse.reshape(b, s, DIL_SLOTS) if dil == 1 else lse)

    kx, vx = _mem_kv(mem, row(mem_ln_g[l]), row(mem_ln_b[l]), wk_b, wv_b)
    expand = jnp.repeat(jnp.eye(DIL_SLOTS, dtype=BF16), HEAD_DIM, axis=1)
    h2 = _mix_xattn(oa, os_, ls, hn, expand, row(g_dil[l]), wo_b, row(ln1_g[l]), row(ln1_b[l]), wq_b, kx, vx,
                    wxo_b, row(ln2_g[l]), row(ln2_b[l]), tm=MIX_TILE)
    return _conv_glu(h2, wg_b, wu_b, conv_w[l], row(conv_b[l]), wd_b, row(ln3_g[l]), row(ln3_b[l]), tm=FFN_TILE)
```

```python
import functools

import numpy as np
import jax
import jax.numpy as jnp
from jax import lax
from jax.experimental import pallas as pl
from jax.experimental.pallas import tpu as pltpu

D_MODEL = 1024
HEAD_DIM = 64
WIN_Q_HEADS = 8
WIN_KV_HEADS = 2
WIN_HALF = 128
DIL_SLOTS = 8
DIL_PAIRS = ((128, 1), (512, 4), (2048, 16))
N_DIL = len(DIL_PAIRS)
ROT_DIM = HEAD_DIM // 4
ROT_HALF = ROT_DIM // 2
ROPE_THETA = 500000.0
X_HEADS = 4
X_HEAD_DIM = D_MODEL // X_HEADS
D_FF = 2816
CONV_WIDTH = 3
WIN_WIDTH = WIN_Q_HEADS * HEAD_DIM
DIL_WIDTH = DIL_SLOTS * HEAD_DIM
A_Q = WIN_WIDTH
A_KV = WIN_KV_HEADS * HEAD_DIM
B_QKV = N_DIL * DIL_WIDTH
DEPTH = 1
DEEPNORM_ALPHA = (2 * DEPTH) ** 0.25
LN_EPS = 1e-5
NEG_INF = -1e30
LOG2E = float(np.log2(np.e))

LANES = 128
TOK_PER_ROW = LANES // ROT_HALF
Q_BLK = 128
N_SUB = 4
DIL_SIDE = 64
assert all(window // (2 * dil) == DIL_SIDE for window, dil in DIL_PAIRS)
VMEM_LIMIT = 56 * 1024 * 1024
QKV_TILE = 512
WIN_TILE = 2048
DIL_TOKENS = 2048
MIX_TILE = 1024
FFN_TILE = 512

F32 = jnp.float32
BF16 = jnp.bfloat16


def _ln(x, g, b):
    mu = jnp.mean(x, -1, keepdims=True)
    xc = x - mu
    var = jnp.mean(xc * xc, -1, keepdims=True)
    return xc * lax.rsqrt(var + LN_EPS) * g + b


def _rms(x, g):
    return x * lax.rsqrt(jnp.mean(x * x, -1, keepdims=True) + LN_EPS) * g


def _dot(a, b):
    return jnp.dot(a, b, preferred_element_type=F32)


def _dot_nt(a, b):
    return lax.dot_general(a, b, (((1,), (1,)), ((), ())), preferred_element_type=F32)


def _const_spec(shape):
    nd = len(shape)
    return pl.BlockSpec(shape, lambda *_: (0,) * nd, pipeline_mode=pl.Buffered(1))


def _params(sem):
    return pltpu.CompilerParams(dimension_semantics=sem, vmem_limit_bytes=VMEM_LIMIT)


def _round_robin(gens):
    live = list(gens)
    while live:
        for g in list(live):
            try:
                next(g)
            except StopIteration:
                live.remove(g)


def _rope_tab_kernel(pos_ref, invf_ref, cos_ref, sin_ref):
    ang = pos_ref[...].astype(F32) * invf_ref[...]
    cos_ref[...] = jnp.cos(ang)
    sin_ref[...] = jnp.sin(ang)


def _rope_tables(positions):
    inv_freq = ROPE_THETA ** (-jnp.arange(0, ROT_DIM, 2, dtype=F32) / ROT_DIM)
    pos_rep = jnp.repeat(positions.reshape(-1), ROT_HALF).reshape(-1, LANES)
    invf = jnp.tile(inv_freq, TOK_PER_ROW).reshape(1, LANES)
    return pl.pallas_call(
        _rope_tab_kernel,
        out_shape=(jax.ShapeDtypeStruct(pos_rep.shape, F32),) * 2,
        name="rope_tables",
    )(pos_rep, invf)


def _expand_rope(tab, tm):
    rows = jnp.broadcast_to(tab[:, None, :], (tm // TOK_PER_ROW, TOK_PER_ROW, LANES)).reshape(tm, LANES)
    tok = lax.broadcasted_iota(jnp.int32, (tm, LANES), 0) % TOK_PER_ROW
    lane = lax.broadcasted_iota(jnp.int32, (tm, LANES), 1)
    idx = tok * ROT_HALF + lane % ROT_HALF
    return jnp.take_along_axis(rows, idx, axis=1, mode="promise_in_bounds")


QKV_DOT_WIDTH = 1536


def _run_width(segs, c0):
    starts = {s[0]: s[1] for s in segs}
    width = 0
    while c0 + width in starts and width + starts[c0 + width] <= QKV_DOT_WIDTH:
        width += starts[c0 + width]
    return width


def _qkv_kernel(segs, tm, scales, x_ref, g_ref, b_ref, w_ref, cos_ref, sin_ref, *rest):
    n_cast = len(scales)
    cast_in, rest = rest[:n_cast], rest[n_cast:]
    h_ref, out_refs, cast_out, stage = rest[0], rest[1:1 + len(segs)], rest[1 + len(segs):-1], rest[-1]
    h = _ln(x_ref[...], g_ref[...], b_ref[...])
    h_ref[...] = h
    hb = h.astype(BF16)
    for scale, wi_ref, wo_ref in zip(scales, cast_in, cast_out):
        wo_ref[...] = (wi_ref[...] * scale if scale != 1.0 else wi_ref[...]).astype(BF16)
    slot = lax.broadcasted_iota(jnp.int32, (tm, LANES), 1) % HEAD_DIM
    cos = _expand_rope(cos_ref[...], tm)
    sin = _expand_rope(sin_ref[...], tm)
    is_t1 = slot < ROT_HALF
    c_tab = jnp.where(slot < ROT_DIM, cos, 1.0)
    s_tab = jnp.where(is_t1, -sin, jnp.where(slot < ROT_DIM, sin, 0.0))
    tables = {scale: (c_tab, s_tab) if scale == 1.0 else (c_tab * scale, s_tab * scale)
              for scale in sorted({seg[4] for seg in segs if seg[2]})}
    lo = slot == lax.broadcasted_iota(jnp.int32, (tm, LANES), 1)
    z, z0 = None, 0
    for (c0, width, rope, dil, scale, dup), o_ref in zip(segs, out_refs):
        if z is None or c0 + width > z0 + z.shape[1]:
            z0 = c0
            z = _dot(hb, w_ref[:, z0:z0 + _run_width(segs, c0)])
        for j in range(width // LANES):
            zc = z[:, c0 - z0 + j * LANES:c0 - z0 + (j + 1) * LANES]
            if rope:
                up = pltpu.roll(zc, LANES - ROT_HALF, 1)
                dn = pltpu.roll(zc, ROT_HALF, 1)
                c_s, s_s = tables[scale]
                zc = zc * c_s + jnp.where(is_t1, up, dn) * s_s
            elif scale != 1.0:
                zc = zc * scale
            if dup:
                sw = pltpu.roll(zc, HEAD_DIM, 1)
                o_ref[:, 2 * j * LANES:(2 * j + 1) * LANES] = jnp.where(lo, zc, sw).astype(BF16)
                o_ref[:, (2 * j + 1) * LANES:(2 * j + 2) * LANES] = jnp.where(lo, sw, zc).astype(BF16)
            elif dil == 1:
                o_ref[:, j * LANES:(j + 1) * LANES] = zc.astype(BF16)
            else:
                stage[...] = zc
                for c in range(dil):
                    col = c * width + j * LANES
                    o_ref[:, col:col + LANES] = stage[pl.ds(c, tm // dil, stride=dil), :].astype(BF16)


BF16_ROWS = 16


def _qkv_proj(x2, ln_g, ln_b, w_all, segs, cos, sin, to_cast, tm):
    t = x2.shape[0]
    steps = t // tm
    wtot = w_all.shape[1]
    row = lambda r, w: pl.BlockSpec((r, w), lambda i: (i, 0))
    out_shape = [jax.ShapeDtypeStruct((t, D_MODEL), F32)]
    out_specs = [row(tm, D_MODEL)]
    for _, width, _, dil, _, dup in segs:
        wout = 2 * width if dup else dil * width
        out_shape.append(jax.ShapeDtypeStruct((t // dil, wout), BF16))
        out_specs.append(row(tm // dil, wout))
    cast_specs = []
    for w, _ in to_cast:
        rows, cols = w.shape
        nblk = max(n for n in range(1, steps + 1) if steps % n == 0 and rows % (n * BF16_ROWS) == 0)
        spec = pl.BlockSpec((rows // nblk, cols), lambda i, every=steps // nblk: (i // every, 0))
        cast_specs.append(spec)
        out_shape.append(jax.ShapeDtypeStruct((rows, cols), BF16))
        out_specs.append(spec)
    return pl.pallas_call(
        functools.partial(_qkv_kernel, segs, tm, tuple(s for _, s in to_cast)),
        grid=(steps,),
        in_specs=[row(tm, D_MODEL), _const_spec((1, D_MODEL)), _const_spec((1, D_MODEL)),
                  _const_spec((D_MODEL, wtot)), row(tm // TOK_PER_ROW, LANES), row(tm // TOK_PER_ROW, LANES)]
                 + cast_specs,
        out_specs=out_specs,
        out_shape=out_shape,
        scratch_shapes=[pltpu.VMEM((tm, LANES), F32)],
        compiler_params=_params(("arbitrary",)),
        name="qkv_proj",
    )(x2, ln_g, ln_b, w_all, cos, sin, *[w for w, _ in to_cast])


def _band_bias(nq, nk, side):
    qi = lax.broadcasted_iota(jnp.int32, (nq, nk), 0)
    kj = lax.broadcasted_iota(jnp.int32, (nq, nk), 1)
    rel = kj - side - qi
    return jnp.where((rel >= -side) & (rel <= side), 0.0, NEG_INF).astype(F32)


def _halo_rows(parts, lo, hi, cols):
    prev, main, nxt = parts
    side, tq = prev.shape[0], main.shape[0]
    pieces = []
    if lo < 0:
        pieces.append(prev[side + lo:side, cols])
    pieces.append(main[max(lo, 0):min(hi, tq), cols])
    if hi > tq:
        pieces.append(nxt[0:hi - tq, cols])
    return pieces[0] if len(pieces) == 1 else jnp.concatenate(pieces, 0)


def _band_units(side, dil, tq, tile, n_tiles, sink_ref, q_ref, k_parts, v_parts, emit, lse_w=0):
    is_win = sink_ref is not None
    blk = Q_BLK
    nk = blk + 2 * side
    n_blk = tq // blk
    width = q_ref.shape[-1] // dil
    n_pairs = width // LANES
    lo = lax.broadcasted_iota(jnp.int32, (blk, LANES), 1) < HEAD_DIM
    lane8 = lax.broadcasted_iota(jnp.int32, (blk, max(lse_w, 1)), 1)
    band = _band_bias(blk, nk, side)
    kj = lax.broadcasted_iota(jnp.int32, (blk, nk), 1)
    band_first = jnp.where((tile == 0) & (kj < side), NEG_INF, band)
    band_last = jnp.where((tile == n_tiles - 1) & (kj >= side + blk), NEG_INF, band)
    biases = []
    for j in range(n_blk):
        bias = band
        if j == 0:
            bias = band_first
        if j == n_blk - 1:
            bias = band_last if j > 0 else jnp.minimum(band_first, band_last)
        biases.append(jnp.concatenate([bias, bias], 0))

    units = [(j, c, pr) for j in range(n_blk) for c in range(dil) for pr in range(n_pairs)]

    def kv_cols(c, pr):
        if is_win:
            return slice((pr // 2) * LANES, (pr // 2 + 1) * LANES)
        return slice(c * width + pr * LANES, c * width + (pr + 1) * LANES)

    def scores(j, c, pr):
        r0 = j * blk
        q2 = q_ref[r0:r0 + blk, c * width + pr * LANES:c * width + (pr + 1) * LANES]
        zero = jnp.zeros_like(q2)
        qs = jnp.concatenate([jnp.where(lo, q2, zero), jnp.where(lo, zero, q2)], 0)
        keys = _halo_rows(k_parts, r0 - side, r0 + blk + side, kv_cols(c, pr))
        return _dot_nt(qs, keys) + biases[j]

    def softmax(s, pr):
        ps, rdens, lses = [], [], []
        for hf in range(2):
            sh = s[hf * blk:(hf + 1) * blk]
            m = jnp.max(sh, -1, keepdims=True)
            if is_win:
                sink = sink_ref[2 * pr + hf] * LOG2E
                m = jnp.maximum(m, sink)
            ph = jnp.exp2(sh - m)
            denom = jnp.sum(ph, -1, keepdims=True)
            if is_win:
                denom = denom + jnp.exp2(sink - m)
            else:
                lses.append(m + jnp.log2(denom))
            ps.append(ph.astype(BF16))
            rdens.append(1.0 / denom)
        return jnp.concatenate(ps, 0), rdens, lses

    pairs, lse8 = [], None

    def finish(unit, o, rdens, lses):
        nonlocal pairs, lse8
        j, c, pr = unit
        pairs.append(jnp.where(lo, o[:blk] * rdens[0], o[blk:] * rdens[1]))
        if lse_w:
            if lse8 is None:
                lse8 = jnp.zeros((blk, lse_w), F32)
            lse8 = jnp.where(lane8 == 2 * pr, lses[0], lse8)
            lse8 = jnp.where(lane8 == 2 * pr + 1, lses[1], lse8)
        if pr == n_pairs - 1:
            emit(j, c, jnp.concatenate(pairs, -1), lse8)
            pairs, lse8 = [], None

    s_next = scores(*units[0])
    pending = None
    for n, (j, c, pr) in enumerate(units):
        s = s_next
        if n + 1 < len(units):
            s_next = scores(*units[n + 1])
        p, rdens, lses = softmax(s, pr)
        o = _dot(p, _halo_rows(v_parts, j * blk - side, (j + 1) * blk + side, kv_cols(c, pr)))
        if pending is not None:
            finish(*pending)
        pending = ((j, c, pr), o, rdens, lses)
    finish(*pending)


def _dil_kernel(dil, tq, q_ref, kp_ref, km_ref, kn_ref, vp_ref, vm_ref, vn_ref, o_ref, lse_ref):
    width = q_ref.shape[-1] // dil

    def emit(j, c, out, lse):
        rows = slice(j * Q_BLK, (j + 1) * Q_BLK)
        o_ref[0, rows, c * width:(c + 1) * width] = out.astype(BF16)
        lse_ref[0, c, rows, :] = lse

    _band_units(DIL_SIDE, dil, tq, pl.program_id(1), pl.num_programs(1), None, q_ref.at[0],
                (kp_ref.at[0], km_ref.at[0], kn_ref.at[0]), (vp_ref.at[0], vm_ref.at[0], vn_ref.at[0]), emit,
                lse_w=lse_ref.shape[-1])


def _band_specs(tq, side, n_rows, width):
    per = tq // side
    nblk = n_rows // side
    main = pl.BlockSpec((1, tq, width), lambda bi, i: (bi, i, 0))
    prev = pl.BlockSpec((1, side, width), lambda bi, i: (bi, jnp.maximum(i * per - 1, 0), 0))
    nxt = pl.BlockSpec((1, side, width), lambda bi, i: (bi, jnp.minimum((i + 1) * per, nblk - 1), 0))
    return prev, main, nxt


def _win_kernel(tq, sink_ref, q_ref, kp_ref, km_ref, kn_ref, vp_ref, vm_ref, vn_ref, g_ref, o_ref):

    def emit(j, c, out, lse):
        o_ref[0, j * Q_BLK:(j + 1) * Q_BLK, :] = _rms(out, g_ref[...]).astype(BF16)

    _band_units(WIN_HALF, 1, tq, pl.program_id(1), pl.num_programs(1), sink_ref, q_ref.at[0],
                (kp_ref.at[0], km_ref.at[0], kn_ref.at[0]), (vp_ref.at[0], vm_ref.at[0], vn_ref.at[0]), emit)


def _win_attn(qa, ka, va, sink, g_win, tq):
    b, s, _ = qa.shape
    side = WIN_HALF
    kw = ka.shape[-1]
    prev, main, nxt = _band_specs(tq, side, s, kw)
    qspec = pl.BlockSpec((1, tq, WIN_WIDTH), lambda bi, i: (bi, i, 0))
    return pl.pallas_call(
        functools.partial(_win_kernel, tq),
        grid=(b, s // tq),
        in_specs=[pl.BlockSpec(memory_space=pltpu.SMEM), qspec, prev, main, nxt, prev, main, nxt,
                  _const_spec((1, WIN_WIDTH))],
        out_specs=qspec,
        out_shape=jax.ShapeDtypeStruct((b, s, WIN_WIDTH), BF16),
        compiler_params=_params(("parallel", "parallel")),
        name="win_attn",
    )(sink, qa, ka, ka, ka, va, va, va, g_win)


def _dil_attn(qv, kv, vv, dil, tq):
    b, ln, wr = qv.shape
    side = DIL_SIDE
    lse_w = DIL_SLOTS if dil == 1 else LANES
    prev, main, nxt = _band_specs(tq, side, ln, wr)
    return pl.pallas_call(
        functools.partial(_dil_kernel, dil, tq),
        grid=(b, ln // tq),
        in_specs=[main, prev, main, nxt, prev, main, nxt],
        out_specs=[main, pl.BlockSpec((1, dil, tq, lse_w), lambda bi, i: (bi, 0, i, 0))],
        out_shape=[jax.ShapeDtypeStruct((b, ln, wr), BF16),
                   jax.ShapeDtypeStruct((b, dil, ln, lse_w), F32)],
        compiler_params=_params(("parallel", "parallel")),
        name=f"dil_attn_{dil}",
    )(qv, kv, kv, kv, vv, vv, vv)


def _mem_kernel(mem_ref, g_ref, b_ref, wk_ref, wv_ref, k_ref, v_ref):
    mn = _ln(mem_ref[0], g_ref[...], b_ref[...]).astype(BF16)
    k_ref[0] = _dot(mn, wk_ref[...]).astype(BF16)
    v_ref[0] = _dot(mn, wv_ref[...]).astype(BF16)


def _mem_kv(mem, g, b, wk, wv):
    bsz, m, d = mem.shape
    blk = pl.BlockSpec((1, m, d), lambda bi: (bi, 0, 0))
    return pl.pallas_call(
        _mem_kernel,
        grid=(bsz,),
        in_specs=[blk, _const_spec((1, d)), _const_spec((1, d)), _const_spec((d, d)), _const_spec((d, d))],
        out_specs=[blk, blk],
        out_shape=[jax.ShapeDtypeStruct((bsz, m, d), BF16)] * 2,
        compiler_params=_params(("parallel",)),
        name="mem_kv",
    )(mem, g, b, wk, wv)


def _mix_kernel(tm, oa_ref, o0_ref, o1_ref, o2_ref, l0_ref, l1_ref, l2_ref, h_ref, e_ref, gd_ref, wo_ref,
                g1_ref, b1_ref, wq_ref, kx_ref, vx_ref, wxo_ref, g2_ref, b2_ref, out_ref, stage, lstage):
    ts = tm // N_SUB
    n_chunks = DIL_WIDTH // LANES

    def sub_tile(k):
        rows = slice(k * ts, (k + 1) * ts)
        ls = []
        for gi, (l_ref, (_, dil)) in enumerate(zip((l0_ref, l1_ref, l2_ref), DIL_PAIRS)):
            if dil == 1:
                ls.append(l_ref[0, rows, :])
            else:
                rrows = slice(k * ts // dil, (k + 1) * ts // dil)
                for c in range(dil):
                    lstage[gi - 1, pl.ds(k * ts + c, ts // dil, stride=dil), :] = l_ref[0, c, rrows, :]
                ls.append(lstage[gi - 1, rows, 0:DIL_SLOTS])
        mx = jnp.maximum(jnp.maximum(ls[0], ls[1]), ls[2])
        es = [jnp.exp2(l - mx) for l in ls]
        tot = es[0] + es[1] + es[2]
        ob = None
        for gi, (e, o_ref, (_, dil)) in enumerate(zip(es, (o0_ref, o1_ref, o2_ref), DIL_PAIRS)):
            wexp = _dot((e / tot).astype(BF16), e_ref[...])
            if dil == 1:
                og = o_ref[0, rows, :].astype(F32)
            else:
                rrows = slice(k * ts // dil, (k + 1) * ts // dil)
                for c in range(dil):
                    for j in range(n_chunks):
                        col = c * DIL_WIDTH + j * LANES
                        stage[gi - 1, j, pl.ds(k * ts + c, ts // dil, stride=dil), :] = (
                            o_ref[0, rrows, col:col + LANES].astype(F32))
                og = jnp.concatenate([stage[gi - 1, j, rows, :] for j in range(n_chunks)], -1)
            term = wexp * og
            ob = term if ob is None else ob + term
        obn = _rms(ob, gd_ref[...]).astype(BF16)
        yield
        mix = _dot(jnp.concatenate([oa_ref[0, rows, :], obn], -1), wo_ref[...])
        yield
        h1 = _ln(DEEPNORM_ALPHA * h_ref[0, rows, :] + mix, g1_ref[...], b1_ref[...])
        yield
        q = _dot(h1.astype(BF16), wq_ref[...]).astype(BF16)
        yield
        heads = []
        for hh in range(X_HEADS):
            cols = slice(hh * X_HEAD_DIM, (hh + 1) * X_HEAD_DIM)
            s = _dot_nt(q[:, cols], kx_ref[0, :, cols])
            yield
            m = jnp.max(s, -1, keepdims=True)
            p = jnp.exp(s - m)
            denom = jnp.sum(p, -1, keepdims=True)
            yield
            heads.append((_dot(p.astype(BF16), vx_ref[0, :, cols]) / denom).astype(BF16))
        xa = _dot(jnp.concatenate(heads, -1), wxo_ref[...])
        yield
        out_ref[0, rows, :] = _ln(DEEPNORM_ALPHA * h1 + xa, g2_ref[...], b2_ref[...])

    _round_robin([sub_tile(k) for k in range(N_SUB)])


def _mix_xattn(oa, os_, ls, h, expand, g_dil, w_out, g1, b1, w_xq, kx, vx, w_xo, g2, b2, tm):
    b, s, d = h.shape
    m = kx.shape[1]
    tok = lambda w: pl.BlockSpec((1, tm, w), lambda bi, i: (bi, i, 0))
    res = lambda dil: pl.BlockSpec((1, tm // dil, dil * DIL_WIDTH), lambda bi, i: (bi, i, 0))
    memspec = pl.BlockSpec((1, m, d), lambda bi, i: (bi, 0, 0))
    return pl.pallas_call(
        functools.partial(_mix_kernel, tm),
        grid=(b, s // tm),
        in_specs=[tok(WIN_WIDTH)] + [res(dil) for _, dil in DIL_PAIRS] +
                 [tok(DIL_SLOTS) if dil == 1 else
                  pl.BlockSpec((1, dil, tm // dil, LANES), lambda bi, i: (bi, 0, i, 0)) for _, dil in DIL_PAIRS] +
                 [tok(d),
                  _const_spec(expand.shape), _const_spec((1, DIL_WIDTH)), _const_spec((d, d)),
                  _const_spec((1, d)), _const_spec((1, d)), _const_spec((d, d)), memspec, memspec,
                  _const_spec((d, d)), _const_spec((1, d)), _const_spec((1, d))],
        out_specs=tok(d),
        out_shape=jax.ShapeDtypeStruct((b, s, d), F32),
        scratch_shapes=[pltpu.VMEM((N_DIL - 1, DIL_WIDTH // LANES, tm, LANES), F32),
                        pltpu.VMEM((N_DIL - 1, tm, LANES), F32)],
        compiler_params=_params(("parallel", "parallel")),
        name="mix_xattn",
    )(oa, *os_, *ls, h, expand, g_dil, w_out, g1, b1, w_xq, kx, vx, w_xo, g2, b2)


FF_CHUNKS = ((0, D_FF),)
HALO = 16
CONV_PHASES = 4


def _ffn_kernel(tm, hp_ref, hm_ref, hn_ref, wg_ref, wu_ref, cw_ref, cb_ref, wd_ref, g3_ref, b3_ref, out_ref,
                gs, ys):
    i = pl.program_id(1)
    n = pl.num_programs(1)
    hm = hm_ref[0]
    hb = hm.astype(BF16)
    hp = jnp.where(i > 0, hp_ref[0], 0.0).astype(BF16)
    hn = jnp.where(i < n - 1, hn_ref[0], 0.0).astype(BF16)
    hcat = jnp.concatenate([hp, hb, hn], 0)
    q = tm // CONV_PHASES
    for sl in range(D_MODEL // LANES):
        ys[sl] = hm[:, sl * LANES:(sl + 1) * LANES]
    hb_phased = jnp.concatenate(
        [jnp.concatenate([ys[sl, pl.ds(ph, q, stride=CONV_PHASES), :] for sl in range(D_MODEL // LANES)], -1)
         for ph in range(CONV_PHASES)], 0).astype(BF16)
    acc = None
    for c0, width in FF_CHUNKS:
        gate = _dot(hcat, wg_ref[:, c0:c0 + width])
        up = _dot(hb_phased, wu_ref[:, c0:c0 + width])
        n_slabs = width // LANES
        for sl in range(n_slabs):
            gs[sl] = gate[:, sl * LANES:(sl + 1) * LANES]
        phases = []
        for ph in range(CONV_PHASES):
            slabs = []
            for sl in range(n_slabs):
                lanes = slice(c0 + sl * LANES, c0 + (sl + 1) * LANES)
                g = cb_ref[:, lanes]
                for j in range(CONV_WIDTH):
                    g = g + gs[sl, pl.ds(HALO - 1 + j + ph, q, stride=CONV_PHASES), :] * cw_ref[j:j + 1, lanes]
                u = up[ph * q:(ph + 1) * q, sl * LANES:(sl + 1) * LANES]
                slabs.append((0.5 * g * (1.0 + lax.erf(g * np.float32(np.sqrt(0.5)))) * u).astype(BF16))
            phases.append(jnp.concatenate(slabs, -1))
        part = _dot(jnp.concatenate(phases, 0), wd_ref[c0:c0 + width, :])
        acc = part if acc is None else acc + part
    for ph in range(CONV_PHASES):
        for sl in range(D_MODEL // LANES):
            ys[sl, pl.ds(ph, q, stride=CONV_PHASES), :] = acc[ph * q:(ph + 1) * q, sl * LANES:(sl + 1) * LANES]
    ff = jnp.concatenate([ys[sl] for sl in range(D_MODEL // LANES)], -1)
    out_ref[0] = _ln(DEEPNORM_ALPHA * hm + ff, g3_ref[...], b3_ref[...])


def _conv_glu(h2, wg, wu, cw, cb, wd, g3, b3, tm):
    b, s, d = h2.shape
    per = tm // HALO
    nblk = s // HALO
    main = pl.BlockSpec((1, tm, d), lambda bi, i: (bi, i, 0))
    prev = pl.BlockSpec((1, HALO, d), lambda bi, i: (bi, jnp.maximum(i * per - 1, 0), 0))
    nxt = pl.BlockSpec((1, HALO, d), lambda bi, i: (bi, jnp.minimum((i + 1) * per, nblk - 1), 0))
    n_slabs = max(w for _, w in FF_CHUNKS) // LANES
    return pl.pallas_call(
        functools.partial(_ffn_kernel, tm),
        grid=(b, s // tm),
        in_specs=[prev, main, nxt, _const_spec((d, D_FF)), _const_spec((d, D_FF)),
                  _const_spec((CONV_WIDTH, D_FF)), _const_spec((1, D_FF)), _const_spec((D_FF, d)),
                  _const_spec((1, d)), _const_spec((1, d))],
        out_specs=main,
        out_shape=jax.ShapeDtypeStruct((b, s, d), F32),
        scratch_shapes=[pltpu.VMEM((n_slabs, tm + 2 * HALO, LANES), F32), pltpu.VMEM((d // LANES, tm, LANES), F32)],
        compiler_params=_params(("parallel", "parallel")),
        name="conv_glu",
    )(h2, h2, h2, wg, wu, cw, cb, wd, g3, b3)


def kernel(x, mem, positions, ln_in_g, ln_in_b, w_in, attn_sink, g_win, g_dil, w_mix_out, ln1_g, ln1_b, mem_ln_g, mem_ln_b, w_xq, w_xk, w_xv, w_xo, ln2_g, ln2_b, w_gate, w_up, conv_w, conv_b, w_down, ln3_g, ln3_b):
    b, s, d = x.shape
    assert DEPTH == 1
    l = 0
    row = lambda v: v.reshape(1, -1)
    cos, sin = _rope_tables(positions)

    qscale = HEAD_DIM ** -0.5 * LOG2E
    o_ka, o_va, o_qb = A_Q, A_Q + A_KV, A_Q + 2 * A_KV
    o_kb, o_vb = o_qb + B_QKV, o_qb + 2 * B_QKV
    names = ["qa", "ka", "va"]
    segs = [(0, A_Q, True, 1, qscale, False),
            (o_ka, A_KV, True, 1, 1.0, True),
            (o_va, A_KV, False, 1, 1.0, True)]
    for kind, base, rope, scale in (("q", o_qb, True, qscale), ("k", o_kb, True, 1.0), ("v", o_vb, False, 1.0)):
        for gi, (_, dil) in enumerate(DIL_PAIRS):
            names.append(f"{kind}{gi}")
            segs.append((base + gi * DIL_WIDTH, DIL_WIDTH, rope, dil, scale, False))

    xscale = X_HEAD_DIM ** -0.5
    to_cast = [(w_mix_out[l], 1.0), (w_xq[l], xscale), (w_xk[l], 1.0), (w_xv[l], 1.0), (w_xo[l], 1.0),
               (w_gate[l], 1.0), (w_up[l], 1.0), (w_down[l], 1.0)]
    outs = _qkv_proj(x.reshape(b * s, d), row(ln_in_g), row(ln_in_b), w_in[l].astype(BF16), tuple(segs),
                     cos, sin, to_cast, tm=QKV_TILE)
    n_seg = 1 + len(segs)
    wo_b, wq_b, wk_b, wv_b, wxo_b, wg_b, wu_b, wd_b = outs[n_seg:]
    hn = outs[0].reshape(b, s, d)
    proj = {name: t.reshape(b, t.shape[0] // b, t.shape[1]) for name, t in zip(names, outs[1:n_seg])}
    qa, ka, va = proj["qa"], proj["ka"], proj["va"]
    oa = _win_attn(qa, ka, va, attn_sink[l], row(g_win[l]), tq=WIN_TILE)
    os_, ls = [], []
    for gi, (_, dil) in enumerate(DIL_PAIRS):
        qg, kg, vg = proj[f"q{gi}"], proj[f"k{gi}"], proj[f"v{gi}"]
        o, lse = _dil_attn(qg, kg, vg, dil, tq=DIL_TOKENS // dil)
        os_.append(o)
        ls.append(lse.reshape(b, s, DIL_SLOTS) if dil == 1 else lse)

    kx, vx = _mem_kv(mem, row(mem_ln_g[l]), row(mem_ln_b[l]), wk_b, wv_b)
    expand = jnp.repeat(jnp.eye(DIL_SLOTS, dtype=BF16), HEAD_DIM, axis=1)
    h2 = _mix_xattn(oa, os_, ls, hn, expand, row(g_dil[l]), wo_b, row(ln1_g[l]), row(ln1_b[l]), wq_b, kx, vx,
                    wxo_b, row(ln2_g[l]), row(ln2_b[l]), tm=MIX_TILE)
    return _conv_glu(h2, wg_b, wu_b, conv_w[l], row(conv_b[l]), wd_b, row(ln3_g[l]), row(ln3_b[l]), tm=FFN_TILE)
```
